```python
import math
import jax
import jax.numpy as jnp
from jax import lax
import numpy as np


D_MODEL = 1024
BATCH = 2
SEQ = 8192
DEPTH = 2

GRID_W = 64
CTX_LEN = 256
N_BRANCH = 3
BRANCH_W = D_MODEL // 2
CONV_K = 31
DN_HEADS = 4
DN_HEAD_K = BRANCH_W // DN_HEADS
DN_HEAD_V = BRANCH_W // DN_HEADS
DN_CHUNK = 64
SHORT_K = 3
DA_HEADS = 4
DA_HEAD = BRANCH_W // (2 * DA_HEADS)
ROPE_BASE = 10000.0
Q_BLOCK = 128
RMS_EPS = 1e-6
LN_EPS = 1e-5
IN_WIDTHS = (BRANCH_W, BRANCH_W, BRANCH_W,
             BRANCH_W, BRANCH_W, BRANCH_W, BRANCH_W,
             2 * DN_HEADS, 2 * DN_HEADS,
             BRANCH_W, BRANCH_W, BRANCH_W, BRANCH_W,
             N_BRANCH * D_MODEL)
IN_SPLITS = tuple(sum(IN_WIDTHS[:i + 1]) for i in range(len(IN_WIDTHS) - 1))
N_IN = sum(IN_WIDTHS)

kernel_name = 'hybrid_conv_deltanet_diffattn_prefix_dit'


def rms_norm(x, gain):
    xf = x.astype(jnp.float32)
    y = xf * lax.rsqrt(jnp.mean(xf * xf, axis=-1, keepdims=True) + RMS_EPS)
    return (y * gain.astype(jnp.float32)).astype(x.dtype)


def layer_norm(x, gain, bias):
    xf = x.astype(jnp.float32)
    mu = jnp.mean(xf, axis=-1, keepdims=True)
    xc = xf - mu
    y = xc * lax.rsqrt(jnp.mean(xc * xc, axis=-1, keepdims=True) + LN_EPS)
    return (y * gain.astype(jnp.float32) + bias.astype(jnp.float32)).astype(x.dtype)


def l2_norm(x):
    xf = x.astype(jnp.float32)
    return xf * lax.rsqrt(jnp.sum(xf * xf, axis=-1, keepdims=True) + RMS_EPS)


def depthwise_conv(x, w):
    pad = w.shape[0] // 2
    return lax.conv_general_dilated(x, w[:, None, :].astype(x.dtype), window_strides=(1,),
                                    padding=[(pad, pad)], dimension_numbers=('NWC', 'WIO', 'NWC'),
                                    feature_group_count=x.shape[-1])


def axial_rope_tables(n_rows):
    n_freq = DA_HEAD // 4
    inv_freq = ROPE_BASE ** (-jnp.arange(n_freq, dtype=jnp.float32) / n_freq)
    row = jnp.repeat(jnp.arange(n_rows, dtype=jnp.float32), GRID_W)
    col = jnp.tile(jnp.arange(GRID_W, dtype=jnp.float32), n_rows)
    ang = jnp.concatenate([row[:, None] * inv_freq, col[:, None] * inv_freq], axis=-1)
    ang = jnp.concatenate([ang, ang], axis=-1)
    return jnp.cos(ang), jnp.sin(ang)


def apply_rope(x, cos, sin):
    xf = x.astype(jnp.float32)
    half = xf.shape[-1] // 2
    rot = jnp.concatenate([-xf[..., half:], xf[..., :half]], axis=-1)
    return (xf * cos[:, None, None, :] + rot * sin[:, None, None, :]).astype(x.dtype)


def chunk_gated_delta(q, k, v, g, beta, s0):
    B, T, H, dk = q.shape
    dv = v.shape[-1]
    C = DN_CHUNK
    n = T // C

    def to_chunks(t):
        return jnp.transpose(t.astype(jnp.float32).reshape(B, n, C, H, -1), (1, 0, 3, 2, 4))

    q = to_chunks(q) * (dk ** -0.5)
    k = to_chunks(k)
    v = to_chunks(v)
    g = jnp.cumsum(to_chunks(g[..., None])[..., 0], axis=-1)
    beta = to_chunks(beta[..., None])[..., 0]
    lower = jnp.tril(jnp.ones((C, C), dtype=bool))
    strict = jnp.tril(jnp.ones((C, C), dtype=bool), -1)
    decay = jnp.where(lower, jnp.exp(jnp.where(lower, g[..., :, None] - g[..., None, :], 0.0)), 0.0)
    k_beta = k * beta[..., None]
    a = jnp.where(strict, jnp.einsum('nbhik,nbhjk->nbhij', k_beta, k) * decay, 0.0) + jnp.eye(C, dtype=jnp.float32)
    rhs = jnp.concatenate([v * beta[..., None], k_beta * jnp.exp(g)[..., None]], axis=-1)
    sol = lax.linalg.triangular_solve(a, rhs, left_side=True, lower=True, unit_diagonal=True)
    u, w = sol[..., :dv], sol[..., dv:]

    def step(S, xs):
        qc, kc, uc, wc, gc, dc = xs
        v_new = uc - jnp.einsum('bhck,bhkv->bhcv', wc, S)
        intra = jnp.einsum('bhik,bhjk->bhij', qc, kc) * dc
        o = jnp.einsum('bhck,bhkv->bhcv', qc * jnp.exp(gc)[..., None], S) + jnp.einsum('bhij,bhjv->bhiv', intra, v_new)
        g_last = gc[..., -1:]
        S = S * jnp.exp(g_last)[..., None] + jnp.einsum('bhck,bhcv->bhkv', kc * jnp.exp(g_last - gc)[..., None], v_new)
        return S, o

    s_final, o = lax.scan(step, s0.astype(jnp.float32), (q, k, u, w, g, decay))
    o = jnp.transpose(o, (1, 0, 3, 2, 4)).reshape(B, T, H, dv)
    return o, s_final


def delta_bidir(q, k, v, g, beta, s0_fwd, s0_bwd):
    o_f, s_f = chunk_gated_delta(q, k, v, g[:, :, 0], beta[:, :, 0], s0_fwd)
    rev = lambda t: jnp.flip(t, axis=1)
    o_b, s_b = chunk_gated_delta(rev(q), rev(k), rev(v), rev(g[:, :, 1]), rev(beta[:, :, 1]), s0_bwd)
    return o_f + rev(o_b), s_f, s_b


def diff_softmax_attend(q, K, V, lam):
    s = jnp.einsum('bqhmd,bkhmd->bhmqk', q, K).astype(jnp.float32)
    p = jax.nn.softmax(s, axis=-1)
    att = p[:, :, 0] - lam * p[:, :, 1]
    return jnp.einsum('bhqk,bkhe->bqhe', att.astype(V.dtype), V)


def diff_attention_latent(q, k, v, k_ctx, v_ctx, lam):
    B, S, H, _, d = q.shape
    q = q * (d ** -0.5)
    K = jnp.concatenate([k, k_ctx], axis=1)
    V = jnp.concatenate([v, v_ctx], axis=1)
    nb = S // Q_BLOCK
    qb = jnp.moveaxis(q.reshape(B, nb, Q_BLOCK, H, 2, d), 1, 0)
    o = lax.map(lambda qi: diff_softmax_attend(qi, K, V, lam), qb)
    return jnp.moveaxis(o, 0, 1).reshape(B, S, H, 2 * d)


def conv_module(val, glu, z, conv_w, conv_b, ln_g, ln_b):
    a = val * jax.nn.sigmoid(glu)
    a = depthwise_conv(a, conv_w) + conv_b
    return jax.nn.silu(layer_norm(a, ln_g, ln_b)) * jax.nn.silu(z)


def merge_branches(ys, gate_logits, w_branch, w_out):
    y = jnp.stack(ys, axis=2)
    proj = jnp.einsum('btnw,nwd->btnd', y, w_branch)
    gates = jax.nn.sigmoid(gate_logits.reshape(gate_logits.shape[0], gate_logits.shape[1], N_BRANCH, D_MODEL))
    return jnp.sum(gates * proj, axis=2) @ w_out


def trunk_layer(x, ctx, c, c_ctx, cos, sin, layer_idx, update_ctx,
                w_ada, b_ada, norm_g, w_in, conv_w, conv_b, conv_ln_g, conv_ln_b,
                dn_conv_w, dn_a_log, dn_dt_bias, dn_norm_g,
                da_q_norm_g, da_k_norm_g, da_lambda, da_subln_g, w_branch, w_out):
    B, S, _ = x.shape
    L = ctx.shape[1]
    shift, scale, gate = jnp.split(jax.nn.silu(c) @ w_ada + b_ada, 3, axis=-1)
    shift_c, scale_c, gate_c = jnp.split(jax.nn.silu(c_ctx) @ w_ada + b_ada, 3, axis=-1)
    h = rms_norm(x, norm_g) * (1.0 + scale[:, None]) + shift[:, None]
    h_c = rms_norm(ctx, norm_g) * (1.0 + scale_c) + shift_c
    (conv_val, conv_glu, conv_z, dq, dk, dv, dz, dbeta, ddecay,
     aq, ak, av, az, mgate) = jnp.split(h @ w_in, IN_SPLITS, axis=-1)
    (conv_val_c, conv_glu_c, conv_z_c, dq_c, dk_c, dv_c, dz_c, dbeta_c, ddecay_c,
     aq_c, ak_c, av_c, az_c, mgate_c) = jnp.split(h_c @ w_in, IN_SPLITS, axis=-1)

    def delta_inputs(q, k, v, beta_logit, decay_logit):
        T = q.shape[1]
        qkv = jax.nn.silu(depthwise_conv(jnp.concatenate([q, k, v], axis=-1), dn_conv_w))
        q, k, v = jnp.split(qkv, 3, axis=-1)
        q = l2_norm(q.reshape(B, T, DN_HEADS, DN_HEAD_K))
        k = l2_norm(k.reshape(B, T, DN_HEADS, DN_HEAD_K))
        v = v.reshape(B, T, DN_HEADS, DN_HEAD_V)
        beta = jax.nn.sigmoid(beta_logit.astype(jnp.float32)).reshape(B, T, 2, DN_HEADS)
        g = -jnp.exp(dn_a_log.astype(jnp.float32)) * jax.nn.softplus(
            decay_logit.astype(jnp.float32).reshape(B, T, 2, DN_HEADS) + dn_dt_bias.astype(jnp.float32))
        return q, k, v, g, beta

    s_zero = jnp.zeros((B, DN_HEADS, DN_HEAD_K, DN_HEAD_V), jnp.float32)
    o_dn_c, s_fwd, s_bwd = delta_bidir(*delta_inputs(dq_c, dk_c, dv_c, dbeta_c, ddecay_c), s_zero, s_zero)
    o_dn, _, _ = delta_bidir(*delta_inputs(dq, dk, dv, dbeta, ddecay), s_fwd, s_bwd)

    def delta_out(o, z):
        T = o.shape[1]
        return rms_norm(o, dn_norm_g).reshape(B, T, BRANCH_W).astype(z.dtype) * jax.nn.silu(z)

    lam_init = 0.8 - 0.6 * math.exp(-0.3 * layer_idx)
    lam_f = da_lambda.astype(jnp.float32)
    lam = jnp.exp(jnp.sum(lam_f[0] * lam_f[1])) - jnp.exp(jnp.sum(lam_f[2] * lam_f[3])) + lam_init

    def qk_heads(t, gain):
        return rms_norm(t.reshape(B, t.shape[1], DA_HEADS, 2, DA_HEAD), gain)

    k_ctx = qk_heads(ak_c, da_k_norm_g)
    v_ctx = av_c.reshape(B, L, DA_HEADS, 2 * DA_HEAD)
    q_lat = apply_rope(qk_heads(aq, da_q_norm_g), cos, sin)
    k_lat = apply_rope(qk_heads(ak, da_k_norm_g), cos, sin)
    v_lat = av.reshape(B, S, DA_HEADS, 2 * DA_HEAD)
    o_da = diff_attention_latent(q_lat, k_lat, v_lat, k_ctx, v_ctx, lam)

    def diff_out(o, z):
        T = o.shape[1]
        return (rms_norm(o, da_subln_g) * (1.0 - lam_init)).reshape(B, T, BRANCH_W) * jax.nn.silu(z)

    y_conv = conv_module(conv_val, conv_glu, conv_z, conv_w, conv_b, conv_ln_g, conv_ln_b)
    ys = (y_conv, delta_out(o_dn, dz), diff_out(o_da, az))
    x_new = x + gate[:, None] * merge_branches(ys, mgate, w_branch, w_out)

    if update_ctx:
        o_da_c = diff_softmax_attend(qk_heads(aq_c, da_q_norm_g) * (DA_HEAD ** -0.5), k_ctx, v_ctx, lam)
        ys_c = (conv_module(conv_val_c, conv_glu_c, conv_z_c, conv_w, conv_b, conv_ln_g, conv_ln_b),
                delta_out(o_dn_c, dz_c), diff_out(o_da_c, az_c))
        ctx = ctx + gate_c * merge_branches(ys_c, mgate_c, w_branch, w_out)
    return x_new, ctx


def setup_inputs(seed: int = 0) -> dict:
    key = jax.random.key(seed)
    ks = jax.random.split(key, 24)
    f32 = jnp.float32
    nrm = lambda k, shape, s: jax.random.normal(k, shape, f32) * s
    dt = jnp.exp(jax.random.uniform(ks[11], (DEPTH, 2, DN_HEADS), f32, math.log(0.001), math.log(0.1)))
    return {
        'x': nrm(ks[0], (BATCH, SEQ, D_MODEL), 1.0),
        'c': nrm(ks[1], (BATCH, D_MODEL), 1.0),
        'ctx': nrm(ks[2], (BATCH, CTX_LEN, D_MODEL), 1.0),
        'c_ctx': nrm(ks[3], (D_MODEL,), 1.0),
        'w_ada': nrm(ks[4], (DEPTH, D_MODEL, 3 * D_MODEL), 0.5 * D_MODEL ** -0.5),
        'b_ada': nrm(ks[5], (DEPTH, 3 * D_MODEL), 0.01),
        'norm_g': 1.0 + nrm(ks[6], (DEPTH, D_MODEL), 0.02),
        'w_in': nrm(ks[7], (DEPTH, D_MODEL, N_IN), D_MODEL ** -0.5),
        'conv_w': nrm(ks[8], (DEPTH, CONV_K, BRANCH_W), CONV_K ** -0.5),
        'conv_b': nrm(ks[9], (DEPTH, BRANCH_W), 0.01),
        'conv_ln_g': 1.0 + nrm(ks[10], (DEPTH, BRANCH_W), 0.02),
        'conv_ln_b': nrm(ks[12], (DEPTH, BRANCH_W), 0.01),
        'dn_conv_w': nrm(ks[13], (DEPTH, SHORT_K, 3 * BRANCH_W), SHORT_K ** -0.5),
        'dn_a_log': jnp.log(jax.random.uniform(ks[14], (DEPTH, 2, DN_HEADS), f32, 1.0, 16.0)),
        'dn_dt_bias': dt + jnp.log(-jnp.expm1(-dt)),
        'dn_norm_g': 1.0 + nrm(ks[15], (DEPTH, DN_HEAD_V), 0.02),
        'da_q_norm_g': 1.0 + nrm(ks[16], (DEPTH, DA_HEAD), 0.02),
        'da_k_norm_g': 1.0 + nrm(ks[17], (DEPTH, DA_HEAD), 0.02),
        'da_lambda': nrm(ks[18], (DEPTH, 4, DA_HEAD), 0.1),
        'da_subln_g': 1.0 + nrm(ks[19], (DEPTH, 2 * DA_HEAD), 0.02),
        'w_branch': nrm(ks[20], (DEPTH, N_BRANCH, BRANCH_W, D_MODEL), BRANCH_W ** -0.5),
        'w_out': nrm(ks[21], (DEPTH, D_MODEL, D_MODEL), D_MODEL ** -0.5),
    }


def reference(x, c, ctx, c_ctx, w_ada, b_ada, norm_g, w_in, conv_w, conv_b, conv_ln_g, conv_ln_b,
              dn_conv_w, dn_a_log, dn_dt_bias, dn_norm_g, da_q_norm_g, da_k_norm_g, da_lambda,
              da_subln_g, w_branch, w_out):
    n_rows = x.shape[1] // GRID_W
    cos, sin = axial_rope_tables(n_rows)
    for layer in range(DEPTH):
        x, ctx = trunk_layer(x, ctx, c, c_ctx, cos, sin, layer, layer < DEPTH - 1,
                             w_ada[layer], b_ada[layer], norm_g[layer], w_in[layer],
                             conv_w[layer], conv_b[layer], conv_ln_g[layer], conv_ln_b[layer],
                             dn_conv_w[layer], dn_a_log[layer], dn_dt_bias[layer], dn_norm_g[layer],
                             da_q_norm_g[layer], da_k_norm_g[layer], da_lambda[layer],
                             da_subln_g[layer], w_branch[layer], w_out[layer])
    return x
```

```python
import functools
import math

import jax
import jax.numpy as jnp
from jax import lax
from jax.experimental import pallas as pl
from jax.experimental.pallas import tpu as pltpu

F32 = jnp.float32
BF16 = jnp.bfloat16
HIGHEST = lax.Precision.HIGHEST

GRID_W = 64
N_BRANCH = 3
CONV_K = 31
DN_HEADS = 4
DN_CHUNK = 64
SHORT_K = 3
DA_HEADS = 4
ROPE_BASE = 10000.0
RMS_EPS = 1e-6
LN_EPS = 1e-5

LANES = 128
SUBLANES_BF16 = 16
HALO = SUBLANES_BF16
VMEM_LIMIT = 56 * 1024 * 1024

TM_PROJ = 256
PROJ_COLS = 512
CONV_ROWS = 64
TQ = 512
TK = 768


def _silu(x):
    return x * jax.nn.sigmoid(x)


def _softplus(x):
    return jnp.maximum(x, 0.0) + jnp.log(1.0 + jnp.exp(-jnp.abs(x)))


def _dot(a, b):
    return jnp.dot(a, b, preferred_element_type=F32)


def _dot_hi(a, b):
    return jnp.dot(a, b, preferred_element_type=F32, precision=HIGHEST)


def _dot_nt(a, b):
    return lax.dot_general(a, b, (((1,), (1,)), ((), ())), preferred_element_type=F32)


def _dot_tn(a, b):
    return lax.dot_general(a, b, (((0,), (0,)), ((), ())), preferred_element_type=F32)


def _params(*sem):
    return pltpu.CompilerParams(dimension_semantics=sem, vmem_limit_bytes=VMEM_LIMIT)


def _adaln_kernel(c_ref, w_ref, b_ref, o_ref):
    o_ref[...] = _dot_hi(_silu(c_ref[...]), w_ref[0]) + b_ref[0]


def _adaln(cvec, w_ada, b_ada3, layer):
    d = cvec.shape[1]
    return pl.pallas_call(
        _adaln_kernel,
        grid=(3,),
        in_specs=[
            pl.BlockSpec((8, d), lambda j: (0, 0)),
            pl.BlockSpec((1, d, d), lambda j: (layer, 0, j)),
            pl.BlockSpec((1, 1, d), lambda j: (layer, 0, j)),
        ],
        out_specs=pl.BlockSpec((8, d), lambda j: (0, j)),
        out_shape=jax.ShapeDtypeStruct((8, 3 * d), F32),
        compiler_params=_params("parallel"),
        name="adaln",
    )(cvec, w_ada, b_ada3)


def _inproj_kernel(x_ref, mod_ref, g_ref, wc, wd, wb, wa, wm, oc, od, ob, oa, om, *, n_lat_blocks, ctx_row):
    d = x_ref.shape[-1]
    b = pl.program_id(0)
    i = pl.program_id(1)
    row = jnp.where(i >= n_lat_blocks, ctx_row, b)
    m = mod_ref[pl.ds(row, 1), :]
    shift = m[:, 0:d]
    scale = m[:, d:2 * d]
    x = x_ref[0]
    ms = jnp.mean(x * x, axis=-1, keepdims=True)
    h = (x * lax.rsqrt(ms + RMS_EPS) * g_ref[0]) * (1.0 + scale) + shift
    hb = h.astype(BF16)
    for w, o in ((wc, oc), (wd, od), (wb, ob), (wa, oa), (wm, om)):
        n = w.shape[1]
        for c0 in range(0, n, PROJ_COLS):
            c1 = min(c0 + PROJ_COLS, n)
            o[0, :, c0:c1] = _dot(hb, w[:, c0:c1]).astype(o.dtype)


def _inproj(xs, mod, norm_g, weights, layer, n_lat_blocks, ctx_row):
    bsz, t, d = xs.shape
    nblk = t // TM_PROJ
    wc, wd, wb, wa, wm = weights
    resident = lambda w: pl.BlockSpec(w.shape, lambda b, i: (0, 0), pipeline_mode=pl.Buffered(1))
    out = lambda n: pl.BlockSpec((1, TM_PROJ, n), lambda b, i: (b, i, 0))
    return pl.pallas_call(
        functools.partial(_inproj_kernel, n_lat_blocks=n_lat_blocks, ctx_row=ctx_row),
        grid=(bsz, nblk),
        in_specs=[
            pl.BlockSpec((1, TM_PROJ, d), lambda b, i: (b, i, 0)),
            pl.BlockSpec(mod.shape, lambda b, i: (0, 0)),
            pl.BlockSpec((1, 1, d), lambda b, i: (layer, 0, 0)),
            resident(wc), resident(wd), resident(wb), resident(wa), resident(wm),
        ],
        out_specs=[out(wc.shape[1]), out(wd.shape[1]), out(wb.shape[1]), out(wa.shape[1]), out(wm.shape[1])],
        out_shape=[
            jax.ShapeDtypeStruct((bsz, t, wc.shape[1]), BF16),
            jax.ShapeDtypeStruct((bsz, t, wd.shape[1]), BF16),
            jax.ShapeDtypeStruct((bsz, t, wb.shape[1]), F32),
            jax.ShapeDtypeStruct((bsz, t, wa.shape[1]), BF16),
            jax.ShapeDtypeStruct((bsz, t, wm.shape[1]), BF16),
        ],
        compiler_params=_params("parallel", "parallel"),
        name="inproj",
    )(xs, mod, norm_g, wc, wd, wb, wa, wm)


def _segment_edges(i, n_lat, n_tot):
    has_prev = jnp.logical_and(i != 0, i != n_lat)
    has_next = jnp.logical_and(i != n_lat - 1, i != n_tot - 1)
    return has_prev, has_next


def _halo_specs(rows, width, col, n_halo_blocks):
    per = rows // HALO
    prev = pl.BlockSpec((1, HALO, width), lambda b, i: (b, jnp.maximum(i * per - 1, 0), col))
    nxt = pl.BlockSpec((1, HALO, width), lambda b, i: (b, jnp.minimum((i + 1) * per, n_halo_blocks - 1), col))
    return prev, nxt


def _conv_kernel(val, glu, z, pval, pglu, nval, nglu, w_ref, b_ref, lng, lnb, o_ref, buf, *, n_lat, n_tot):
    tm = val.shape[1]
    i = pl.program_id(1)
    has_prev, has_next = _segment_edges(i, n_lat, n_tot)

    def glu_act(v, g):
        return v.astype(F32) * jax.nn.sigmoid(g.astype(F32))

    buf[HALO:HALO + tm, :] = glu_act(val[0], glu[0])
    buf[0:HALO, :] = jnp.where(has_prev, glu_act(pval[0], pglu[0]), 0.0)
    buf[HALO + tm:HALO + tm + HALO, :] = jnp.where(has_next, glu_act(nval[0], nglu[0]), 0.0)
    pad = CONV_K // 2
    for r in range(0, tm, CONV_ROWS):
        acc = jnp.zeros((CONV_ROWS, buf.shape[1]), F32) + b_ref[0]
        for k in range(CONV_K):
            start = r + HALO - pad + k
            acc = acc + buf[start:start + CONV_ROWS, :] * w_ref[0, k:k + 1, :]
        mu = jnp.mean(acc, axis=-1, keepdims=True)
        xc = acc - mu
        y = xc * lax.rsqrt(jnp.mean(xc * xc, axis=-1, keepdims=True) + LN_EPS)
        y = y * lng[0] + lnb[0]
        zz = z[0, r:r + CONV_ROWS, :].astype(F32)
        o_ref[0, r:r + CONV_ROWS, :] = (_silu(y) * _silu(zz)).astype(o_ref.dtype)


def _conv_module(pc, conv_w, conv_b, ln_g, ln_b, layer, n_lat, n_tot):
    bsz, t, _ = pc.shape
    w = conv_w.shape[-1]
    tm = TM_PROJ
    main = lambda col: pl.BlockSpec((1, tm, w), lambda b, i: (b, i, col))
    pv, nv = _halo_specs(tm, w, 0, t // HALO)
    pg, ng = _halo_specs(tm, w, 1, t // HALO)
    vec = pl.BlockSpec((1, 1, w), lambda b, i: (layer, 0, 0))
    return pl.pallas_call(
        functools.partial(_conv_kernel, n_lat=n_lat, n_tot=n_tot),
        grid=(bsz, n_tot),
        in_specs=[main(0), main(1), main(2), pv, pg, nv, ng,
                  pl.BlockSpec((1, CONV_K, w), lambda b, i: (layer, 0, 0)), vec, vec, vec],
        out_specs=pl.BlockSpec((1, tm, w), lambda b, i: (b, i, 0)),
        out_shape=jax.ShapeDtypeStruct((bsz, n_tot * tm, w), BF16),
        scratch_shapes=[pltpu.VMEM((tm + 2 * HALO, w), F32)],
        compiler_params=_params("parallel", "parallel"),
        name="conv_module",
    )(pc, pc, pc, pc, pc, pc, pc, conv_w, conv_b, ln_g, ln_b)


def _dna_kernel(q, k, v, pq, pk, pv, nq, nk, nv, bd, bdt, cw, alog_r, bias_r, alog_c, bias_c,
                qg_o, kg_o, u_o, w_o, in_o, sd_o, buf, *, n_lat, n_tot):
    c = pl.program_id(1)
    has_prev, has_next = _segment_edges(c, n_lat, n_tot)
    cc = DN_CHUNK
    bw = q.shape[-1]
    hd = bw // DN_HEADS
    for idx, (m_, p_, n_) in enumerate(((q, pq, nq), (k, pk, nk), (v, pv, nv))):
        cols = slice(idx * bw, (idx + 1) * bw)
        buf[HALO:HALO + cc, cols] = m_[0].astype(F32)
        buf[0:HALO, cols] = jnp.where(has_prev, p_[0].astype(F32), 0.0)
        buf[HALO + cc:HALO + cc + HALO, cols] = jnp.where(has_next, n_[0].astype(F32), 0.0)
    conv = (buf[HALO - 1:HALO - 1 + cc, :] * cw[0, 0:1, :] + buf[HALO:HALO + cc, :] * cw[0, 1:2, :]
            + buf[HALO + 1:HALO + 1 + cc, :] * cw[0, 2:3, :])
    act = _silu(conv)

    x = bd[0]
    beta_c = jax.nn.sigmoid(x)
    g_c = -jnp.exp(alog_r[0]) * _softplus(x + bias_r[0])
    xt = bdt[0, 0]
    g_r = -jnp.exp(alog_c[0]) * _softplus(xt + bias_c[0])

    ii = lax.broadcasted_iota(jnp.int32, (cc, cc), 0)
    jj = lax.broadcasted_iota(jnp.int32, (cc, cc), 1)
    low = jj <= ii
    upp = jj >= ii
    lowf = low.astype(F32)
    uppf = upp.astype(F32)
    eye = (ii == jj).astype(F32)
    gcum_c = (_dot_hi(lowf, g_c), _dot_hi(uppf, g_c))
    gcum_r = (_dot_hi(g_r, uppf), _dot_hi(g_r, lowf))
    incl = (low, upp)
    strict = (jj < ii, jj > ii)
    last_row = (cc - 1, 0)
    nh = DN_HEADS
    sd_rows = [[], []]
    for h in range(nh):
        hs = slice(h * hd, (h + 1) * hd)
        qh = act[:, h * hd:(h + 1) * hd]
        kh = act[:, bw + h * hd:bw + (h + 1) * hd]
        vh = act[:, 2 * bw + h * hd:2 * bw + (h + 1) * hd]
        qh = qh * lax.rsqrt(jnp.sum(qh * qh, axis=-1, keepdims=True) + RMS_EPS) * (hd ** -0.5)
        kh = kh * lax.rsqrt(jnp.sum(kh * kh, axis=-1, keepdims=True) + RMS_EPS)
        kb16 = kh.astype(BF16)
        kk = _dot_nt(kb16, kb16)
        qk = _dot_nt(qh.astype(BF16), kb16)
        for d in range(2):
            col = 2 * nh + nh * d + h
            gi = gcum_c[d][:, col:col + 1]
            gj = gcum_r[d][col:col + 1, :]
            dm = jnp.where(incl[d], jnp.exp(jnp.where(incl[d], gi - gj, 0.0)), 0.0)
            beta = beta_c[:, nh * d + h:nh * d + h + 1]
            nmat = jnp.where(strict[d], beta * kk * dm, 0.0)
            pw = nmat
            ainv = eye - nmat
            for _ in range(int(math.log2(cc)) - 1):
                pw = _dot_hi(pw, pw)
                ainv = ainv + _dot_hi(ainv, pw)
            egi = jnp.exp(gi)
            rhs = jnp.concatenate([vh * beta, kh * (beta * egi)], axis=1)
            sol = _dot_hi(ainv, rhs)
            glast = gi[last_row[d]:last_row[d] + 1, :]
            u_o[d, 0, :, hs] = sol[:, 0:hd].astype(u_o.dtype)
            w_o[d, 0, :, hs] = sol[:, hd:2 * hd].astype(w_o.dtype)
            qg_o[d, 0, :, hs] = (qh * egi).astype(qg_o.dtype)
            kg_o[d, 0, :, hs] = (kh * jnp.exp(glast - gi)).astype(kg_o.dtype)
            in_o[d, 0, :, h * cc:(h + 1) * cc] = (qk * dm).astype(in_o.dtype)
            sd_rows[d].append(jnp.broadcast_to(jnp.exp(glast), (1, LANES)))
    for d in range(2):
        sd_o[d, 0, 0] = jnp.concatenate(sd_rows[d] + [jnp.zeros((8 - nh, LANES), F32)], axis=0)


def _deltanet_stage_a(pd, pbd, pbdt, dn_conv_w, alog_r, bias_r, alog_c, bias_c, layer, n_lat, n_tot):
    bsz, t, _ = pd.shape
    bw = dn_conv_w.shape[-1] // 3
    cc = DN_CHUNK
    main = lambda col: pl.BlockSpec((1, cc, bw), lambda b, c: (b, c, col))
    halos = [_halo_specs(cc, bw, col, t // HALO) for col in range(3)]
    small = lambda a: pl.BlockSpec((1,) + a.shape[1:], lambda b, c: (layer,) + (0,) * (a.ndim - 1))
    tok = lambda n, dt: jax.ShapeDtypeStruct((2, bsz, t, n), dt)
    tok_spec = lambda n: pl.BlockSpec((2, 1, cc, n), lambda b, c: (0, b, c, 0))
    return pl.pallas_call(
        functools.partial(_dna_kernel, n_lat=n_lat, n_tot=n_tot),
        grid=(bsz, n_tot),
        in_specs=[main(0), main(1), main(2),
                  halos[0][0], halos[1][0], halos[2][0], halos[0][1], halos[1][1], halos[2][1],
                  pl.BlockSpec((1, cc, LANES), lambda b, c: (b, c, 0)),
                  pl.BlockSpec((1, 1, 4 * DN_HEADS, cc), lambda b, c: (b, c, 0, 0)),
                  small(dn_conv_w), small(alog_r), small(bias_r), small(alog_c), small(bias_c)],
        out_specs=[tok_spec(bw), tok_spec(bw), tok_spec(bw), tok_spec(bw), tok_spec(DN_HEADS * cc),
                   pl.BlockSpec((2, 1, 1, 8, LANES), lambda b, c: (0, b, c, 0, 0))],
        out_shape=[tok(bw, BF16), tok(bw, BF16), tok(bw, BF16), tok(bw, BF16), tok(DN_HEADS * cc, BF16),
                   jax.ShapeDtypeStruct((2, bsz, n_tot, 8, LANES), F32)],
        scratch_shapes=[pltpu.VMEM((cc + 2 * HALO, 3 * bw), F32)],
        compiler_params=_params("parallel", "parallel"),
        name="deltanet_a",
    )(pd, pd, pd, pd, pd, pd, pd, pd, pd, pbd, pbdt, dn_conv_w, alog_r, bias_r, alog_c, bias_c)


def _dnb_kernel(qg, kg, u, w, intra, sd, o_ref, state):
    s = pl.program_id(2)
    cc = DN_CHUNK
    hd = state.shape[-1]

    @pl.when(s == 0)
    def _():
        state[...] = jnp.zeros_like(state)

    for h in range(DN_HEADS):
        hs = slice(h * hd, (h + 1) * hd)
        sh = state[h]
        sb = sh.astype(BF16)
        vnew = u[0, 0, :, hs].astype(F32) - _dot(w[0, 0, :, hs], sb)
        vb = vnew.astype(BF16)
        o = _dot(qg[0, 0, :, hs], sb) + _dot(intra[0, 0, :, h * cc:(h + 1) * cc], vb)
        o_ref[0, 0, :, hs] = o.astype(o_ref.dtype)
        state[h] = sh * sd[0, 0, 0, h:h + 1, :] + _dot_tn(kg[0, 0, :, hs], vb)


def _deltanet_stage_b(qg, kg, u, w, intra, sd, n_lat, n_tot):
    _, bsz, t, bw = qg.shape
    cc = DN_CHUNK
    hd = bw // DN_HEADS
    n_ctx = n_tot - n_lat

    def chunk(d, s):
        fwd = jnp.where(s < n_ctx, n_lat + s, s - n_ctx)
        return jnp.where(d == 0, fwd, n_tot - 1 - s)

    tok = lambda n: pl.BlockSpec((1, 1, cc, n), lambda b, d, s: (d, b, chunk(d, s), 0))
    return pl.pallas_call(
        _dnb_kernel,
        grid=(bsz, 2, n_tot),
        in_specs=[tok(bw), tok(bw), tok(bw), tok(bw), tok(DN_HEADS * cc),
                  pl.BlockSpec((1, 1, 1, 8, LANES), lambda b, d, s: (d, b, chunk(d, s), 0, 0))],
        out_specs=tok(bw),
        out_shape=jax.ShapeDtypeStruct((2, bsz, t, bw), BF16),
        scratch_shapes=[pltpu.VMEM((DN_HEADS, hd, hd), F32)],
        compiler_params=_params("parallel", "parallel", "arbitrary"),
        name="deltanet_b",
    )(qg, kg, u, w, intra, sd)


def _qkprep_kernel(q, k, cos, sin_a, sin_b, gq, gk, qo, ko):
    dh = LANES // 2
    r = lax.broadcasted_iota(jnp.int32, (LANES, LANES), 0) // dh
    c = lax.broadcasted_iota(jnp.int32, (LANES, LANES), 1) // dh
    seg = (r == c).astype(BF16)
    cs, sa, sb = cos[...], sin_a[...], sin_b[...]

    def norm_rope(x, gain, scale):
        sq = x * x
        hi = sq.astype(BF16)
        lo = (sq - hi.astype(F32)).astype(BF16)
        ms = (_dot(hi, seg) + _dot(lo, seg)) * (1.0 / dh)
        y = x * lax.rsqrt(ms + RMS_EPS) * gain
        y = y * cs + pltpu.roll(y, LANES - dh // 2, 1) * sa + pltpu.roll(y, dh // 2, 1) * sb
        return y * scale

    for h in range(DA_HEADS):
        hs = slice(h * LANES, (h + 1) * LANES)
        qo[0, :, hs] = norm_rope(q[0, :, hs].astype(F32), gq[0], dh ** -0.5).astype(qo.dtype)
        ko[0, :, hs] = norm_rope(k[0, :, hs].astype(F32), gk[0], 1.0).astype(ko.dtype)


def _qk_prep(pa, cos, sin_a, sin_b, gq, gk, layer):
    bsz, t, _ = pa.shape
    bw = DA_HEADS * LANES
    tm = TM_PROJ
    tab = pl.BlockSpec((tm, LANES), lambda b, i: (i, 0))
    gain = pl.BlockSpec((1, 1, LANES), lambda b, i: (layer, 0, 0))
    return pl.pallas_call(
        _qkprep_kernel,
        grid=(bsz, t // tm),
        in_specs=[pl.BlockSpec((1, tm, bw), lambda b, i: (b, i, 0)),
                  pl.BlockSpec((1, tm, bw), lambda b, i: (b, i, 1)),
                  tab, tab, tab, gain, gain],
        out_specs=[pl.BlockSpec((1, tm, bw), lambda b, i: (b, i, 0))] * 2,
        out_shape=[jax.ShapeDtypeStruct((bsz, t, bw), BF16)] * 2,
        compiler_params=_params("parallel", "parallel"),
        name="qk_prep",
    )(pa, pa, cos, sin_a, sin_b, gq, gk)


def _attn_kernel(q_ref, k_ref, v_ref, z_ref, lam_ref, g_ref, o_ref, qs, m_s, l_s, acc, *, lam_init):
    tq = q_ref.shape[1]
    dh = LANES // 2
    ki = pl.program_id(3)

    @pl.when(ki == 0)
    def _():
        q = q_ref[0].astype(F32)
        lane = lax.broadcasted_iota(jnp.int32, q.shape, 1)
        qs[0:tq, :] = jnp.where(lane < dh, q, 0.0).astype(qs.dtype)
        qs[tq:2 * tq, :] = jnp.where(lane >= dh, q, 0.0).astype(qs.dtype)
        m_s[...] = jnp.full_like(m_s, -jnp.inf)
        l_s[...] = jnp.zeros_like(l_s)
        acc[...] = jnp.zeros_like(acc)

    s = _dot_nt(qs[...], k_ref[0])
    m_prev = m_s[...]
    m_new = jnp.maximum(m_prev, jnp.max(s, axis=1, keepdims=True))
    alpha = jnp.exp(m_prev - m_new)
    p = jnp.exp(s - m_new[:, 0:1])
    l_s[...] = alpha * l_s[...] + jnp.sum(p, axis=1, keepdims=True)
    acc[...] = alpha * acc[...] + _dot(p.astype(BF16), v_ref[0])
    m_s[...] = m_new

    @pl.when(ki == pl.num_programs(3) - 1)
    def _():
        lm = lam_ref[0]
        lam = (jnp.exp(jnp.sum(lm[0:1] * lm[1:2], axis=1, keepdims=True))
               - jnp.exp(jnp.sum(lm[2:3] * lm[3:4], axis=1, keepdims=True)) + lam_init)
        on = acc[...] / l_s[...]
        o = on[0:tq] - lam * on[tq:2 * tq]
        y = o * lax.rsqrt(jnp.mean(o * o, axis=-1, keepdims=True) + RMS_EPS) * g_ref[0] * (1.0 - lam_init)
        o_ref[0] = (y * _silu(z_ref[0].astype(F32))).astype(o_ref.dtype)


def _diff_attention(qn, kn, pa, da_lambda, subln_g, layer, lam_init, *, tq, tk, q_rows, q_off, k_rows, k_off):
    bsz = qn.shape[0]
    v_col = 2 * DA_HEADS
    z_col = 3 * DA_HEADS
    qb, kb = q_off // tq, k_off // tk
    return pl.pallas_call(
        functools.partial(_attn_kernel, lam_init=lam_init),
        grid=(bsz, DA_HEADS, q_rows // tq, k_rows // tk),
        in_specs=[pl.BlockSpec((1, tq, LANES), lambda b, h, i, j: (b, i + qb, h)),
                  pl.BlockSpec((1, tk, LANES), lambda b, h, i, j: (b, j + kb, h)),
                  pl.BlockSpec((1, tk, LANES), lambda b, h, i, j: (b, j + kb, v_col + h)),
                  pl.BlockSpec((1, tq, LANES), lambda b, h, i, j: (b, i + qb, z_col + h)),
                  pl.BlockSpec((1,) + da_lambda.shape[1:], lambda b, h, i, j: (layer, 0, 0)),
                  pl.BlockSpec((1, 1, LANES), lambda b, h, i, j: (layer, 0, 0))],
        out_specs=pl.BlockSpec((1, tq, LANES), lambda b, h, i, j: (b, i, h)),
        out_shape=jax.ShapeDtypeStruct((bsz, q_rows, DA_HEADS * LANES), BF16),
        scratch_shapes=[pltpu.VMEM((2 * tq, LANES), BF16),
                        pltpu.VMEM((2 * tq, LANES), F32),
                        pltpu.VMEM((2 * tq, LANES), F32),
                        pltpu.VMEM((2 * tq, LANES), F32)],
        compiler_params=_params("parallel", "parallel", "parallel", "arbitrary"),
        name="diff_attn",
    )(qn, kn, pa, pa, da_lambda, subln_g)


def _merge_kernel(x_ref, yc, of, ob, dz, ydl, ydc, mg, mod_ref, dng, wb, wo, o_ref, *, n_lat_blocks, ctx_row):
    d = x_ref.shape[-1]
    b = pl.program_id(0)
    i = pl.program_id(1)
    is_ctx = i >= n_lat_blocks
    row = jnp.where(is_ctx, ctx_row, b)
    gate = mod_ref[pl.ds(row, 1), 2 * d:3 * d]
    o = of[0, 0].astype(F32) + ob[0, 0].astype(F32)
    hd = dng.shape[-1]
    parts = []
    for h in range(DN_HEADS):
        oh = o[:, h * hd:(h + 1) * hd]
        parts.append(oh * lax.rsqrt(jnp.mean(oh * oh, axis=-1, keepdims=True) + RMS_EPS) * dng[0])
    ydn = (jnp.concatenate(parts, axis=1) * _silu(dz[0].astype(F32))).astype(BF16)
    yda = jnp.where(is_ctx, ydc[0], ydl[0])
    merged = (jax.nn.sigmoid(mg[0, :, 0:d].astype(F32)) * _dot(yc[0], wb[0, 0])
              + jax.nn.sigmoid(mg[0, :, d:2 * d].astype(F32)) * _dot(ydn, wb[0, 1])
              + jax.nn.sigmoid(mg[0, :, 2 * d:3 * d].astype(F32)) * _dot(yda, wb[0, 2]))
    o_ref[0] = x_ref[0] + gate * _dot(merged.astype(BF16), wo[0])


def _merge(xs, y_conv, o_dn, pd, yd_lat, yd_ctx, pm, mod, dn_norm_g, w_branch, w_out, layer,
           n_lat_blocks, n_blocks, ctx_row):
    bsz, _, d = xs.shape
    tm = TM_PROJ
    bw = y_conv.shape[-1]
    n_ctx_blocks = yd_ctx.shape[1] // tm
    tok = lambda n, col=0: pl.BlockSpec((1, tm, n), lambda b, i: (b, i, col))
    dn = lambda direction: pl.BlockSpec((1, 1, tm, bw), lambda b, i: (direction, b, i, 0))
    return pl.pallas_call(
        functools.partial(_merge_kernel, n_lat_blocks=n_lat_blocks, ctx_row=ctx_row),
        grid=(bsz, n_blocks),
        in_specs=[tok(d), tok(bw), dn(0), dn(1), tok(bw, 3),
                  pl.BlockSpec((1, tm, bw), lambda b, i: (b, jnp.minimum(i, n_lat_blocks - 1), 0)),
                  pl.BlockSpec((1, tm, bw),
                               lambda b, i: (b, jnp.clip(i - n_lat_blocks, 0, n_ctx_blocks - 1), 0)),
                  tok(N_BRANCH * d),
                  pl.BlockSpec(mod.shape, lambda b, i: (0, 0)),
                  pl.BlockSpec((1, 1, dn_norm_g.shape[-1]), lambda b, i: (layer, 0, 0)),
                  pl.BlockSpec((1,) + w_branch.shape[1:], lambda b, i: (layer, 0, 0, 0)),
                  pl.BlockSpec((1,) + w_out.shape[1:], lambda b, i: (layer, 0, 0))],
        out_specs=tok(d),
        out_shape=jax.ShapeDtypeStruct((bsz, n_blocks * tm, d), F32),
        compiler_params=_params("parallel", "parallel"),
        name="merge",
    )(xs, y_conv, o_dn, o_dn, pd, yd_lat, yd_ctx, pm, mod, dn_norm_g, w_branch, w_out)


def _rope_tables(seq, ctx_len, dh):
    n_freq = dh // 4
    inv_freq = ROPE_BASE ** (-jnp.arange(n_freq, dtype=F32) / n_freq)
    n_rows = seq // GRID_W
    row = jnp.repeat(jnp.arange(n_rows, dtype=F32), GRID_W)
    col = jnp.tile(jnp.arange(GRID_W, dtype=F32), n_rows)
    ang = jnp.concatenate([row[:, None] * inv_freq, col[:, None] * inv_freq], axis=-1)
    ang = jnp.concatenate([ang, ang], axis=-1)
    cos = jnp.tile(jnp.cos(ang), (1, LANES // dh))
    sin = jnp.tile(jnp.sin(ang), (1, LANES // dh))
    first_half = (jnp.arange(LANES) % dh) < dh // 2
    sin_a = jnp.where(first_half, -sin, 0.0)
    sin_b = jnp.where(first_half, 0.0, sin)
    pad = lambda tbl, fill: jnp.concatenate([tbl, jnp.full((ctx_len, LANES), fill, F32)], axis=0)
    return pad(cos, 1.0), pad(sin_a, 0.0), pad(sin_b, 0.0)


def kernel(x, c, ctx, c_ctx, w_ada, b_ada, norm_g, w_in, conv_w, conv_b, conv_ln_g, conv_ln_b, dn_conv_w,
           dn_a_log, dn_dt_bias, dn_norm_g, da_q_norm_g, da_k_norm_g, da_lambda, da_subln_g, w_branch, w_out):
    bsz, seq, d = x.shape
    ctx_len = ctx.shape[1]
    depth = w_in.shape[0]
    bw = d // 2
    dh = bw // (2 * DA_HEADS)
    t = seq + ctx_len
    assert 2 * dh == LANES and bw // DN_HEADS == LANES and bsz + 1 <= 8
    assert seq % TQ == 0 and t % TK == 0 and seq % TM_PROJ == 0 and ctx_len % TM_PROJ == 0
    n_lat_blocks, n_blocks = seq // TM_PROJ, t // TM_PROJ
    n_lat_chunks, n_chunks = seq // DN_CHUNK, t // DN_CHUNK
    ctx_row = bsz

    xs = jnp.concatenate([x, ctx], axis=1)
    cvec = jnp.concatenate([c, c_ctx[None, :], jnp.zeros((8 - bsz - 1, d), F32)], axis=0)
    cos, sin_a, sin_b = _rope_tables(seq, ctx_len, dh)

    e_conv, e_dn = 3 * bw, 7 * bw
    e_bd = e_dn + 4 * DN_HEADS
    e_da = e_bd + 4 * bw
    nbd = 4 * DN_HEADS
    row3 = lambda a: a.reshape(depth, 1, a.shape[-1])
    gate_row = lambda a: jnp.pad(a.reshape(depth, 1, 2 * DN_HEADS), ((0, 0), (0, 0), (2 * DN_HEADS, LANES - nbd)))
    gate_col = lambda a: jnp.pad(a.reshape(depth, 2 * DN_HEADS, 1), ((0, 0), (2 * DN_HEADS, 0), (0, 0)))
    alog_r, bias_r = gate_row(dn_a_log), gate_row(dn_dt_bias)
    alog_c, bias_c = gate_col(dn_a_log), gate_col(dn_dt_bias)
    tile2 = lambda a: row3(jnp.tile(a, (1, 2)))
    gq, gk = tile2(da_q_norm_g), tile2(da_k_norm_g)
    b_ada3, norm_g3 = row3(b_ada), row3(norm_g)
    conv_b3, ln_g3, ln_b3 = row3(conv_b), row3(conv_ln_g), row3(conv_ln_b)
    dng3, subln3 = row3(dn_norm_g), row3(da_subln_g)
    w_branch16, w_out16 = w_branch.astype(BF16), w_out.astype(BF16)

    for layer in range(depth):
        last = layer == depth - 1
        lam_init = 0.8 - 0.6 * math.exp(-0.3 * layer)
        wl = w_in[layer]
        weights = (wl[:, 0:e_conv].astype(BF16), wl[:, e_conv:e_dn].astype(BF16),
                   jnp.pad(wl[:, e_dn:e_bd], ((0, 0), (0, LANES - nbd))).astype(BF16),
                   wl[:, e_bd:e_da].astype(BF16), wl[:, e_da:].astype(BF16))
        mod = _adaln(cvec, w_ada, b_ada3, layer)
        pc, pd, pbd, pa, pm = _inproj(xs, mod, norm_g3, weights, layer, n_lat_blocks, ctx_row)

        out_blocks = n_lat_blocks if last else n_blocks
        y_conv = _conv_module(pc, conv_w, conv_b3, ln_g3, ln_b3, layer, n_lat_blocks, out_blocks)

        pbdt = jnp.swapaxes(pbd[:, :, 0:nbd].reshape(bsz, n_chunks, DN_CHUNK, nbd), 2, 3)
        qg, kg, u, w, intra, sd = _deltanet_stage_a(pd, pbd, pbdt, dn_conv_w, alog_r, bias_r, alog_c, bias_c,
                                                    layer, n_lat_chunks, n_chunks)
        o_dn = _deltanet_stage_b(qg, kg, u, w, intra, sd, n_lat_chunks, n_chunks)

        qn, kn = _qk_prep(pa, cos, sin_a, sin_b, gq, gk, layer)
        yd_lat = _diff_attention(qn, kn, pa, da_lambda, subln3, layer, lam_init,
                                 tq=TQ, tk=TK, q_rows=seq, q_off=0, k_rows=t, k_off=0)
        yd_ctx = _diff_attention(qn, kn, pa, da_lambda, subln3, layer, lam_init,
                                 tq=ctx_len, tk=ctx_len, q_rows=ctx_len, q_off=seq, k_rows=ctx_len, k_off=seq)

        xs = _merge(xs, y_conv, o_dn, pd, yd_lat, yd_ctx, pm, mod, dng3, w_branch16, w_out16, layer,
                    n_lat_blocks, out_blocks, ctx_row)
    return xs
```

```python
import functools
import math

import jax
import jax.numpy as jnp
from jax import lax
from jax.experimental import pallas as pl
from jax.experimental.pallas import tpu as pltpu

F32 = jnp.float32
BF16 = jnp.bfloat16
HIGHEST = lax.Precision.HIGHEST

GRID_W = 64
N_BRANCH = 3
CONV_K = 31
DN_HEADS = 4
DN_CHUNK = 64
SHORT_K = 3
DA_HEADS = 4
ROPE_BASE = 10000.0
RMS_EPS = 1e-6
LN_EPS = 1e-5

LANES = 128
SUBLANES_BF16 = 16
HALO = SUBLANES_BF16
VMEM_LIMIT = 56 * 1024 * 1024

TM_PROJ = 256
PROJ_COLS = 512
CONV_ROWS = 64
DNA_CHUNKS = 4
TQ = 512
ATT_KEYS = 1024
ATT_ROWS = 256


def _silu(x):
    return x * jax.nn.sigmoid(x)


def _softplus(x):
    return jnp.maximum(x, 0.0) + jnp.log(1.0 + jnp.exp(-jnp.abs(x)))


def _dot(a, b):
    return jnp.dot(a, b, preferred_element_type=F32)


def _dot_hi(a, b):
    return jnp.dot(a, b, preferred_element_type=F32, precision=HIGHEST)


def _dot_nt(a, b):
    return lax.dot_general(a, b, (((1,), (1,)), ((), ())), preferred_element_type=F32)


def _dot_tn(a, b):
    return lax.dot_general(a, b, (((0,), (0,)), ((), ())), preferred_element_type=F32)


def _params(*sem):
    return pltpu.CompilerParams(dimension_semantics=sem, vmem_limit_bytes=VMEM_LIMIT)


def _adaln_kernel(c_ref, w_ref, b_ref, o_ref):
    o_ref[...] = _dot_hi(_silu(c_ref[...]), w_ref[0]) + b_ref[0]


def _adaln(cvec, w_ada, b_ada3, layer):
    d = cvec.shape[1]
    return pl.pallas_call(
        _adaln_kernel,
        grid=(3,),
        in_specs=[
            pl.BlockSpec((8, d), lambda j: (0, 0)),
            pl.BlockSpec((1, d, d), lambda j: (layer, 0, j)),
            pl.BlockSpec((1, 1, d), lambda j: (layer, 0, j)),
        ],
        out_specs=pl.BlockSpec((8, d), lambda j: (0, j)),
        out_shape=jax.ShapeDtypeStruct((8, 3 * d), F32),
        compiler_params=_params("parallel"),
        name="adaln",
    )(cvec, w_ada, b_ada3)


def _inproj_kernel(x_ref, mod_ref, g_ref, wc, wd, wb, wa, wm, oc, od, ob, oa, om, *, n_lat_blocks, ctx_row):
    d = x_ref.shape[-1]
    b = pl.program_id(0)
    i = pl.program_id(1)
    row = jnp.where(i >= n_lat_blocks, ctx_row, b)
    m = mod_ref[pl.ds(row, 1), :]
    shift = m[:, 0:d]
    scale = m[:, d:2 * d]
    x = x_ref[0]
    ms = jnp.mean(x * x, axis=-1, keepdims=True)
    h = (x * lax.rsqrt(ms + RMS_EPS) * g_ref[0]) * (1.0 + scale) + shift
    hb = h.astype(BF16)
    for w, o in ((wc, oc), (wd, od), (wb, ob), (wa, oa), (wm, om)):
        n = w.shape[1]
        for c0 in range(0, n, PROJ_COLS):
            c1 = min(c0 + PROJ_COLS, n)
            o[0, :, c0:c1] = _dot(hb, w[:, c0:c1]).astype(o.dtype)


def _inproj(xs, mod, norm_g, weights, layer, n_lat_blocks, ctx_row):
    bsz, t, d = xs.shape
    nblk = t // TM_PROJ
    wc, wd, wb, wa, wm = weights
    resident = lambda w: pl.BlockSpec(w.shape, lambda b, i: (0, 0), pipeline_mode=pl.Buffered(1))
    out = lambda n: pl.BlockSpec((1, TM_PROJ, n), lambda b, i: (b, i, 0))
    return pl.pallas_call(
        functools.partial(_inproj_kernel, n_lat_blocks=n_lat_blocks, ctx_row=ctx_row),
        grid=(bsz, nblk),
        in_specs=[
            pl.BlockSpec((1, TM_PROJ, d), lambda b, i: (b, i, 0)),
            pl.BlockSpec(mod.shape, lambda b, i: (0, 0)),
            pl.BlockSpec((1, 1, d), lambda b, i: (layer, 0, 0)),
            resident(wc), resident(wd), resident(wb), resident(wa), resident(wm),
        ],
        out_specs=[out(wc.shape[1]), out(wd.shape[1]), out(wb.shape[1]), out(wa.shape[1]), out(wm.shape[1])],
        out_shape=[
            jax.ShapeDtypeStruct((bsz, t, wc.shape[1]), BF16),
            jax.ShapeDtypeStruct((bsz, t, wd.shape[1]), BF16),
            jax.ShapeDtypeStruct((bsz, t, wb.shape[1]), F32),
            jax.ShapeDtypeStruct((bsz, t, wa.shape[1]), BF16),
            jax.ShapeDtypeStruct((bsz, t, wm.shape[1]), BF16),
        ],
        compiler_params=_params("parallel", "parallel"),
        name="inproj",
    )(xs, mod, norm_g, wc, wd, wb, wa, wm)


def _segment_edges(i, n_lat, n_tot):
    has_prev = jnp.logical_and(i != 0, i != n_lat)
    has_next = jnp.logical_and(i != n_lat - 1, i != n_tot - 1)
    return has_prev, has_next


def _halo_specs(rows, width, col, n_halo_blocks):
    per = rows // HALO
    prev = pl.BlockSpec((1, HALO, width), lambda b, i: (b, jnp.maximum(i * per - 1, 0), col))
    nxt = pl.BlockSpec((1, HALO, width), lambda b, i: (b, jnp.minimum((i + 1) * per, n_halo_blocks - 1), col))
    return prev, nxt


def _conv_kernel(val, glu, z, pval, pglu, nval, nglu, w_ref, b_ref, lng, lnb, o_ref, buf, *, n_lat, n_tot):
    tm = val.shape[1]
    i = pl.program_id(1)
    has_prev, has_next = _segment_edges(i, n_lat, n_tot)

    def glu_act(v, g):
        return v.astype(F32) * jax.nn.sigmoid(g.astype(F32))

    buf[HALO:HALO + tm, :] = glu_act(val[0], glu[0])
    buf[0:HALO, :] = jnp.where(has_prev, glu_act(pval[0], pglu[0]), 0.0)
    buf[HALO + tm:HALO + tm + HALO, :] = jnp.where(has_next, glu_act(nval[0], nglu[0]), 0.0)
    pad = CONV_K // 2
    for r in range(0, tm, CONV_ROWS):
        acc = jnp.zeros((CONV_ROWS, buf.shape[1]), F32) + b_ref[0]
        for k in range(CONV_K):
            start = r + HALO - pad + k
            acc = acc + buf[start:start + CONV_ROWS, :] * w_ref[0, k:k + 1, :]
        mu = jnp.mean(acc, axis=-1, keepdims=True)
        xc = acc - mu
        y = xc * lax.rsqrt(jnp.mean(xc * xc, axis=-1, keepdims=True) + LN_EPS)
        y = y * lng[0] + lnb[0]
        zz = z[0, r:r + CONV_ROWS, :].astype(F32)
        o_ref[0, r:r + CONV_ROWS, :] = (_silu(y) * _silu(zz)).astype(o_ref.dtype)


def _conv_module(pc, conv_w, conv_b, ln_g, ln_b, layer, n_lat, n_tot):
    bsz, t, _ = pc.shape
    w = conv_w.shape[-1]
    tm = TM_PROJ
    main = lambda col: pl.BlockSpec((1, tm, w), lambda b, i: (b, i, col))
    pv, nv = _halo_specs(tm, w, 0, t // HALO)
    pg, ng = _halo_specs(tm, w, 1, t // HALO)
    vec = pl.BlockSpec((1, 1, w), lambda b, i: (layer, 0, 0))
    return pl.pallas_call(
        functools.partial(_conv_kernel, n_lat=n_lat, n_tot=n_tot),
        grid=(bsz, n_tot),
        in_specs=[main(0), main(1), main(2), pv, pg, nv, ng,
                  pl.BlockSpec((1, CONV_K, w), lambda b, i: (layer, 0, 0)), vec, vec, vec],
        out_specs=pl.BlockSpec((1, tm, w), lambda b, i: (b, i, 0)),
        out_shape=jax.ShapeDtypeStruct((bsz, n_tot * tm, w), BF16),
        scratch_shapes=[pltpu.VMEM((tm + 2 * HALO, w), F32)],
        compiler_params=_params("parallel", "parallel"),
        name="conv_module",
    )(pc, pc, pc, pc, pc, pc, pc, conv_w, conv_b, ln_g, ln_b)


def _tri_inverses(nmats, eye, blk16, blk32):
    b = lambda a: a.astype(BF16)
    nds = [jnp.where(blk16, n, 0.0) for n in nmats]
    ts = [eye - nd for nd in nds]
    pws = nds
    for _ in range(3):
        pwbs = [b(p) for p in pws]
        pws = [_dot(p, p) for p in pwbs]
        tbs = [b(t) for t in ts]
        pwbs = [b(p) for p in pws]
        prods = [_dot(t, p) for t, p in zip(tbs, pwbs)]
        ts = [t + pr for t, pr in zip(ts, prods)]
    level32 = jnp.logical_and(blk32, jnp.logical_not(blk16))
    level64 = jnp.logical_not(blk32)
    for level in (level32, level64):
        cs = [b(jnp.where(level, n, 0.0)) for n in nmats]
        tbs = [b(t) for t in ts]
        mids = [_dot(c, t) for c, t in zip(cs, tbs)]
        mids = [b(m) for m in mids]
        prods = [_dot(t, m) for t, m in zip(tbs, mids)]
        ts = [t - pr for t, pr in zip(ts, prods)]
    return ts


def _dna_kernel(q, k, v, pq, pk, pv, nq, nk, nv, bd, bdt, cw, alog_r, bias_r, alog_c, bias_c,
                qg_o, kg_o, u_o, w_o, in_o, sd_o, buf, *, n_lat, n_tot):
    i = pl.program_id(1)
    has_prev, has_next = _segment_edges(i, n_lat, n_tot)
    cc = DN_CHUNK
    rows = q.shape[1]
    bw = q.shape[-1]
    hd = bw // DN_HEADS
    nh = DN_HEADS
    for idx, (m_, p_, n_) in enumerate(((q, pq, nq), (k, pk, nk), (v, pv, nv))):
        cols = slice(idx * bw, (idx + 1) * bw)
        buf[HALO:HALO + rows, cols] = m_[0].astype(F32)
        buf[0:HALO, cols] = jnp.where(has_prev, p_[0].astype(F32), 0.0)
        buf[HALO + rows:HALO + rows + HALO, cols] = jnp.where(has_next, n_[0].astype(F32), 0.0)

    ii = lax.broadcasted_iota(jnp.int32, (cc, cc), 0)
    jj = lax.broadcasted_iota(jnp.int32, (cc, cc), 1)
    low = jj <= ii
    upp = jj >= ii
    lowf = low.astype(F32)
    uppf = upp.astype(F32)
    eye = (ii == jj).astype(F32)
    blk16 = (ii // 16) == (jj // 16)
    blk32 = (ii // 32) == (jj // 32)
    incl = (low, upp)
    strict = (jj < ii, jj > ii)
    last_row = (cc - 1, 0)

    n_ch = rows // cc
    beta_cs, gcum_cs, gcum_rs = [], [], []
    heads = []
    for ch in range(n_ch):
        r0 = ch * cc
        conv = (buf[HALO - 1 + r0:HALO - 1 + r0 + cc, :] * cw[0, 0:1, :]
                + buf[HALO + r0:HALO + r0 + cc, :] * cw[0, 1:2, :]
                + buf[HALO + 1 + r0:HALO + 1 + r0 + cc, :] * cw[0, 2:3, :])
        act = _silu(conv)
        x = bd[0, r0:r0 + cc, :]
        beta_cs.append(jax.nn.sigmoid(x))
        g_c = -jnp.exp(alog_r[0]) * _softplus(x + bias_r[0])
        xt = bdt[0, ch]
        g_r = -jnp.exp(alog_c[0]) * _softplus(xt + bias_c[0])
        gcum_cs.append((_dot_hi(lowf, g_c), _dot_hi(uppf, g_c)))
        gcum_rs.append((_dot_hi(g_r, uppf), _dot_hi(g_r, lowf)))
        for h in range(nh):
            qh = act[:, h * hd:(h + 1) * hd]
            kh = act[:, bw + h * hd:bw + (h + 1) * hd]
            vh = act[:, 2 * bw + h * hd:2 * bw + (h + 1) * hd]
            qh = qh * lax.rsqrt(jnp.sum(qh * qh, axis=-1, keepdims=True) + RMS_EPS) * (hd ** -0.5)
            kh = kh * lax.rsqrt(jnp.sum(kh * kh, axis=-1, keepdims=True) + RMS_EPS)
            heads.append((ch, h, qh, kh, vh))
    k16 = [kh.astype(BF16) for (_, _, _, kh, _) in heads]
    q16 = [qh.astype(BF16) for (_, _, qh, _, _) in heads]
    kks = [_dot_nt(kb, kb) for kb in k16]
    qks = [_dot_nt(qb, kb) for qb, kb in zip(q16, k16)]

    inst = []
    nmats = []
    for (ch, h, _, _, _), kk in zip(heads, kks):
        for d in range(2):
            col = 2 * nh + nh * d + h
            gi = gcum_cs[ch][d][:, col:col + 1]
            gj = gcum_rs[ch][d][col:col + 1, :]
            dm = jnp.where(incl[d], jnp.exp(jnp.where(incl[d], gi - gj, 0.0)), 0.0)
            beta = beta_cs[ch][:, nh * d + h:nh * d + h + 1]
            nmats.append(jnp.where(strict[d], beta * kk * dm, 0.0))
            inst.append((ch, h, d, gi, beta, dm))
    ainvs = _tri_inverses(nmats, eye, blk16, blk32)

    rhss = []
    for (ch, h, d, gi, beta, _), ainv in zip(inst, ainvs):
        _, _, _, kh, vh = heads[ch * nh + h]
        rhss.append(jnp.concatenate([vh * beta, kh * (beta * jnp.exp(gi))], axis=1).astype(BF16))
    a16 = [a.astype(BF16) for a in ainvs]
    sols = [_dot(a, r) for a, r in zip(a16, rhss)]

    sd_rows = [[[] for _ in range(2)] for _ in range(n_ch)]
    for (ch, h, d, gi, _, dm), sol in zip(inst, sols):
        _, _, qh, kh, _ = heads[ch * nh + h]
        rs = slice(ch * cc, (ch + 1) * cc)
        hs = slice(h * hd, (h + 1) * hd)
        glast = gi[last_row[d]:last_row[d] + 1, :]
        u_o[d, 0, rs, hs] = sol[:, 0:hd].astype(u_o.dtype)
        w_o[d, 0, rs, hs] = sol[:, hd:2 * hd].astype(w_o.dtype)
        qg_o[d, 0, rs, hs] = (qh * jnp.exp(gi)).astype(qg_o.dtype)
        kg_o[d, 0, rs, hs] = (kh * jnp.exp(glast - gi)).astype(kg_o.dtype)
        in_o[d, 0, rs, h * cc:(h + 1) * cc] = (qks[ch * nh + h] * dm).astype(in_o.dtype)
        sd_rows[ch][d].append(jnp.broadcast_to(jnp.exp(glast), (1, LANES)))
    for ch in range(n_ch):
        for d in range(2):
            sd_o[d, 0, ch] = jnp.concatenate(sd_rows[ch][d] + [jnp.zeros((8 - nh, LANES), F32)], axis=0)


def _deltanet_stage_a(pd, pbd, pbdt, dn_conv_w, alog_r, bias_r, alog_c, bias_c, layer, n_lat, n_tot):
    bsz, t, _ = pd.shape
    bw = dn_conv_w.shape[-1] // 3
    cc = DN_CHUNK
    rows = DNA_CHUNKS * cc
    main = lambda col: pl.BlockSpec((1, rows, bw), lambda b, i: (b, i, col))
    halos = [_halo_specs(rows, bw, col, t // HALO) for col in range(3)]
    small = lambda a: pl.BlockSpec((1,) + a.shape[1:], lambda b, i: (layer,) + (0,) * (a.ndim - 1))
    tok = lambda n, dt: jax.ShapeDtypeStruct((2, bsz, t, n), dt)
    tok_spec = lambda n: pl.BlockSpec((2, 1, rows, n), lambda b, i: (0, b, i, 0))
    return pl.pallas_call(
        functools.partial(_dna_kernel, n_lat=n_lat, n_tot=n_tot),
        grid=(bsz, n_tot),
        in_specs=[main(0), main(1), main(2),
                  halos[0][0], halos[1][0], halos[2][0], halos[0][1], halos[1][1], halos[2][1],
                  pl.BlockSpec((1, rows, LANES), lambda b, i: (b, i, 0)),
                  pl.BlockSpec((1, DNA_CHUNKS, 4 * DN_HEADS, cc), lambda b, i: (b, i, 0, 0)),
                  small(dn_conv_w), small(alog_r), small(bias_r), small(alog_c), small(bias_c)],
        out_specs=[tok_spec(bw), tok_spec(bw), tok_spec(bw), tok_spec(bw), tok_spec(DN_HEADS * cc),
                   pl.BlockSpec((2, 1, DNA_CHUNKS, 8, LANES), lambda b, i: (0, b, i, 0, 0))],
        out_shape=[tok(bw, BF16), tok(bw, BF16), tok(bw, BF16), tok(bw, BF16), tok(DN_HEADS * cc, BF16),
                   jax.ShapeDtypeStruct((2, bsz, t // cc, 8, LANES), F32)],
        scratch_shapes=[pltpu.VMEM((rows + 2 * HALO, 3 * bw), F32)],
        compiler_params=_params("parallel", "parallel"),
        name="deltanet_a",
    )(pd, pd, pd, pd, pd, pd, pd, pd, pd, pbd, pbdt, dn_conv_w, alog_r, bias_r, alog_c, bias_c)


def _dnb_kernel(qg_f, kg_f, u_f, w_f, in_f, sd_f, qg_b, kg_b, u_b, w_b, in_b, sd_b, of_ref, ob_ref, state):
    cc = DN_CHUNK
    hd = state.shape[-1]
    bsz = state.shape[1]

    @pl.when(pl.program_id(0) == 0)
    def _():
        state[...] = jnp.zeros_like(state)

    dirs = ((qg_f, kg_f, u_f, w_f, in_f, sd_f, of_ref), (qg_b, kg_b, u_b, w_b, in_b, sd_b, ob_ref))
    chains = [(d, b, h) for d in range(2) for b in range(bsz) for h in range(DN_HEADS)]
    hs = lambda h: slice(h * hd, (h + 1) * hd)
    s32 = [state[d, b, h] for d, b, h in chains]
    s16 = [s.astype(BF16) for s in s32]
    ws = [_dot(dirs[d][3][0, b, :, hs(h)], s) for (d, b, h), s in zip(chains, s16)]
    qs = [_dot(dirs[d][0][0, b, :, hs(h)], s) for (d, b, h), s in zip(chains, s16)]
    vnew = [(dirs[d][2][0, b, :, hs(h)].astype(F32) - x).astype(BF16) for (d, b, h), x in zip(chains, ws)]
    intra = [_dot(dirs[d][4][0, b, :, h * cc:(h + 1) * cc], v) for (d, b, h), v in zip(chains, vnew)]
    upd = [_dot_tn(dirs[d][1][0, b, :, hs(h)], v) for (d, b, h), v in zip(chains, vnew)]
    for (d, b, h), s, o1, o2, up in zip(chains, s32, qs, intra, upd):
        dirs[d][6][b, :, hs(h)] = (o1 + o2).astype(dirs[d][6].dtype)
        state[d, b, h] = s * dirs[d][5][0, b, 0, h:h + 1, :] + up


def _deltanet_stage_b(qg, kg, u, w, intra, sd, n_lat, n_tot):
    _, bsz, t, bw = qg.shape
    cc = DN_CHUNK
    hd = bw // DN_HEADS
    n_ctx = n_tot - n_lat
    chunk_f = lambda s: jnp.where(s < n_ctx, n_lat + s, s - n_ctx)
    chunk_b = lambda s: n_tot - 1 - s

    def specs(d, chunk):
        tok = lambda n: pl.BlockSpec((1, bsz, cc, n), lambda s: (d, 0, chunk(s), 0))
        return [tok(bw), tok(bw), tok(bw), tok(bw), tok(DN_HEADS * cc),
                pl.BlockSpec((1, bsz, 1, 8, LANES), lambda s: (d, 0, chunk(s), 0, 0))]

    out = lambda chunk: pl.BlockSpec((bsz, cc, bw), lambda s: (0, chunk(s), 0))
    args = (qg, kg, u, w, intra, sd)
    return pl.pallas_call(
        _dnb_kernel,
        grid=(n_tot,),
        in_specs=specs(0, chunk_f) + specs(1, chunk_b),
        out_specs=[out(chunk_f), out(chunk_b)],
        out_shape=[jax.ShapeDtypeStruct((bsz, t, bw), BF16)] * 2,
        scratch_shapes=[pltpu.VMEM((2, bsz, DN_HEADS, hd, hd), F32)],
        compiler_params=_params("arbitrary"),
        name="deltanet_b",
    )(*args, *args)


def _qkprep_kernel(q, k, cos, sin_a, sin_b, gq, gk, qo, ko):
    dh = LANES // 2
    r = lax.broadcasted_iota(jnp.int32, (LANES, LANES), 0) // dh
    c = lax.broadcasted_iota(jnp.int32, (LANES, LANES), 1) // dh
    seg = (r == c).astype(BF16)
    cs, sa, sb = cos[...], sin_a[...], sin_b[...]

    def norm_rope(x, gain, scale):
        sq = x * x
        hi = sq.astype(BF16)
        lo = (sq - hi.astype(F32)).astype(BF16)
        ms = (_dot(hi, seg) + _dot(lo, seg)) * (1.0 / dh)
        y = x * lax.rsqrt(ms + RMS_EPS) * gain
        y = y * cs + pltpu.roll(y, LANES - dh // 2, 1) * sa + pltpu.roll(y, dh // 2, 1) * sb
        return y * scale

    q_scale = dh ** -0.5 * math.log2(math.e)
    for h in range(DA_HEADS):
        hs = slice(h * LANES, (h + 1) * LANES)
        qo[0, :, hs] = norm_rope(q[0, :, hs].astype(F32), gq[0], q_scale).astype(qo.dtype)
        ko[0, :, hs] = norm_rope(k[0, :, hs].astype(F32), gk[0], 1.0).astype(ko.dtype)


def _qk_prep(pa, cos, sin_a, sin_b, gq, gk, layer):
    bsz, t, _ = pa.shape
    bw = DA_HEADS * LANES
    tm = TM_PROJ
    tab = pl.BlockSpec((tm, LANES), lambda b, i: (i, 0))
    gain = pl.BlockSpec((1, 1, LANES), lambda b, i: (layer, 0, 0))
    return pl.pallas_call(
        _qkprep_kernel,
        grid=(bsz, t // tm),
        in_specs=[pl.BlockSpec((1, tm, bw), lambda b, i: (b, i, 0)),
                  pl.BlockSpec((1, tm, bw), lambda b, i: (b, i, 1)),
                  tab, tab, tab, gain, gain],
        out_specs=[pl.BlockSpec((1, tm, bw), lambda b, i: (b, i, 0))] * 2,
        out_shape=[jax.ShapeDtypeStruct((bsz, t, bw), BF16)] * 2,
        compiler_params=_params("parallel", "parallel"),
        name="qk_prep",
    )(pa, pa, cos, sin_a, sin_b, gq, gk)


def _attn_kernel(q_ref, k_ref, v_ref, z_ref, lam_ref, g_ref, o_ref, qs, vx, m_s, acc, *, lam_init):
    tq = q_ref.shape[1]
    n_keys = k_ref.shape[1]
    dh = LANES // 2
    n_full, rem = divmod(n_keys, ATT_KEYS)

    @pl.when(pl.program_id(2) == 0)
    def _():
        vx[:, 0:LANES] = v_ref[0]
        vx[:, LANES:2 * LANES] = jnp.ones((n_keys, LANES), vx.dtype)

    q = q_ref[0].astype(F32)
    lane = lax.broadcasted_iota(jnp.int32, q.shape, 1)
    qs[0:tq, :] = jnp.where(lane < dh, q, 0.0).astype(qs.dtype)
    qs[tq:2 * tq, :] = jnp.where(lane >= dh, q, 0.0).astype(qs.dtype)
    m_s[...] = jnp.full_like(m_s, -jnp.inf)
    acc[...] = jnp.zeros_like(acc)

    def step(k0, size):
        kblk = k_ref[0, pl.ds(k0, size), :]
        vblk = vx[pl.ds(k0, size), :]
        groups = [slice(r0, r0 + ATT_ROWS) for r0 in range(0, 2 * tq, ATT_ROWS)]
        scores = lambda rows: _dot_nt(qs[rows, :], kblk)
        s_next = scores(groups[0])
        for g, rows in enumerate(groups):
            s = s_next
            if g + 1 < len(groups):
                s_next = scores(groups[g + 1])
            m_prev = m_s[rows, :]
            m_new = jnp.maximum(m_prev, jnp.max(s, axis=1, keepdims=True))
            alpha = jnp.exp2(m_prev - m_new)
            p = jnp.exp2(s - m_new[:, 0:1])
            pv = _dot(p.astype(BF16), vblk)
            acc[rows, 0:LANES] = alpha * acc[rows, 0:LANES] + pv[:, 0:LANES]
            acc[rows, LANES:2 * LANES] = alpha * acc[rows, LANES:2 * LANES] + pv[:, LANES:2 * LANES]
            m_s[rows, :] = m_new

    if n_full:
        def body(j, carry):
            step(pl.multiple_of(j * ATT_KEYS, ATT_KEYS), ATT_KEYS)
            return carry
        lax.fori_loop(0, n_full, body, 0)
    if rem:
        step(n_full * ATT_KEYS, rem)

    lm = lam_ref[0]
    lam = (jnp.exp(jnp.sum(lm[0:1] * lm[1:2], axis=1, keepdims=True))
           - jnp.exp(jnp.sum(lm[2:3] * lm[3:4], axis=1, keepdims=True)) + lam_init)
    on = acc[:, 0:LANES] / acc[:, LANES:2 * LANES]
    o = on[0:tq] - lam * on[tq:2 * tq]
    y = o * lax.rsqrt(jnp.mean(o * o, axis=-1, keepdims=True) + RMS_EPS) * g_ref[0] * (1.0 - lam_init)
    o_ref[0] = (y * _silu(z_ref[0].astype(F32))).astype(o_ref.dtype)


def _diff_attention(qn, kn, pa, da_lambda, subln_g, layer, lam_init, *, tq, q_rows, q_off, k_rows, k_off):
    bsz = qn.shape[0]
    v_col = 2 * DA_HEADS
    z_col = 3 * DA_HEADS
    qb, kb = q_off // tq, k_off // k_rows
    return pl.pallas_call(
        functools.partial(_attn_kernel, lam_init=lam_init),
        grid=(bsz, DA_HEADS, q_rows // tq),
        in_specs=[pl.BlockSpec((1, tq, LANES), lambda b, h, i: (b, i + qb, h)),
                  pl.BlockSpec((1, k_rows, LANES), lambda b, h, i: (b, kb, h)),
                  pl.BlockSpec((1, k_rows, LANES), lambda b, h, i: (b, kb, v_col + h)),
                  pl.BlockSpec((1, tq, LANES), lambda b, h, i: (b, i + qb, z_col + h)),
                  pl.BlockSpec((1,) + da_lambda.shape[1:], lambda b, h, i: (layer, 0, 0)),
                  pl.BlockSpec((1, 1, LANES), lambda b, h, i: (layer, 0, 0))],
        out_specs=pl.BlockSpec((1, tq, LANES), lambda b, h, i: (b, i, h)),
        out_shape=jax.ShapeDtypeStruct((bsz, q_rows, DA_HEADS * LANES), BF16),
        scratch_shapes=[pltpu.VMEM((2 * tq, LANES), BF16),
                        pltpu.VMEM((k_rows, 2 * LANES), BF16),
                        pltpu.VMEM((2 * tq, LANES), F32),
                        pltpu.VMEM((2 * tq, 2 * LANES), F32)],
        compiler_params=_params("parallel", "parallel", "arbitrary"),
        name="diff_attn",
    )(qn, kn, pa, pa, da_lambda, subln_g)


def _merge_kernel(x_ref, yc, of, ob, dz, ydl, ydc, mg, mod_ref, dng, wb, wo, o_ref, *, n_lat_blocks, ctx_row):
    d = x_ref.shape[-1]
    b = pl.program_id(0)
    i = pl.program_id(1)
    is_ctx = i >= n_lat_blocks
    row = jnp.where(is_ctx, ctx_row, b)
    gate = mod_ref[pl.ds(row, 1), 2 * d:3 * d]
    o = of[0].astype(F32) + ob[0].astype(F32)
    hd = dng.shape[-1]
    parts = []
    for h in range(DN_HEADS):
        oh = o[:, h * hd:(h + 1) * hd]
        parts.append(oh * lax.rsqrt(jnp.mean(oh * oh, axis=-1, keepdims=True) + RMS_EPS) * dng[0])
    ydn = (jnp.concatenate(parts, axis=1) * _silu(dz[0].astype(F32))).astype(BF16)
    yda = jnp.where(is_ctx, ydc[0], ydl[0])
    merged = (jax.nn.sigmoid(mg[0, :, 0:d].astype(F32)) * _dot(yc[0], wb[0, 0])
              + jax.nn.sigmoid(mg[0, :, d:2 * d].astype(F32)) * _dot(ydn, wb[0, 1])
              + jax.nn.sigmoid(mg[0, :, 2 * d:3 * d].astype(F32)) * _dot(yda, wb[0, 2]))
    o_ref[0] = x_ref[0] + gate * _dot(merged.astype(BF16), wo[0])


def _merge(xs, y_conv, o_fwd, o_bwd, pd, yd_lat, yd_ctx, pm, mod, dn_norm_g, w_branch, w_out, layer,
           n_lat_blocks, n_blocks, ctx_row):
    bsz, _, d = xs.shape
    tm = TM_PROJ
    bw = y_conv.shape[-1]
    n_ctx_blocks = yd_ctx.shape[1] // tm
    tok = lambda n, col=0: pl.BlockSpec((1, tm, n), lambda b, i: (b, i, col))
    return pl.pallas_call(
        functools.partial(_merge_kernel, n_lat_blocks=n_lat_blocks, ctx_row=ctx_row),
        grid=(bsz, n_blocks),
        in_specs=[tok(d), tok(bw), tok(bw), tok(bw), tok(bw, 3),
                  pl.BlockSpec((1, tm, bw), lambda b, i: (b, jnp.minimum(i, n_lat_blocks - 1), 0)),
                  pl.BlockSpec((1, tm, bw),
                               lambda b, i: (b, jnp.clip(i - n_lat_blocks, 0, n_ctx_blocks - 1), 0)),
                  tok(N_BRANCH * d),
                  pl.BlockSpec(mod.shape, lambda b, i: (0, 0)),
                  pl.BlockSpec((1, 1, dn_norm_g.shape[-1]), lambda b, i: (layer, 0, 0)),
                  pl.BlockSpec((1,) + w_branch.shape[1:], lambda b, i: (layer, 0, 0, 0)),
                  pl.BlockSpec((1,) + w_out.shape[1:], lambda b, i: (layer, 0, 0))],
        out_specs=tok(d),
        out_shape=jax.ShapeDtypeStruct((bsz, n_blocks * tm, d), F32),
        compiler_params=_params("parallel", "parallel"),
        name="merge",
    )(xs, y_conv, o_fwd, o_bwd, pd, yd_lat, yd_ctx, pm, mod, dn_norm_g, w_branch, w_out)


def _rope_tables(seq, ctx_len, dh):
    n_freq = dh // 4
    inv_freq = ROPE_BASE ** (-jnp.arange(n_freq, dtype=F32) / n_freq)
    n_rows = seq // GRID_W
    row = jnp.repeat(jnp.arange(n_rows, dtype=F32), GRID_W)
    col = jnp.tile(jnp.arange(GRID_W, dtype=F32), n_rows)
    ang = jnp.concatenate([row[:, None] * inv_freq, col[:, None] * inv_freq], axis=-1)
    ang = jnp.concatenate([ang, ang], axis=-1)
    cos = jnp.tile(jnp.cos(ang), (1, LANES // dh))
    sin = jnp.tile(jnp.sin(ang), (1, LANES // dh))
    first_half = (jnp.arange(LANES) % dh) < dh // 2
    sin_a = jnp.where(first_half, -sin, 0.0)
    sin_b = jnp.where(first_half, 0.0, sin)
    pad = lambda tbl, fill: jnp.concatenate([tbl, jnp.full((ctx_len, LANES), fill, F32)], axis=0)
    return pad(cos, 1.0), pad(sin_a, 0.0), pad(sin_b, 0.0)


def kernel(x, c, ctx, c_ctx, w_ada, b_ada, norm_g, w_in, conv_w, conv_b, conv_ln_g, conv_ln_b, dn_conv_w,
           dn_a_log, dn_dt_bias, dn_norm_g, da_q_norm_g, da_k_norm_g, da_lambda, da_subln_g, w_branch, w_out):
    bsz, seq, d = x.shape
    ctx_len = ctx.shape[1]
    depth = w_in.shape[0]
    bw = d // 2
    dh = bw // (2 * DA_HEADS)
    t = seq + ctx_len
    dna_rows = DNA_CHUNKS * DN_CHUNK
    assert 2 * dh == LANES and bw // DN_HEADS == LANES and bsz + 1 <= 8
    assert seq % TQ == 0 and seq % ctx_len == 0 and seq % TM_PROJ == 0 and ctx_len % TM_PROJ == 0
    assert seq % dna_rows == 0 and ctx_len % dna_rows == 0 and (2 * TQ) % ATT_ROWS == 0
    n_lat_blocks, n_blocks = seq // TM_PROJ, t // TM_PROJ
    n_lat_chunks, n_chunks = seq // DN_CHUNK, t // DN_CHUNK
    ctx_row = bsz

    xs = jnp.concatenate([x, ctx], axis=1)
    cvec = jnp.concatenate([c, c_ctx[None, :], jnp.zeros((8 - bsz - 1, d), F32)], axis=0)
    cos, sin_a, sin_b = _rope_tables(seq, ctx_len, dh)

    e_conv, e_dn = 3 * bw, 7 * bw
    e_bd = e_dn + 4 * DN_HEADS
    e_da = e_bd + 4 * bw
    nbd = 4 * DN_HEADS
    row3 = lambda a: a.reshape(depth, 1, a.shape[-1])
    gate_row = lambda a: jnp.pad(a.reshape(depth, 1, 2 * DN_HEADS), ((0, 0), (0, 0), (2 * DN_HEADS, LANES - nbd)))
    gate_col = lambda a: jnp.pad(a.reshape(depth, 2 * DN_HEADS, 1), ((0, 0), (2 * DN_HEADS, 0), (0, 0)))
    alog_r, bias_r = gate_row(dn_a_log), gate_row(dn_dt_bias)
    alog_c, bias_c = gate_col(dn_a_log), gate_col(dn_dt_bias)
    tile2 = lambda a: row3(jnp.tile(a, (1, 2)))
    gq, gk = tile2(da_q_norm_g), tile2(da_k_norm_g)
    b_ada3, norm_g3 = row3(b_ada), row3(norm_g)
    conv_b3, ln_g3, ln_b3 = row3(conv_b), row3(conv_ln_g), row3(conv_ln_b)
    dng3, subln3 = row3(dn_norm_g), row3(da_subln_g)
    w_branch16, w_out16 = w_branch.astype(BF16), w_out.astype(BF16)

    for layer in range(depth):
        last = layer == depth - 1
        lam_init = 0.8 - 0.6 * math.exp(-0.3 * layer)
        wl = w_in[layer]
        weights = (wl[:, 0:e_conv].astype(BF16), wl[:, e_conv:e_dn].astype(BF16),
                   jnp.pad(wl[:, e_dn:e_bd], ((0, 0), (0, LANES - nbd))).astype(BF16),
                   wl[:, e_bd:e_da].astype(BF16), wl[:, e_da:].astype(BF16))
        mod = _adaln(cvec, w_ada, b_ada3, layer)
        pc, pd, pbd, pa, pm = _inproj(xs, mod, norm_g3, weights, layer, n_lat_blocks, ctx_row)

        out_blocks = n_lat_blocks if last else n_blocks
        y_conv = _conv_module(pc, conv_w, conv_b3, ln_g3, ln_b3, layer, n_lat_blocks, out_blocks)

        pbdt = jnp.swapaxes(pbd[:, :, 0:nbd].reshape(bsz, n_chunks, DN_CHUNK, nbd), 2, 3)
        qg, kg, u, w, intra, sd = _deltanet_stage_a(pd, pbd, pbdt, dn_conv_w, alog_r, bias_r, alog_c, bias_c,
                                                    layer, seq // dna_rows, t // dna_rows)
        o_fwd, o_bwd = _deltanet_stage_b(qg, kg, u, w, intra, sd, n_lat_chunks, n_chunks)

        qn, kn = _qk_prep(pa, cos, sin_a, sin_b, gq, gk, layer)
        yd_lat = _diff_attention(qn, kn, pa, da_lambda, subln3, layer, lam_init,
                                 tq=TQ, q_rows=seq, q_off=0, k_rows=t, k_off=0)
        yd_ctx = _diff_attention(qn, kn, pa, da_lambda, subln3, layer, lam_init,
                                 tq=ctx_len, q_rows=ctx_len, q_off=seq, k_rows=ctx_len, k_off=seq)

        xs = _merge(xs, y_conv, o_fwd, o_bwd, pd, yd_lat, yd_ctx, pm, mod, dng3, w_branch16, w_out16, layer,
                    n_lat_blocks, out_blocks, ctx_row)
    return xs
```

```python
import functools
import math

import jax
import jax.numpy as jnp
from jax import lax
from jax.experimental import pallas as pl
from jax.experimental.pallas import tpu as pltpu

F32 = jnp.float32
BF16 = jnp.bfloat16
HIGHEST = lax.Precision.HIGHEST

GRID_W = 64
N_BRANCH = 3
CONV_K = 31
DN_HEADS = 4
DN_CHUNK = 64
SHORT_K = 3
DA_HEADS = 4
ROPE_BASE = 10000.0
RMS_EPS = 1e-6
LN_EPS = 1e-5

LANES = 128
SUBLANES = 8
SUBLANES_BF16 = 16
HALO = SUBLANES_BF16
VMEM_LIMIT = 56 * 1024 * 1024

TM_PROJ = 256
PROJ_COLS = 512
CONV_ROWS = 64
DNA_CHUNKS = 4
TQ = 512
ATT_KEYS = 1024
ATT_ROWS = 256


def _silu(x):
    return x * jax.nn.sigmoid(x)


def _softplus(x):
    return jnp.maximum(x, 0.0) + jnp.log(1.0 + jnp.exp(-jnp.abs(x)))


def _dot(a, b):
    return jnp.dot(a, b, preferred_element_type=F32)


def _dot_hi(a, b):
    return jnp.dot(a, b, preferred_element_type=F32, precision=HIGHEST)


def _dot_nt(a, b):
    return lax.dot_general(a, b, (((1,), (1,)), ((), ())), preferred_element_type=F32)


def _dot_tn(a, b):
    return lax.dot_general(a, b, (((0,), (0,)), ((), ())), preferred_element_type=F32)


def _params(*sem):
    return pltpu.CompilerParams(dimension_semantics=sem, vmem_limit_bytes=VMEM_LIMIT)


def _adaln_kernel(c_ref, w_ref, b_ref, o_ref):
    o_ref[...] = _dot_hi(_silu(c_ref[...]), w_ref[0]) + b_ref[0]


def _adaln(cvec, w_ada, b_ada3, layer):
    d = cvec.shape[1]
    return pl.pallas_call(
        _adaln_kernel,
        grid=(3,),
        in_specs=[
            pl.BlockSpec((8, d), lambda j: (0, 0)),
            pl.BlockSpec((1, d, d), lambda j: (layer, 0, j)),
            pl.BlockSpec((1, 1, d), lambda j: (layer, 0, j)),
        ],
        out_specs=pl.BlockSpec((8, d), lambda j: (0, j)),
        out_shape=jax.ShapeDtypeStruct((8, 3 * d), F32),
        compiler_params=_params("parallel"),
        name="adaln",
    )(cvec, w_ada, b_ada3)


def _inproj_kernel(x_ref, mod_ref, g_ref, wc, wd, wb, wa, wm, oc, od, ob, oa, om, *, n_lat_blocks, ctx_row):
    d = x_ref.shape[-1]
    b = pl.program_id(0)
    i = pl.program_id(1)
    row = jnp.where(i >= n_lat_blocks, ctx_row, b)
    m = mod_ref[pl.ds(row, 1), :]
    shift = m[:, 0:d]
    scale = m[:, d:2 * d]
    x = x_ref[0]
    ms = jnp.mean(x * x, axis=-1, keepdims=True)
    h = (x * lax.rsqrt(ms + RMS_EPS) * g_ref[0]) * (1.0 + scale) + shift
    hb = h.astype(BF16)
    for w, o in ((wc, oc), (wd, od), (wb, ob), (wa, oa), (wm, om)):
        n = w.shape[1]
        for c0 in range(0, n, PROJ_COLS):
            c1 = min(c0 + PROJ_COLS, n)
            o[0, :, c0:c1] = _dot(hb, w[:, c0:c1]).astype(o.dtype)


def _inproj(xs, mod, norm_g, weights, layer, n_lat_blocks, ctx_row):
    bsz, t, d = xs.shape
    nblk = t // TM_PROJ
    wc, wd, wb, wa, wm = weights
    resident = lambda w: pl.BlockSpec(w.shape, lambda b, i: (0, 0), pipeline_mode=pl.Buffered(1))
    out = lambda n: pl.BlockSpec((1, TM_PROJ, n), lambda b, i: (b, i, 0))
    return pl.pallas_call(
        functools.partial(_inproj_kernel, n_lat_blocks=n_lat_blocks, ctx_row=ctx_row),
        grid=(bsz, nblk),
        in_specs=[
            pl.BlockSpec((1, TM_PROJ, d), lambda b, i: (b, i, 0)),
            pl.BlockSpec(mod.shape, lambda b, i: (0, 0)),
            pl.BlockSpec((1, 1, d), lambda b, i: (layer, 0, 0)),
            resident(wc), resident(wd), resident(wb), resident(wa), resident(wm),
        ],
        out_specs=[out(wc.shape[1]), out(wd.shape[1]), out(wb.shape[1]), out(wa.shape[1]), out(wm.shape[1])],
        out_shape=[
            jax.ShapeDtypeStruct((bsz, t, wc.shape[1]), BF16),
            jax.ShapeDtypeStruct((bsz, t, wd.shape[1]), BF16),
            jax.ShapeDtypeStruct((bsz, t, wb.shape[1]), F32),
            jax.ShapeDtypeStruct((bsz, t, wa.shape[1]), BF16),
            jax.ShapeDtypeStruct((bsz, t, wm.shape[1]), BF16),
        ],
        compiler_params=_params("parallel", "parallel"),
        name="inproj",
    )(xs, mod, norm_g, wc, wd, wb, wa, wm)


def _segment_edges(i, n_lat, n_tot):
    has_prev = jnp.logical_and(i != 0, i != n_lat)
    has_next = jnp.logical_and(i != n_lat - 1, i != n_tot - 1)
    return has_prev, has_next


def _halo_specs(rows, width, col, n_halo_blocks):
    per = rows // HALO
    prev = pl.BlockSpec((1, HALO, width), lambda b, i: (b, jnp.maximum(i * per - 1, 0), col))
    nxt = pl.BlockSpec((1, HALO, width), lambda b, i: (b, jnp.minimum((i + 1) * per, n_halo_blocks - 1), col))
    return prev, nxt


def _conv_kernel(val, glu, z, pval, pglu, nval, nglu, w_ref, b_ref, lng, lnb, o_ref, buf, *, n_lat, n_tot):
    tm = val.shape[1]
    i = pl.program_id(1)
    has_prev, has_next = _segment_edges(i, n_lat, n_tot)

    def glu_act(v, g):
        return v.astype(F32) * jax.nn.sigmoid(g.astype(F32))

    buf[0, HALO:HALO + tm, :] = glu_act(val[0], glu[0])
    buf[0, 0:HALO, :] = jnp.where(has_prev, glu_act(pval[0], pglu[0]), 0.0)
    buf[0, HALO + tm:HALO + tm + HALO, :] = jnp.where(has_next, glu_act(nval[0], nglu[0]), 0.0)
    span = buf.shape[1] - SUBLANES
    for j in range(1, SUBLANES):
        buf[j, 0:span, :] = buf[0, j:j + span, :]
    pad = CONV_K // 2
    for r in range(0, tm, CONV_ROWS):
        acc = jnp.zeros((CONV_ROWS, buf.shape[2]), F32) + b_ref[0]
        for k in range(CONV_K):
            whole, part = divmod(HALO - pad + k, SUBLANES)
            start = r + whole * SUBLANES
            acc = acc + buf[part, start:start + CONV_ROWS, :] * w_ref[0, k:k + 1, :]
        mu = jnp.mean(acc, axis=-1, keepdims=True)
        xc = acc - mu
        y = xc * lax.rsqrt(jnp.mean(xc * xc, axis=-1, keepdims=True) + LN_EPS)
        y = y * lng[0] + lnb[0]
        zz = z[0, r:r + CONV_ROWS, :].astype(F32)
        o_ref[0, r:r + CONV_ROWS, :] = (_silu(y) * _silu(zz)).astype(o_ref.dtype)


def _conv_module(pc, conv_w, conv_b, ln_g, ln_b, layer, n_lat, n_tot):
    bsz, t, _ = pc.shape
    w = conv_w.shape[-1]
    tm = TM_PROJ
    main = lambda col: pl.BlockSpec((1, tm, w), lambda b, i: (b, i, col))
    pv, nv = _halo_specs(tm, w, 0, t // HALO)
    pg, ng = _halo_specs(tm, w, 1, t // HALO)
    vec = pl.BlockSpec((1, 1, w), lambda b, i: (layer, 0, 0))
    return pl.pallas_call(
        functools.partial(_conv_kernel, n_lat=n_lat, n_tot=n_tot),
        grid=(bsz, n_tot),
        in_specs=[main(0), main(1), main(2), pv, pg, nv, ng,
                  pl.BlockSpec((1, CONV_K, w), lambda b, i: (layer, 0, 0)), vec, vec, vec],
        out_specs=pl.BlockSpec((1, tm, w), lambda b, i: (b, i, 0)),
        out_shape=jax.ShapeDtypeStruct((bsz, n_tot * tm, w), BF16),
        scratch_shapes=[pltpu.VMEM((SUBLANES, tm + 2 * HALO, w), F32)],
        compiler_params=_params("parallel", "parallel"),
        name="conv_module",
    )(pc, pc, pc, pc, pc, pc, pc, conv_w, conv_b, ln_g, ln_b)


def _tri_inverses(nmats, eye, blk16, blk32):
    b = lambda a: a.astype(BF16)
    nds = [jnp.where(blk16, n, 0.0) for n in nmats]
    ts = [eye - nd for nd in nds]
    pws = nds
    for _ in range(3):
        pwbs = [b(p) for p in pws]
        pws = [_dot(p, p) for p in pwbs]
        tbs = [b(t) for t in ts]
        pwbs = [b(p) for p in pws]
        prods = [_dot(t, p) for t, p in zip(tbs, pwbs)]
        ts = [t + pr for t, pr in zip(ts, prods)]
    level32 = jnp.logical_and(blk32, jnp.logical_not(blk16))
    level64 = jnp.logical_not(blk32)
    for level in (level32, level64):
        cs = [b(jnp.where(level, n, 0.0)) for n in nmats]
        tbs = [b(t) for t in ts]
        mids = [_dot(c, t) for c, t in zip(cs, tbs)]
        mids = [b(m) for m in mids]
        prods = [_dot(t, m) for t, m in zip(tbs, mids)]
        ts = [t - pr for t, pr in zip(ts, prods)]
    return ts


def _dna_kernel(q, k, v, pq, pk, pv, nq, nk, nv, bd, bdt, cw, alog_r, bias_r, alog_c, bias_c,
                qg_o, kg_o, u_o, w_o, in_o, sd_o, buf, act_s, *, n_lat, n_tot):
    i = pl.program_id(1)
    has_prev, has_next = _segment_edges(i, n_lat, n_tot)
    cc = DN_CHUNK
    rows = q.shape[1]
    bw = q.shape[-1]
    hd = bw // DN_HEADS
    nh = DN_HEADS
    for idx, (m_, p_, n_) in enumerate(((q, pq, nq), (k, pk, nk), (v, pv, nv))):
        cols = slice(idx * bw, (idx + 1) * bw)
        buf[HALO:HALO + rows, cols] = m_[0].astype(F32)
        buf[0:HALO, cols] = jnp.where(has_prev, p_[0].astype(F32), 0.0)
        buf[HALO + rows:HALO + rows + HALO, cols] = jnp.where(has_next, n_[0].astype(F32), 0.0)

    ii = lax.broadcasted_iota(jnp.int32, (cc, cc), 0)
    jj = lax.broadcasted_iota(jnp.int32, (cc, cc), 1)
    low = jj <= ii
    upp = jj >= ii
    lowf = low.astype(F32)
    uppf = upp.astype(F32)
    eye = (ii == jj).astype(F32)
    blk16 = (ii // 16) == (jj // 16)
    blk32 = (ii // 32) == (jj // 32)
    incl = (low, upp)
    strict = (jj < ii, jj > ii)
    last_row = (cc - 1, 0)

    n_ch = rows // cc
    for r0 in range(0, rows, cc):
        conv = (buf[HALO + r0:HALO + r0 + cc, :] * cw[0, 1:2, :]
                + buf[HALO - 1 + r0:HALO - 1 + r0 + cc, :] * cw[0, 0:1, :]
                + buf[HALO + 1 + r0:HALO + 1 + r0 + cc, :] * cw[0, 2:3, :])
        act_s[r0:r0 + cc, :] = _silu(conv)
    beta_cs, gcum_cs, gcum_rs = [], [], []
    heads = []
    for ch in range(n_ch):
        r0 = ch * cc
        act = act_s[r0:r0 + cc, :]
        x = bd[0, r0:r0 + cc, :]
        beta_cs.append(jax.nn.sigmoid(x))
        g_c = -jnp.exp(alog_r[0]) * _softplus(x + bias_r[0])
        xt = bdt[0, ch]
        g_r = -jnp.exp(alog_c[0]) * _softplus(xt + bias_c[0])
        gcum_cs.append((_dot_hi(lowf, g_c), _dot_hi(uppf, g_c)))
        gcum_rs.append((_dot_hi(g_r, uppf), _dot_hi(g_r, lowf)))
        for h in range(nh):
            qh = act[:, h * hd:(h + 1) * hd]
            kh = act[:, bw + h * hd:bw + (h + 1) * hd]
            vh = act[:, 2 * bw + h * hd:2 * bw + (h + 1) * hd]
            qh = qh * lax.rsqrt(jnp.sum(qh * qh, axis=-1, keepdims=True) + RMS_EPS) * (hd ** -0.5)
            kh = kh * lax.rsqrt(jnp.sum(kh * kh, axis=-1, keepdims=True) + RMS_EPS)
            heads.append((ch, h, qh, kh, vh))
    k16 = [kh.astype(BF16) for (_, _, _, kh, _) in heads]
    q16 = [qh.astype(BF16) for (_, _, qh, _, _) in heads]
    kks = [_dot_nt(kb, kb) for kb in k16]
    qks = [_dot_nt(qb, kb) for qb, kb in zip(q16, k16)]

    inst = []
    nmats = []
    for (ch, h, _, _, _), kk in zip(heads, kks):
        for d in range(2):
            col = 2 * nh + nh * d + h
            gi = gcum_cs[ch][d][:, col:col + 1]
            gj = gcum_rs[ch][d][col:col + 1, :]
            dm = jnp.where(incl[d], jnp.exp(jnp.where(incl[d], gi - gj, 0.0)), 0.0)
            beta = beta_cs[ch][:, nh * d + h:nh * d + h + 1]
            nmats.append(jnp.where(strict[d], beta * kk * dm, 0.0))
            inst.append((ch, h, d, gi, beta, dm))
    ainvs = _tri_inverses(nmats, eye, blk16, blk32)

    rhss = []
    for (ch, h, d, gi, beta, _), ainv in zip(inst, ainvs):
        _, _, _, kh, vh = heads[ch * nh + h]
        rhss.append(jnp.concatenate([vh * beta, kh * (beta * jnp.exp(gi))], axis=1).astype(BF16))
    a16 = [a.astype(BF16) for a in ainvs]
    sols = [_dot(a, r) for a, r in zip(a16, rhss)]

    sd_rows = [[[] for _ in range(2)] for _ in range(n_ch)]
    for (ch, h, d, gi, _, dm), sol in zip(inst, sols):
        _, _, qh, kh, _ = heads[ch * nh + h]
        rs = slice(ch * cc, (ch + 1) * cc)
        hs = slice(h * hd, (h + 1) * hd)
        glast = gi[last_row[d]:last_row[d] + 1, :]
        u_o[d, 0, rs, hs] = sol[:, 0:hd].astype(u_o.dtype)
        w_o[d, 0, rs, hs] = sol[:, hd:2 * hd].astype(w_o.dtype)
        qg_o[d, 0, rs, hs] = (qh * jnp.exp(gi)).astype(qg_o.dtype)
        kg_o[d, 0, rs, hs] = (kh * jnp.exp(glast - gi)).astype(kg_o.dtype)
        in_o[d, 0, rs, h * cc:(h + 1) * cc] = (qks[ch * nh + h] * dm).astype(in_o.dtype)
        sd_rows[ch][d].append(jnp.broadcast_to(jnp.exp(glast), (1, LANES)))
    for ch in range(n_ch):
        for d in range(2):
            sd_o[d, 0, ch] = jnp.concatenate(sd_rows[ch][d] + [jnp.zeros((8 - nh, LANES), F32)], axis=0)


def _deltanet_stage_a(pd, pbd, pbdt, dn_conv_w, alog_r, bias_r, alog_c, bias_c, layer, n_lat, n_tot):
    bsz, t, _ = pd.shape
    bw = dn_conv_w.shape[-1] // 3
    cc = DN_CHUNK
    rows = DNA_CHUNKS * cc
    main = lambda col: pl.BlockSpec((1, rows, bw), lambda b, i: (b, i, col))
    halos = [_halo_specs(rows, bw, col, t // HALO) for col in range(3)]
    small = lambda a: pl.BlockSpec((1,) + a.shape[1:], lambda b, i: (layer,) + (0,) * (a.ndim - 1))
    tok = lambda n, dt: jax.ShapeDtypeStruct((2, bsz, t, n), dt)
    tok_spec = lambda n: pl.BlockSpec((2, 1, rows, n), lambda b, i: (0, b, i, 0))
    return pl.pallas_call(
        functools.partial(_dna_kernel, n_lat=n_lat, n_tot=n_tot),
        grid=(bsz, n_tot),
        in_specs=[main(0), main(1), main(2),
                  halos[0][0], halos[1][0], halos[2][0], halos[0][1], halos[1][1], halos[2][1],
                  pl.BlockSpec((1, rows, LANES), lambda b, i: (b, i, 0)),
                  pl.BlockSpec((1, DNA_CHUNKS, 4 * DN_HEADS, cc), lambda b, i: (b, i, 0, 0)),
                  small(dn_conv_w), small(alog_r), small(bias_r), small(alog_c), small(bias_c)],
        out_specs=[tok_spec(bw), tok_spec(bw), tok_spec(bw), tok_spec(bw), tok_spec(DN_HEADS * cc),
                   pl.BlockSpec((2, 1, DNA_CHUNKS, 8, LANES), lambda b, i: (0, b, i, 0, 0))],
        out_shape=[tok(bw, BF16), tok(bw, BF16), tok(bw, BF16), tok(bw, BF16), tok(DN_HEADS * cc, BF16),
                   jax.ShapeDtypeStruct((2, bsz, t // cc, 8, LANES), F32)],
        scratch_shapes=[pltpu.VMEM((rows + 2 * HALO, 3 * bw), F32), pltpu.VMEM((rows, 3 * bw), F32)],
        compiler_params=_params("parallel", "parallel"),
        name="deltanet_a",
    )(pd, pd, pd, pd, pd, pd, pd, pd, pd, pbd, pbdt, dn_conv_w, alog_r, bias_r, alog_c, bias_c)


def _dnb_kernel(qg_f, kg_f, u_f, w_f, in_f, sd_f, qg_b, kg_b, u_b, w_b, in_b, sd_b, of_ref, ob_ref, state):
    cc = DN_CHUNK
    hd = state.shape[-1]
    bsz = state.shape[1]

    @pl.when(pl.program_id(0) == 0)
    def _():
        state[...] = jnp.zeros_like(state)

    dirs = ((qg_f, kg_f, u_f, w_f, in_f, sd_f, of_ref), (qg_b, kg_b, u_b, w_b, in_b, sd_b, ob_ref))
    chains = [(d, b, h) for d in range(2) for b in range(bsz) for h in range(DN_HEADS)]
    hs = lambda h: slice(h * hd, (h + 1) * hd)
    s32 = [state[d, b, h] for d, b, h in chains]
    s16 = [s.astype(BF16) for s in s32]
    ws = [_dot(dirs[d][3][0, b, :, hs(h)], s) for (d, b, h), s in zip(chains, s16)]
    qs = [_dot(dirs[d][0][0, b, :, hs(h)], s) for (d, b, h), s in zip(chains, s16)]
    vnew = [(dirs[d][2][0, b, :, hs(h)].astype(F32) - x).astype(BF16) for (d, b, h), x in zip(chains, ws)]
    intra = [_dot(dirs[d][4][0, b, :, h * cc:(h + 1) * cc], v) for (d, b, h), v in zip(chains, vnew)]
    upd = [_dot_tn(dirs[d][1][0, b, :, hs(h)], v) for (d, b, h), v in zip(chains, vnew)]
    for (d, b, h), s, o1, o2, up in zip(chains, s32, qs, intra, upd):
        dirs[d][6][b, :, hs(h)] = (o1 + o2).astype(dirs[d][6].dtype)
        state[d, b, h] = s * dirs[d][5][0, b, 0, h:h + 1, :] + up


def _deltanet_stage_b(qg, kg, u, w, intra, sd, n_lat, n_tot):
    _, bsz, t, bw = qg.shape
    cc = DN_CHUNK
    hd = bw // DN_HEADS
    n_ctx = n_tot - n_lat
    chunk_f = lambda s: jnp.where(s < n_ctx, n_lat + s, s - n_ctx)
    chunk_b = lambda s: n_tot - 1 - s

    def specs(d, chunk):
        tok = lambda n: pl.BlockSpec((1, bsz, cc, n), lambda s: (d, 0, chunk(s), 0))
        return [tok(bw), tok(bw), tok(bw), tok(bw), tok(DN_HEADS * cc),
                pl.BlockSpec((1, bsz, 1, 8, LANES), lambda s: (d, 0, chunk(s), 0, 0))]

    out = lambda chunk: pl.BlockSpec((bsz, cc, bw), lambda s: (0, chunk(s), 0))
    args = (qg, kg, u, w, intra, sd)
    return pl.pallas_call(
        _dnb_kernel,
        grid=(n_tot,),
        in_specs=specs(0, chunk_f) + specs(1, chunk_b),
        out_specs=[out(chunk_f), out(chunk_b)],
        out_shape=[jax.ShapeDtypeStruct((bsz, t, bw), BF16)] * 2,
        scratch_shapes=[pltpu.VMEM((2, bsz, DN_HEADS, hd, hd), F32)],
        compiler_params=_params("arbitrary"),
        name="deltanet_b",
    )(*args, *args)


def _qkprep_kernel(q, k, cos, sin_a, sin_b, gq, gk, qo, ko):
    dh = LANES // 2
    r = lax.broadcasted_iota(jnp.int32, (LANES, LANES), 0) // dh
    c = lax.broadcasted_iota(jnp.int32, (LANES, LANES), 1) // dh
    seg = (r == c).astype(BF16)
    cs, sa, sb = cos[...], sin_a[...], sin_b[...]

    def norm_rope(x, gain, scale):
        sq = x * x
        hi = sq.astype(BF16)
        lo = (sq - hi.astype(F32)).astype(BF16)
        ms = (_dot(hi, seg) + _dot(lo, seg)) * (1.0 / dh)
        y = x * lax.rsqrt(ms + RMS_EPS) * gain
        y = y * cs + pltpu.roll(y, LANES - dh // 2, 1) * sa + pltpu.roll(y, dh // 2, 1) * sb
        return y * scale

    q_scale = dh ** -0.5 * math.log2(math.e)
    for h in range(DA_HEADS):
        hs = slice(h * LANES, (h + 1) * LANES)
        qo[0, :, hs] = norm_rope(q[0, :, hs].astype(F32), gq[0], q_scale).astype(qo.dtype)
        ko[0, :, hs] = norm_rope(k[0, :, hs].astype(F32), gk[0], 1.0).astype(ko.dtype)


def _qk_prep(pa, cos, sin_a, sin_b, gq, gk, layer):
    bsz, t, _ = pa.shape
    bw = DA_HEADS * LANES
    tm = TM_PROJ
    tab = pl.BlockSpec((tm, LANES), lambda b, i: (i, 0))
    gain = pl.BlockSpec((1, 1, LANES), lambda b, i: (layer, 0, 0))
    return pl.pallas_call(
        _qkprep_kernel,
        grid=(bsz, t // tm),
        in_specs=[pl.BlockSpec((1, tm, bw), lambda b, i: (b, i, 0)),
                  pl.BlockSpec((1, tm, bw), lambda b, i: (b, i, 1)),
                  tab, tab, tab, gain, gain],
        out_specs=[pl.BlockSpec((1, tm, bw), lambda b, i: (b, i, 0))] * 2,
        out_shape=[jax.ShapeDtypeStruct((bsz, t, bw), BF16)] * 2,
        compiler_params=_params("parallel", "parallel"),
        name="qk_prep",
    )(pa, pa, cos, sin_a, sin_b, gq, gk)


def _attn_kernel(q_ref, k_ref, v_ref, z_ref, lam_ref, g_ref, o_ref, qs, vx, m_s, acc, *, lam_init):
    tq = q_ref.shape[1]
    n_keys = k_ref.shape[1]
    dh = LANES // 2
    n_full, rem = divmod(n_keys, ATT_KEYS)

    @pl.when(pl.program_id(2) == 0)
    def _():
        vx[:, 0:LANES] = v_ref[0]
        vx[:, LANES:2 * LANES] = jnp.ones((n_keys, LANES), vx.dtype)

    q = q_ref[0].astype(F32)
    lane = lax.broadcasted_iota(jnp.int32, q.shape, 1)
    qs[0:tq, :] = jnp.where(lane < dh, q, 0.0).astype(qs.dtype)
    qs[tq:2 * tq, :] = jnp.where(lane >= dh, q, 0.0).astype(qs.dtype)
    m_s[...] = jnp.full_like(m_s, -jnp.inf)
    acc[...] = jnp.zeros_like(acc)

    key_blocks = [(j * ATT_KEYS, ATT_KEYS) for j in range(n_full)] + ([(n_full * ATT_KEYS, rem)] if rem else [])
    groups = [slice(r0, r0 + ATT_ROWS) for r0 in range(0, 2 * tq, ATT_ROWS)]
    tasks = [(k0, size, rows) for k0, size in key_blocks for rows in groups]
    scores = lambda k0, size, rows: _dot_nt(qs[rows, :], k_ref[0, k0:k0 + size, :])
    s_next = scores(*tasks[0])
    for i, (k0, size, rows) in enumerate(tasks):
        s = s_next
        if i + 1 < len(tasks):
            s_next = scores(*tasks[i + 1])
        m_prev = m_s[rows, :]
        m_new = jnp.maximum(m_prev, jnp.max(s, axis=1, keepdims=True))
        alpha = jnp.exp2(m_prev - m_new)
        p = jnp.exp2(s - m_new[:, 0:1])
        pv = _dot(p.astype(BF16), vx[k0:k0 + size, :])
        acc[rows, 0:LANES] = alpha * acc[rows, 0:LANES] + pv[:, 0:LANES]
        acc[rows, LANES:2 * LANES] = alpha * acc[rows, LANES:2 * LANES] + pv[:, LANES:2 * LANES]
        m_s[rows, :] = m_new

    lm = lam_ref[0]
    lam = (jnp.exp(jnp.sum(lm[0:1] * lm[1:2], axis=1, keepdims=True))
           - jnp.exp(jnp.sum(lm[2:3] * lm[3:4], axis=1, keepdims=True)) + lam_init)
    on = acc[:, 0:LANES] / acc[:, LANES:2 * LANES]
    o = on[0:tq] - lam * on[tq:2 * tq]
    y = o * lax.rsqrt(jnp.mean(o * o, axis=-1, keepdims=True) + RMS_EPS) * g_ref[0] * (1.0 - lam_init)
    o_ref[0] = (y * _silu(z_ref[0].astype(F32))).astype(o_ref.dtype)


def _diff_attention(qn, kn, pa, da_lambda, subln_g, layer, lam_init, *, tq, q_rows, q_off, k_rows, k_off):
    bsz = qn.shape[0]
    v_col = 2 * DA_HEADS
    z_col = 3 * DA_HEADS
    qb, kb = q_off // tq, k_off // k_rows
    return pl.pallas_call(
        functools.partial(_attn_kernel, lam_init=lam_init),
        grid=(bsz, DA_HEADS, q_rows // tq),
        in_specs=[pl.BlockSpec((1, tq, LANES), lambda b, h, i: (b, i + qb, h)),
                  pl.BlockSpec((1, k_rows, LANES), lambda b, h, i: (b, kb, h)),
                  pl.BlockSpec((1, k_rows, LANES), lambda b, h, i: (b, kb, v_col + h)),
                  pl.BlockSpec((1, tq, LANES), lambda b, h, i: (b, i + qb, z_col + h)),
                  pl.BlockSpec((1,) + da_lambda.shape[1:], lambda b, h, i: (layer, 0, 0)),
                  pl.BlockSpec((1, 1, LANES), lambda b, h, i: (layer, 0, 0))],
        out_specs=pl.BlockSpec((1, tq, LANES), lambda b, h, i: (b, i, h)),
        out_shape=jax.ShapeDtypeStruct((bsz, q_rows, DA_HEADS * LANES), BF16),
        scratch_shapes=[pltpu.VMEM((2 * tq, LANES), BF16),
                        pltpu.VMEM((k_rows, 2 * LANES), BF16),
                        pltpu.VMEM((2 * tq, LANES), F32),
                        pltpu.VMEM((2 * tq, 2 * LANES), F32)],
        compiler_params=_params("parallel", "parallel", "arbitrary"),
        name="diff_attn",
    )(qn, kn, pa, pa, da_lambda, subln_g)


def _merge_kernel(x_ref, yc, of, ob, dz, ydl, ydc, mg, mod_ref, dng, wb, wo, o_ref, *, n_lat_blocks, ctx_row):
    d = x_ref.shape[-1]
    b = pl.program_id(0)
    i = pl.program_id(1)
    is_ctx = i >= n_lat_blocks
    row = jnp.where(is_ctx, ctx_row, b)
    gate = mod_ref[pl.ds(row, 1), 2 * d:3 * d]
    o = of[0].astype(F32) + ob[0].astype(F32)
    hd = dng.shape[-1]
    parts = []
    for h in range(DN_HEADS):
        oh = o[:, h * hd:(h + 1) * hd]
        parts.append(oh * lax.rsqrt(jnp.mean(oh * oh, axis=-1, keepdims=True) + RMS_EPS) * dng[0])
    ydn = (jnp.concatenate(parts, axis=1) * _silu(dz[0].astype(F32))).astype(BF16)
    yda = jnp.where(is_ctx, ydc[0], ydl[0])
    merged = (jax.nn.sigmoid(mg[0, :, 0:d].astype(F32)) * _dot(yc[0], wb[0, 0])
              + jax.nn.sigmoid(mg[0, :, d:2 * d].astype(F32)) * _dot(ydn, wb[0, 1])
              + jax.nn.sigmoid(mg[0, :, 2 * d:3 * d].astype(F32)) * _dot(yda, wb[0, 2]))
    o_ref[0] = x_ref[0] + gate * _dot(merged.astype(BF16), wo[0])


def _merge(xs, y_conv, o_fwd, o_bwd, pd, yd_lat, yd_ctx, pm, mod, dn_norm_g, w_branch, w_out, layer,
           n_lat_blocks, n_blocks, ctx_row):
    bsz, _, d = xs.shape
    tm = TM_PROJ
    bw = y_conv.shape[-1]
    n_ctx_blocks = yd_ctx.shape[1] // tm
    tok = lambda n, col=0: pl.BlockSpec((1, tm, n), lambda b, i: (b, i, col))
    return pl.pallas_call(
        functools.partial(_merge_kernel, n_lat_blocks=n_lat_blocks, ctx_row=ctx_row),
        grid=(bsz, n_blocks),
        in_specs=[tok(d), tok(bw), tok(bw), tok(bw), tok(bw, 3),
                  pl.BlockSpec((1, tm, bw), lambda b, i: (b, jnp.minimum(i, n_lat_blocks - 1), 0)),
                  pl.BlockSpec((1, tm, bw),
                               lambda b, i: (b, jnp.clip(i - n_lat_blocks, 0, n_ctx_blocks - 1), 0)),
                  tok(N_BRANCH * d),
                  pl.BlockSpec(mod.shape, lambda b, i: (0, 0)),
                  pl.BlockSpec((1, 1, dn_norm_g.shape[-1]), lambda b, i: (layer, 0, 0)),
                  pl.BlockSpec((1,) + w_branch.shape[1:], lambda b, i: (layer, 0, 0, 0)),
                  pl.BlockSpec((1,) + w_out.shape[1:], lambda b, i: (layer, 0, 0))],
        out_specs=tok(d),
        out_shape=jax.ShapeDtypeStruct((bsz, n_blocks * tm, d), F32),
        compiler_params=_params("parallel", "parallel"),
        name="merge",
    )(xs, y_conv, o_fwd, o_bwd, pd, yd_lat, yd_ctx, pm, mod, dn_norm_g, w_branch, w_out)


def _rope_tables(seq, ctx_len, dh):
    n_freq = dh // 4
    inv_freq = ROPE_BASE ** (-jnp.arange(n_freq, dtype=F32) / n_freq)
    n_rows = seq // GRID_W
    row = jnp.repeat(jnp.arange(n_rows, dtype=F32), GRID_W)
    col = jnp.tile(jnp.arange(GRID_W, dtype=F32), n_rows)
    ang = jnp.concatenate([row[:, None] * inv_freq, col[:, None] * inv_freq], axis=-1)
    ang = jnp.concatenate([ang, ang], axis=-1)
    cos = jnp.tile(jnp.cos(ang), (1, LANES // dh))
    sin = jnp.tile(jnp.sin(ang), (1, LANES // dh))
    first_half = (jnp.arange(LANES) % dh) < dh // 2
    sin_a = jnp.where(first_half, -sin, 0.0)
    sin_b = jnp.where(first_half, 0.0, sin)
    pad = lambda tbl, fill: jnp.concatenate([tbl, jnp.full((ctx_len, LANES), fill, F32)], axis=0)
    return pad(cos, 1.0), pad(sin_a, 0.0), pad(sin_b, 0.0)


def kernel(x, c, ctx, c_ctx, w_ada, b_ada, norm_g, w_in, conv_w, conv_b, conv_ln_g, conv_ln_b, dn_conv_w,
           dn_a_log, dn_dt_bias, dn_norm_g, da_q_norm_g, da_k_norm_g, da_lambda, da_subln_g, w_branch, w_out):
    bsz, seq, d = x.shape
    ctx_len = ctx.shape[1]
    depth = w_in.shape[0]
    bw = d // 2
    dh = bw // (2 * DA_HEADS)
    t = seq + ctx_len
    dna_rows = DNA_CHUNKS * DN_CHUNK
    assert 2 * dh == LANES and bw // DN_HEADS == LANES and bsz + 1 <= 8
    assert seq % TQ == 0 and seq % ctx_len == 0 and seq % TM_PROJ == 0 and ctx_len % TM_PROJ == 0
    assert seq % dna_rows == 0 and ctx_len % dna_rows == 0 and (2 * TQ) % ATT_ROWS == 0
    n_lat_blocks, n_blocks = seq // TM_PROJ, t // TM_PROJ
    n_lat_chunks, n_chunks = seq // DN_CHUNK, t // DN_CHUNK
    ctx_row = bsz

    xs = jnp.concatenate([x, ctx], axis=1)
    cvec = jnp.concatenate([c, c_ctx[None, :], jnp.zeros((8 - bsz - 1, d), F32)], axis=0)
    cos, sin_a, sin_b = _rope_tables(seq, ctx_len, dh)

    e_conv, e_dn = 3 * bw, 7 * bw
    e_bd = e_dn + 4 * DN_HEADS
    e_da = e_bd + 4 * bw
    nbd = 4 * DN_HEADS
    row3 = lambda a: a.reshape(depth, 1, a.shape[-1])
    gate_row = lambda a: jnp.pad(a.reshape(depth, 1, 2 * DN_HEADS), ((0, 0), (0, 0), (2 * DN_HEADS, LANES - nbd)))
    gate_col = lambda a: jnp.pad(a.reshape(depth, 2 * DN_HEADS, 1), ((0, 0), (2 * DN_HEADS, 0), (0, 0)))
    alog_r, bias_r = gate_row(dn_a_log), gate_row(dn_dt_bias)
    alog_c, bias_c = gate_col(dn_a_log), gate_col(dn_dt_bias)
    tile2 = lambda a: row3(jnp.tile(a, (1, 2)))
    gq, gk = tile2(da_q_norm_g), tile2(da_k_norm_g)
    b_ada3, norm_g3 = row3(b_ada), row3(norm_g)
    conv_b3, ln_g3, ln_b3 = row3(conv_b), row3(conv_ln_g), row3(conv_ln_b)
    dng3, subln3 = row3(dn_norm_g), row3(da_subln_g)
    w_branch16, w_out16 = w_branch.astype(BF16), w_out.astype(BF16)

    for layer in range(depth):
        last = layer == depth - 1
        lam_init = 0.8 - 0.6 * math.exp(-0.3 * layer)
        wl = w_in[layer]
        weights = (wl[:, 0:e_conv].astype(BF16), wl[:, e_conv:e_dn].astype(BF16),
                   jnp.pad(wl[:, e_dn:e_bd], ((0, 0), (0, LANES - nbd))).astype(BF16),
                   wl[:, e_bd:e_da].astype(BF16), wl[:, e_da:].astype(BF16))
        mod = _adaln(cvec, w_ada, b_ada3, layer)
        pc, pd, pbd, pa, pm = _inproj(xs, mod, norm_g3, weights, layer, n_lat_blocks, ctx_row)

        out_blocks = n_lat_blocks if last else n_blocks
        y_conv = _conv_module(pc, conv_w, conv_b3, ln_g3, ln_b3, layer, n_lat_blocks, out_blocks)

        pbdt = jnp.swapaxes(pbd[:, :, 0:nbd].reshape(bsz, n_chunks, DN_CHUNK, nbd), 2, 3)
        qg, kg, u, w, intra, sd = _deltanet_stage_a(pd, pbd, pbdt, dn_conv_w, alog_r, bias_r, alog_c, bias_c,
                                                    layer, seq // dna_rows, t // dna_rows)
        o_fwd, o_bwd = _deltanet_stage_b(qg, kg, u, w, intra, sd, n_lat_chunks, n_chunks)

        qn, kn = _qk_prep(pa, cos, sin_a, sin_b, gq, gk, layer)
        yd_lat = _diff_attention(qn, kn, pa, da_lambda, subln3, layer, lam_init,
                                 tq=TQ, q_rows=seq, q_off=0, k_rows=t, k_off=0)
        yd_ctx = _diff_attention(qn, kn, pa, da_lambda, subln3, layer, lam_init,
                                 tq=ctx_len, q_rows=ctx_len, q_off=seq, k_rows=ctx_len, k_off=seq)

        xs = _merge(xs, y_conv, o_fwd, o_bwd, pd, yd_lat, yd_ctx, pm, mod, dng3, w_branch16, w_out16, layer,
                    n_lat_blocks, out_blocks, ctx_row)
    return xs
```

```python
import functools
import math

import jax
import jax.numpy as jnp
from jax import lax
from jax.experimental import pallas as pl
from jax.experimental.pallas import tpu as pltpu

F32 = jnp.float32
BF16 = jnp.bfloat16
HIGHEST = lax.Precision.HIGHEST

GRID_W = 64
N_BRANCH = 3
CONV_K = 31
DN_HEADS = 4
DN_CHUNK = 64
SHORT_K = 3
DA_HEADS = 4
ROPE_BASE = 10000.0
RMS_EPS = 1e-6
LN_EPS = 1e-5

LANES = 128
SUBLANES = 8
SUBLANES_BF16 = 16
HALO = SUBLANES_BF16
VMEM_LIMIT = 56 * 1024 * 1024

TM_PROJ = 256
PROJ_COLS = 512
CONV_ROWS = 64
DNA_CHUNKS = 4
TQ = 512
ATT_KEYS = 4096
ATT_ROWS = 512


def _silu(x):
    return x * jax.nn.sigmoid(x)


def _softplus(x):
    return jnp.maximum(x, 0.0) + jnp.log(1.0 + jnp.exp(-jnp.abs(x)))


def _dot(a, b):
    return jnp.dot(a, b, preferred_element_type=F32)


def _dot_hi(a, b):
    return jnp.dot(a, b, preferred_element_type=F32, precision=HIGHEST)


def _dot_nt(a, b):
    return lax.dot_general(a, b, (((1,), (1,)), ((), ())), preferred_element_type=F32)


def _dot_tn(a, b):
    return lax.dot_general(a, b, (((0,), (0,)), ((), ())), preferred_element_type=F32)


def _params(*sem):
    return pltpu.CompilerParams(dimension_semantics=sem, vmem_limit_bytes=VMEM_LIMIT)


def _adaln_kernel(c_ref, w_ref, b_ref, o_ref):
    o_ref[...] = _dot_hi(_silu(c_ref[...]), w_ref[0]) + b_ref[0]


def _adaln(cvec, w_ada, b_ada3, layer):
    d = cvec.shape[1]
    return pl.pallas_call(
        _adaln_kernel,
        grid=(3,),
        in_specs=[
            pl.BlockSpec((8, d), lambda j: (0, 0)),
            pl.BlockSpec((1, d, d), lambda j: (layer, 0, j)),
            pl.BlockSpec((1, 1, d), lambda j: (layer, 0, j)),
        ],
        out_specs=pl.BlockSpec((8, d), lambda j: (0, j)),
        out_shape=jax.ShapeDtypeStruct((8, 3 * d), F32),
        compiler_params=_params("parallel"),
        name="adaln",
    )(cvec, w_ada, b_ada3)


def _inproj_kernel(x_ref, xp_ref, xn_ref, mod_ref, g_ref, wv, wg, wz, wd, wb, wq, wk, wvz, wm,
                   cw_ref, cb_ref, lng, lnb, cos, sin_a, sin_b, gq, gk,
                   yc_o, od, ob, oq, ok, ovz, om, hb_s, buf, zc_s, *, n_lat_blocks, n_blocks, ctx_row):
    d = x_ref.shape[-1]
    tm = x_ref.shape[1]
    b = pl.program_id(0)
    i = pl.program_id(1)
    row = jnp.where(i >= n_lat_blocks, ctx_row, b)
    m = mod_ref[pl.ds(row, 1), :]
    shift = m[:, 0:d]
    scale = m[:, d:2 * d]

    def modulated(x):
        ms = jnp.mean(x * x, axis=-1, keepdims=True)
        return ((x * lax.rsqrt(ms + RMS_EPS) * g_ref[0]) * (1.0 + scale) + shift).astype(BF16)

    hb_s[0:HALO, :] = modulated(xp_ref[0])
    hb_s[HALO:HALO + tm, :] = modulated(x_ref[0])
    hb_s[HALO + tm:HALO + tm + HALO, :] = modulated(xn_ref[0])
    hb_ext = lambda: hb_s[...]
    hb = lambda: hb_s[HALO:HALO + tm, :]

    matmul_tasks, vector_tasks = [], []

    def project(w, o):
        n = w.shape[1]
        for c0 in range(0, n, PROJ_COLS):
            c1 = min(c0 + PROJ_COLS, n)

            def task(c0=c0, c1=c1):
                o[0, :, c0:c1] = _dot(hb(),w[:, c0:c1]).astype(o.dtype)
            matmul_tasks.append(task)

    has_prev, has_next = _segment_edges(i, n_lat_blocks, n_blocks)
    a = _dot(hb_ext(),wv[...]) * jax.nn.sigmoid(_dot(hb_ext(),wg[...]))
    zc_s[...] = _dot(hb(),wz[...])
    buf[0, HALO:HALO + tm, :] = a[HALO:HALO + tm]
    buf[0, 0:HALO, :] = jnp.where(has_prev, a[0:HALO], 0.0)
    buf[0, HALO + tm:HALO + tm + HALO, :] = jnp.where(has_next, a[HALO + tm:HALO + tm + HALO], 0.0)
    span = buf.shape[1] - SUBLANES

    def shift_task(j):
        buf[j, 0:span, :] = buf[0, j:j + span, :]

    pad = CONV_K // 2

    def conv_task(r):
        acc = jnp.zeros((CONV_ROWS, buf.shape[2]), F32) + cb_ref[0]
        for k in range(CONV_K):
            whole, part = divmod(HALO - pad + k, SUBLANES)
            start = r + whole * SUBLANES
            acc = acc + buf[part, start:start + CONV_ROWS, :] * cw_ref[0, k:k + 1, :]
        mu = jnp.mean(acc, axis=-1, keepdims=True)
        xc = acc - mu
        y = xc * lax.rsqrt(jnp.mean(xc * xc, axis=-1, keepdims=True) + LN_EPS)
        y = y * lng[0] + lnb[0]
        yc_o[0, r:r + CONV_ROWS, :] = (_silu(y) * _silu(zc_s[r:r + CONV_ROWS, :])).astype(yc_o.dtype)

    vector_tasks += [functools.partial(shift_task, j) for j in range(1, SUBLANES)]
    vector_tasks += [functools.partial(conv_task, r) for r in range(0, tm, CONV_ROWS)]

    project(wd, od)
    project(wb, ob)

    dh = LANES // 2
    first_map = lax.broadcasted_iota(jnp.int32, (tm, LANES), 1) < dh
    cs, sa, sb = cos[...], sin_a[...], sin_b[...]

    def norm_rope(x, gain, scale_):
        sq = x * x
        s0 = jnp.sum(jnp.where(first_map, sq, 0.0), axis=-1, keepdims=True)
        s1 = jnp.sum(jnp.where(first_map, 0.0, sq), axis=-1, keepdims=True)
        ms = jnp.where(first_map, s0, s1) * (1.0 / dh)
        y = x * lax.rsqrt(ms + RMS_EPS) * gain
        y = y * cs + pltpu.roll(y, LANES - dh // 2, 1) * sa + pltpu.roll(y, dh // 2, 1) * sb
        return y * scale_

    q_scale = dh ** -0.5 * math.log2(math.e)
    def qk_task(w, o, gain, scale_, h):
        hs = slice(h * LANES, (h + 1) * LANES)
        o[0, :, hs] = norm_rope(_dot(hb(),w[:, hs]), gain[0], scale_).astype(o.dtype)

    for h in range(DA_HEADS):
        matmul_tasks.append(functools.partial(qk_task, wq, oq, gq, q_scale, h))
        matmul_tasks.append(functools.partial(qk_task, wk, ok, gk, 1.0, h))

    project(wvz, ovz)
    project(wm, om)

    done = 0
    for n, task in enumerate(matmul_tasks):
        task()
        while done * len(matmul_tasks) < (n + 1) * len(vector_tasks):
            vector_tasks[done]()
            done += 1


def _inproj(xs, mod, norm_g, weights, conv_w, conv_b, ln_g, ln_b, cos, sin_a, sin_b, gq, gk, layer,
            n_lat_blocks, ctx_row):
    bsz, t, d = xs.shape
    tm = TM_PROJ
    nblk = t // tm
    wv, wg, wz, wd, wb, wq, wk, wvz, wm = weights
    bw = wv.shape[1]
    resident = lambda w: pl.BlockSpec(w.shape, lambda b, i: (0, 0), pipeline_mode=pl.Buffered(1))
    out = lambda n: pl.BlockSpec((1, tm, n), lambda b, i: (b, i, 0))
    xp, xn = _halo_specs(tm, d, 0, t // HALO)
    vec = lambda n: pl.BlockSpec((1, 1, n), lambda b, i: (layer, 0, 0))
    tab = pl.BlockSpec((tm, LANES), lambda b, i: (i, 0))
    widths = (bw, wd.shape[1], wb.shape[1], wq.shape[1], wk.shape[1], wvz.shape[1], wm.shape[1])
    dtypes = (BF16, BF16, F32, BF16, BF16, BF16, BF16)
    return pl.pallas_call(
        functools.partial(_inproj_kernel, n_lat_blocks=n_lat_blocks, n_blocks=nblk, ctx_row=ctx_row),
        grid=(bsz, nblk),
        in_specs=[
            pl.BlockSpec((1, tm, d), lambda b, i: (b, i, 0)), xp, xn,
            pl.BlockSpec(mod.shape, lambda b, i: (0, 0)),
            vec(d),
            resident(wv), resident(wg), resident(wz), resident(wd), resident(wb), resident(wq), resident(wk),
            resident(wvz), resident(wm),
            pl.BlockSpec((1, CONV_K, bw), lambda b, i: (layer, 0, 0)), vec(bw), vec(bw), vec(bw),
            tab, tab, tab, vec(LANES), vec(LANES),
        ],
        out_specs=[out(n) for n in widths],
        out_shape=[jax.ShapeDtypeStruct((bsz, t, n), dt) for n, dt in zip(widths, dtypes)],
        scratch_shapes=[pltpu.VMEM((tm + 2 * HALO, d), BF16),
                        pltpu.VMEM((SUBLANES, tm + 2 * HALO, bw), F32),
                        pltpu.VMEM((tm, bw), F32)],
        compiler_params=_params("parallel", "parallel"),
        name="inproj",
    )(xs, xs, xs, mod, norm_g, wv, wg, wz, wd, wb, wq, wk, wvz, wm,
      conv_w, conv_b, ln_g, ln_b, cos, sin_a, sin_b, gq, gk)


def _segment_edges(i, n_lat, n_tot):
    has_prev = jnp.logical_and(i != 0, i != n_lat)
    has_next = jnp.logical_and(i != n_lat - 1, i != n_tot - 1)
    return has_prev, has_next


def _halo_specs(rows, width, col, n_halo_blocks):
    per = rows // HALO
    prev = pl.BlockSpec((1, HALO, width), lambda b, i: (b, jnp.maximum(i * per - 1, 0), col))
    nxt = pl.BlockSpec((1, HALO, width), lambda b, i: (b, jnp.minimum((i + 1) * per, n_halo_blocks - 1), col))
    return prev, nxt


def _tri_inverses(nmats, eye, blk16, blk32):
    b = lambda a: a.astype(BF16)
    nds = [jnp.where(blk16, n, 0.0) for n in nmats]
    ts = [eye - nd for nd in nds]
    pws = nds
    for _ in range(3):
        pwbs = [b(p) for p in pws]
        pws = [_dot(p, p) for p in pwbs]
        tbs = [b(t) for t in ts]
        pwbs = [b(p) for p in pws]
        prods = [_dot(t, p) for t, p in zip(tbs, pwbs)]
        ts = [t + pr for t, pr in zip(ts, prods)]
    level32 = jnp.logical_and(blk32, jnp.logical_not(blk16))
    level64 = jnp.logical_not(blk32)
    for level in (level32, level64):
        cs = [b(jnp.where(level, n, 0.0)) for n in nmats]
        tbs = [b(t) for t in ts]
        mids = [_dot(c, t) for c, t in zip(cs, tbs)]
        mids = [b(m) for m in mids]
        prods = [_dot(t, m) for t, m in zip(tbs, mids)]
        ts = [t - pr for t, pr in zip(ts, prods)]
    return ts


def _dna_kernel(q, k, v, pq, pk, pv, nq, nk, nv, bd, bdt, cw, alog_r, bias_r, alog_c, bias_c,
                qg_o, kg_o, u_o, w_o, in_o, sd_o, buf, act_s, *, n_lat, n_tot):
    i = pl.program_id(1)
    has_prev, has_next = _segment_edges(i, n_lat, n_tot)
    cc = DN_CHUNK
    rows = q.shape[1]
    bw = q.shape[-1]
    hd = bw // DN_HEADS
    nh = DN_HEADS
    for idx, (m_, p_, n_) in enumerate(((q, pq, nq), (k, pk, nk), (v, pv, nv))):
        cols = slice(idx * bw, (idx + 1) * bw)
        buf[HALO:HALO + rows, cols] = m_[0].astype(F32)
        buf[0:HALO, cols] = jnp.where(has_prev, p_[0].astype(F32), 0.0)
        buf[HALO + rows:HALO + rows + HALO, cols] = jnp.where(has_next, n_[0].astype(F32), 0.0)

    ii = lax.broadcasted_iota(jnp.int32, (cc, cc), 0)
    jj = lax.broadcasted_iota(jnp.int32, (cc, cc), 1)
    low = jj <= ii
    upp = jj >= ii
    lowf = low.astype(F32)
    uppf = upp.astype(F32)
    eye = (ii == jj).astype(F32)
    blk16 = (ii // 16) == (jj // 16)
    blk32 = (ii // 32) == (jj // 32)
    incl = (low, upp)
    strict = (jj < ii, jj > ii)
    last_row = (cc - 1, 0)

    n_ch = rows // cc
    for r0 in range(0, rows, cc):
        conv = (buf[HALO + r0:HALO + r0 + cc, :] * cw[0, 1:2, :]
                + buf[HALO - 1 + r0:HALO - 1 + r0 + cc, :] * cw[0, 0:1, :]
                + buf[HALO + 1 + r0:HALO + 1 + r0 + cc, :] * cw[0, 2:3, :])
        act_s[r0:r0 + cc, :] = _silu(conv)
    beta_cs, gcum_cs, gcum_rs = [], [], []
    heads = []
    for ch in range(n_ch):
        r0 = ch * cc
        act = act_s[r0:r0 + cc, :]
        x = bd[0, r0:r0 + cc, :]
        beta_cs.append(jax.nn.sigmoid(x))
        g_c = -jnp.exp(alog_r[0]) * _softplus(x + bias_r[0])
        xt = bdt[0, ch]
        g_r = -jnp.exp(alog_c[0]) * _softplus(xt + bias_c[0])
        gcum_cs.append((_dot_hi(lowf, g_c), _dot_hi(uppf, g_c)))
        gcum_rs.append((_dot_hi(g_r, uppf), _dot_hi(g_r, lowf)))
        for h in range(nh):
            qh = act[:, h * hd:(h + 1) * hd]
            kh = act[:, bw + h * hd:bw + (h + 1) * hd]
            vh = act[:, 2 * bw + h * hd:2 * bw + (h + 1) * hd]
            qh = qh * lax.rsqrt(jnp.sum(qh * qh, axis=-1, keepdims=True) + RMS_EPS) * (hd ** -0.5)
            kh = kh * lax.rsqrt(jnp.sum(kh * kh, axis=-1, keepdims=True) + RMS_EPS)
            heads.append((ch, h, qh, kh, vh))
    k16 = [kh.astype(BF16) for (_, _, _, kh, _) in heads]
    q16 = [qh.astype(BF16) for (_, _, qh, _, _) in heads]
    kks = [_dot_nt(kb, kb) for kb in k16]
    qks = [_dot_nt(qb, kb) for qb, kb in zip(q16, k16)]

    inst = []
    nmats = []
    for (ch, h, _, _, _), kk in zip(heads, kks):
        for d in range(2):
            col = 2 * nh + nh * d + h
            gi = gcum_cs[ch][d][:, col:col + 1]
            gj = gcum_rs[ch][d][col:col + 1, :]
            dm = jnp.where(incl[d], jnp.exp(jnp.where(incl[d], gi - gj, 0.0)), 0.0)
            beta = beta_cs[ch][:, nh * d + h:nh * d + h + 1]
            nmats.append(jnp.where(strict[d], beta * kk * dm, 0.0))
            inst.append((ch, h, d, gi, beta, dm))
    ainvs = _tri_inverses(nmats, eye, blk16, blk32)

    rhss = []
    for (ch, h, d, gi, beta, _), ainv in zip(inst, ainvs):
        _, _, _, kh, vh = heads[ch * nh + h]
        rhss.append(jnp.concatenate([vh * beta, kh * (beta * jnp.exp(gi))], axis=1).astype(BF16))
    a16 = [a.astype(BF16) for a in ainvs]
    sols = [_dot(a, r) for a, r in zip(a16, rhss)]

    sd_rows = [[[] for _ in range(2)] for _ in range(n_ch)]
    for (ch, h, d, gi, _, dm), sol in zip(inst, sols):
        _, _, qh, kh, _ = heads[ch * nh + h]
        rs = slice(ch * cc, (ch + 1) * cc)
        hs = slice(h * hd, (h + 1) * hd)
        glast = gi[last_row[d]:last_row[d] + 1, :]
        u_o[d, 0, rs, hs] = sol[:, 0:hd].astype(u_o.dtype)
        w_o[d, 0, rs, hs] = sol[:, hd:2 * hd].astype(w_o.dtype)
        qg_o[d, 0, rs, hs] = (qh * jnp.exp(gi)).astype(qg_o.dtype)
        kg_o[d, 0, rs, hs] = (kh * jnp.exp(glast - gi)).astype(kg_o.dtype)
        in_o[d, 0, rs, h * cc:(h + 1) * cc] = (qks[ch * nh + h] * dm).astype(in_o.dtype)
        sd_rows[ch][d].append(jnp.broadcast_to(jnp.exp(glast), (1, LANES)))
    for ch in range(n_ch):
        for d in range(2):
            sd_o[d, 0, ch] = jnp.concatenate(sd_rows[ch][d] + [jnp.zeros((8 - nh, LANES), F32)], axis=0)


def _deltanet_stage_a(pd, pbd, pbdt, dn_conv_w, alog_r, bias_r, alog_c, bias_c, layer, n_lat, n_tot):
    bsz, t, _ = pd.shape
    bw = dn_conv_w.shape[-1] // 3
    cc = DN_CHUNK
    rows = DNA_CHUNKS * cc
    main = lambda col: pl.BlockSpec((1, rows, bw), lambda b, i: (b, i, col))
    halos = [_halo_specs(rows, bw, col, t // HALO) for col in range(3)]
    small = lambda a: pl.BlockSpec((1,) + a.shape[1:], lambda b, i: (layer,) + (0,) * (a.ndim - 1))
    tok = lambda n, dt: jax.ShapeDtypeStruct((2, bsz, t, n), dt)
    tok_spec = lambda n: pl.BlockSpec((2, 1, rows, n), lambda b, i: (0, b, i, 0))
    return pl.pallas_call(
        functools.partial(_dna_kernel, n_lat=n_lat, n_tot=n_tot),
        grid=(bsz, n_tot),
        in_specs=[main(0), main(1), main(2),
                  halos[0][0], halos[1][0], halos[2][0], halos[0][1], halos[1][1], halos[2][1],
                  pl.BlockSpec((1, rows, LANES), lambda b, i: (b, i, 0)),
                  pl.BlockSpec((1, DNA_CHUNKS, 4 * DN_HEADS, cc), lambda b, i: (b, i, 0, 0)),
                  small(dn_conv_w), small(alog_r), small(bias_r), small(alog_c), small(bias_c)],
        out_specs=[tok_spec(bw), tok_spec(bw), tok_spec(bw), tok_spec(bw), tok_spec(DN_HEADS * cc),
                   pl.BlockSpec((2, 1, DNA_CHUNKS, 8, LANES), lambda b, i: (0, b, i, 0, 0))],
        out_shape=[tok(bw, BF16), tok(bw, BF16), tok(bw, BF16), tok(bw, BF16), tok(DN_HEADS * cc, BF16),
                   jax.ShapeDtypeStruct((2, bsz, t // cc, 8, LANES), F32)],
        scratch_shapes=[pltpu.VMEM((rows + 2 * HALO, 3 * bw), F32), pltpu.VMEM((rows, 3 * bw), F32)],
        compiler_params=_params("parallel", "parallel"),
        name="deltanet_a",
    )(pd, pd, pd, pd, pd, pd, pd, pd, pd, pbd, pbdt, dn_conv_w, alog_r, bias_r, alog_c, bias_c)


def _dnb_kernel(qg_f, kg_f, u_f, w_f, in_f, sd_f, qg_b, kg_b, u_b, w_b, in_b, sd_b, of_ref, ob_ref, state):
    cc = DN_CHUNK
    hd = state.shape[-1]
    bsz = state.shape[1]

    @pl.when(pl.program_id(0) == 0)
    def _():
        state[...] = jnp.zeros_like(state)

    dirs = ((qg_f, kg_f, u_f, w_f, in_f, sd_f, of_ref), (qg_b, kg_b, u_b, w_b, in_b, sd_b, ob_ref))
    chains = [(d, b, h) for d in range(2) for b in range(bsz) for h in range(DN_HEADS)]
    hs = lambda h: slice(h * hd, (h + 1) * hd)
    s32 = [state[d, b, h] for d, b, h in chains]
    s16 = [s.astype(BF16) for s in s32]
    ws = [_dot(dirs[d][3][0, b, :, hs(h)], s) for (d, b, h), s in zip(chains, s16)]
    qs = [_dot(dirs[d][0][0, b, :, hs(h)], s) for (d, b, h), s in zip(chains, s16)]
    vnew = [(dirs[d][2][0, b, :, hs(h)].astype(F32) - x).astype(BF16) for (d, b, h), x in zip(chains, ws)]
    intra = [_dot(dirs[d][4][0, b, :, h * cc:(h + 1) * cc], v) for (d, b, h), v in zip(chains, vnew)]
    upd = [_dot_tn(dirs[d][1][0, b, :, hs(h)], v) for (d, b, h), v in zip(chains, vnew)]
    for (d, b, h), s, o1, o2, up in zip(chains, s32, qs, intra, upd):
        dirs[d][6][b, :, hs(h)] = (o1 + o2).astype(dirs[d][6].dtype)
        state[d, b, h] = s * dirs[d][5][0, b, 0, h:h + 1, :] + up


def _deltanet_stage_b(qg, kg, u, w, intra, sd, n_lat, n_tot):
    _, bsz, t, bw = qg.shape
    cc = DN_CHUNK
    hd = bw // DN_HEADS
    n_ctx = n_tot - n_lat
    chunk_f = lambda s: jnp.where(s < n_ctx, n_lat + s, s - n_ctx)
    chunk_b = lambda s: n_tot - 1 - s

    def specs(d, chunk):
        tok = lambda n: pl.BlockSpec((1, bsz, cc, n), lambda s: (d, 0, chunk(s), 0))
        return [tok(bw), tok(bw), tok(bw), tok(bw), tok(DN_HEADS * cc),
                pl.BlockSpec((1, bsz, 1, 8, LANES), lambda s: (d, 0, chunk(s), 0, 0))]

    out = lambda chunk: pl.BlockSpec((bsz, cc, bw), lambda s: (0, chunk(s), 0))
    args = (qg, kg, u, w, intra, sd)
    return pl.pallas_call(
        _dnb_kernel,
        grid=(n_tot,),
        in_specs=specs(0, chunk_f) + specs(1, chunk_b),
        out_specs=[out(chunk_f), out(chunk_b)],
        out_shape=[jax.ShapeDtypeStruct((bsz, t, bw), BF16)] * 2,
        scratch_shapes=[pltpu.VMEM((2, bsz, DN_HEADS, hd, hd), F32)],
        compiler_params=_params("arbitrary"),
        name="deltanet_b",
    )(*args, *args)


def _attn_kernel(q_ref, k_ref, v_ref, z_ref, lam_ref, g_ref, o_ref, qs, vx, m_s, acc, *, lam_init):
    tq = q_ref.shape[1]
    n_keys = k_ref.shape[1]
    dh = LANES // 2
    n_full, rem = divmod(n_keys, ATT_KEYS)

    @pl.when(pl.program_id(2) == 0)
    def _():
        vx[:, 0:LANES] = v_ref[0]
        vx[:, LANES:2 * LANES] = jnp.ones((n_keys, LANES), vx.dtype)

    q = q_ref[0].astype(F32)
    lane = lax.broadcasted_iota(jnp.int32, q.shape, 1)
    qs[0:tq, :] = jnp.where(lane < dh, q, 0.0).astype(qs.dtype)
    qs[tq:2 * tq, :] = jnp.where(lane >= dh, q, 0.0).astype(qs.dtype)
    m_s[...] = jnp.full_like(m_s, -jnp.inf)
    acc[...] = jnp.zeros_like(acc)

    key_blocks = [(j * ATT_KEYS, ATT_KEYS) for j in range(n_full)] + ([(n_full * ATT_KEYS, rem)] if rem else [])
    groups = [slice(r0, r0 + ATT_ROWS) for r0 in range(0, 2 * tq, ATT_ROWS)]
    tasks = [(k0, size, rows) for k0, size in key_blocks for rows in groups]
    scores = lambda k0, size, rows: _dot_nt(qs[rows, :], k_ref[0, k0:k0 + size, :])
    s_next = scores(*tasks[0])
    for i, (k0, size, rows) in enumerate(tasks):
        s = s_next
        if i + 1 < len(tasks):
            s_next = scores(*tasks[i + 1])
        m_prev = m_s[rows, :]
        m_new = jnp.maximum(m_prev, jnp.max(s, axis=1, keepdims=True))
        alpha = jnp.exp2(m_prev - m_new)
        p = jnp.exp2(s - m_new[:, 0:1])
        pv = _dot(p.astype(BF16), vx[k0:k0 + size, :])
        acc[rows, 0:LANES] = alpha * acc[rows, 0:LANES] + pv[:, 0:LANES]
        acc[rows, LANES:2 * LANES] = alpha * acc[rows, LANES:2 * LANES] + pv[:, LANES:2 * LANES]
        m_s[rows, :] = m_new

    lm = lam_ref[0]
    lam = (jnp.exp(jnp.sum(lm[0:1] * lm[1:2], axis=1, keepdims=True))
           - jnp.exp(jnp.sum(lm[2:3] * lm[3:4], axis=1, keepdims=True)) + lam_init)
    on = acc[:, 0:LANES] / acc[:, LANES:2 * LANES]
    o = on[0:tq] - lam * on[tq:2 * tq]
    y = o * lax.rsqrt(jnp.mean(o * o, axis=-1, keepdims=True) + RMS_EPS) * g_ref[0] * (1.0 - lam_init)
    o_ref[0] = (y * _silu(z_ref[0].astype(F32))).astype(o_ref.dtype)


def _diff_attention(qn, kn, pa, da_lambda, subln_g, layer, lam_init, *, tq, q_rows, q_off, k_rows, k_off):
    bsz = qn.shape[0]
    v_col = 0
    z_col = DA_HEADS
    qb, kb = q_off // tq, k_off // k_rows
    return pl.pallas_call(
        functools.partial(_attn_kernel, lam_init=lam_init),
        grid=(bsz, DA_HEADS, q_rows // tq),
        in_specs=[pl.BlockSpec((1, tq, LANES), lambda b, h, i: (b, i + qb, h)),
                  pl.BlockSpec((1, k_rows, LANES), lambda b, h, i: (b, kb, h)),
                  pl.BlockSpec((1, k_rows, LANES), lambda b, h, i: (b, kb, v_col + h)),
                  pl.BlockSpec((1, tq, LANES), lambda b, h, i: (b, i + qb, z_col + h)),
                  pl.BlockSpec((1,) + da_lambda.shape[1:], lambda b, h, i: (layer, 0, 0)),
                  pl.BlockSpec((1, 1, LANES), lambda b, h, i: (layer, 0, 0))],
        out_specs=pl.BlockSpec((1, tq, LANES), lambda b, h, i: (b, i, h)),
        out_shape=jax.ShapeDtypeStruct((bsz, q_rows, DA_HEADS * LANES), BF16),
        scratch_shapes=[pltpu.VMEM((2 * tq, LANES), BF16),
                        pltpu.VMEM((k_rows, 2 * LANES), BF16),
                        pltpu.VMEM((2 * tq, LANES), F32),
                        pltpu.VMEM((2 * tq, 2 * LANES), F32)],
        compiler_params=_params("parallel", "parallel", "arbitrary"),
        name="diff_attn",
    )(qn, kn, pa, pa, da_lambda, subln_g)


def _merge_kernel(x_ref, yc, of, ob, dz, ydl, ydc, mg, mod_ref, dng, wb, wo, o_ref, *, n_lat_blocks, ctx_row):
    d = x_ref.shape[-1]
    b = pl.program_id(0)
    i = pl.program_id(1)
    is_ctx = i >= n_lat_blocks
    row = jnp.where(is_ctx, ctx_row, b)
    gate = mod_ref[pl.ds(row, 1), 2 * d:3 * d]
    o = of[0].astype(F32) + ob[0].astype(F32)
    hd = dng.shape[-1]
    parts = []
    for h in range(DN_HEADS):
        oh = o[:, h * hd:(h + 1) * hd]
        parts.append(oh * lax.rsqrt(jnp.mean(oh * oh, axis=-1, keepdims=True) + RMS_EPS) * dng[0])
    ydn = (jnp.concatenate(parts, axis=1) * _silu(dz[0].astype(F32))).astype(BF16)
    yda = jnp.where(is_ctx, ydc[0], ydl[0])
    merged = (jax.nn.sigmoid(mg[0, :, 0:d].astype(F32)) * _dot(yc[0], wb[0, 0])
              + jax.nn.sigmoid(mg[0, :, d:2 * d].astype(F32)) * _dot(ydn, wb[0, 1])
              + jax.nn.sigmoid(mg[0, :, 2 * d:3 * d].astype(F32)) * _dot(yda, wb[0, 2]))
    o_ref[0] = x_ref[0] + gate * _dot(merged.astype(BF16), wo[0])


def _merge(xs, y_conv, o_fwd, o_bwd, pd, yd_lat, yd_ctx, pm, mod, dn_norm_g, w_branch, w_out, layer,
           n_lat_blocks, n_blocks, ctx_row):
    bsz, _, d = xs.shape
    tm = TM_PROJ
    bw = y_conv.shape[-1]
    n_ctx_blocks = yd_ctx.shape[1] // tm
    tok = lambda n, col=0: pl.BlockSpec((1, tm, n), lambda b, i: (b, i, col))
    return pl.pallas_call(
        functools.partial(_merge_kernel, n_lat_blocks=n_lat_blocks, ctx_row=ctx_row),
        grid=(bsz, n_blocks),
        in_specs=[tok(d), tok(bw), tok(bw), tok(bw), tok(bw, 3),
                  pl.BlockSpec((1, tm, bw), lambda b, i: (b, jnp.minimum(i, n_lat_blocks - 1), 0)),
                  pl.BlockSpec((1, tm, bw),
                               lambda b, i: (b, jnp.clip(i - n_lat_blocks, 0, n_ctx_blocks - 1), 0)),
                  tok(N_BRANCH * d),
                  pl.BlockSpec(mod.shape, lambda b, i: (0, 0)),
                  pl.BlockSpec((1, 1, dn_norm_g.shape[-1]), lambda b, i: (layer, 0, 0)),
                  pl.BlockSpec((1,) + w_branch.shape[1:], lambda b, i: (layer, 0, 0, 0)),
                  pl.BlockSpec((1,) + w_out.shape[1:], lambda b, i: (layer, 0, 0))],
        out_specs=tok(d),
        out_shape=jax.ShapeDtypeStruct((bsz, n_blocks * tm, d), F32),
        compiler_params=_params("parallel", "parallel"),
        name="merge",
    )(xs, y_conv, o_fwd, o_bwd, pd, yd_lat, yd_ctx, pm, mod, dn_norm_g, w_branch, w_out)


def _rope_tables(seq, ctx_len, dh):
    n_freq = dh // 4
    inv_freq = ROPE_BASE ** (-jnp.arange(n_freq, dtype=F32) / n_freq)
    n_rows = seq // GRID_W
    row_ang = jnp.arange(n_rows, dtype=F32)[:, None] * inv_freq
    col_ang = jnp.arange(GRID_W, dtype=F32)[:, None] * inv_freq
    cos_r, sin_r, cos_c, sin_c = lax.optimization_barrier(
        (jnp.cos(row_ang), jnp.sin(row_ang), jnp.cos(col_ang), jnp.sin(col_ang)))

    def table(by_row, by_col):
        r = jnp.broadcast_to(by_row[:, None, :], (n_rows, GRID_W, n_freq)).reshape(seq, n_freq)
        c = jnp.broadcast_to(by_col[None, :, :], (n_rows, GRID_W, n_freq)).reshape(seq, n_freq)
        return jnp.tile(jnp.concatenate([r, c], axis=-1), (1, 2 * LANES // dh))

    cos = table(cos_r, cos_c)
    sin = table(sin_r, sin_c)
    first_half = (jnp.arange(LANES) % dh) < dh // 2
    sin_a = jnp.where(first_half, -sin, 0.0)
    sin_b = jnp.where(first_half, 0.0, sin)
    pad = lambda tbl, fill: jnp.concatenate([tbl, jnp.full((ctx_len, LANES), fill, F32)], axis=0)
    return pad(cos, 1.0), pad(sin_a, 0.0), pad(sin_b, 0.0)


def kernel(x, c, ctx, c_ctx, w_ada, b_ada, norm_g, w_in, conv_w, conv_b, conv_ln_g, conv_ln_b, dn_conv_w,
           dn_a_log, dn_dt_bias, dn_norm_g, da_q_norm_g, da_k_norm_g, da_lambda, da_subln_g, w_branch, w_out):
    bsz, seq, d = x.shape
    ctx_len = ctx.shape[1]
    depth = w_in.shape[0]
    bw = d // 2
    dh = bw // (2 * DA_HEADS)
    t = seq + ctx_len
    dna_rows = DNA_CHUNKS * DN_CHUNK
    assert 2 * dh == LANES and bw // DN_HEADS == LANES and bsz + 1 <= 8
    assert seq % TQ == 0 and seq % ctx_len == 0 and seq % TM_PROJ == 0 and ctx_len % TM_PROJ == 0
    assert seq % dna_rows == 0 and ctx_len % dna_rows == 0 and (2 * TQ) % ATT_ROWS == 0
    n_lat_blocks, n_blocks = seq // TM_PROJ, t // TM_PROJ
    n_lat_chunks, n_chunks = seq // DN_CHUNK, t // DN_CHUNK
    ctx_row = bsz

    xs = jnp.concatenate([x, ctx], axis=1)
    cvec = jnp.concatenate([c, c_ctx[None, :], jnp.zeros((8 - bsz - 1, d), F32)], axis=0)
    cos, sin_a, sin_b = _rope_tables(seq, ctx_len, dh)

    e_conv, e_dn = 3 * bw, 7 * bw
    e_bd = e_dn + 4 * DN_HEADS
    e_da = e_bd + 4 * bw
    nbd = 4 * DN_HEADS
    row3 = lambda a: a.reshape(depth, 1, a.shape[-1])
    gate_row = lambda a: jnp.pad(a.reshape(depth, 1, 2 * DN_HEADS), ((0, 0), (0, 0), (2 * DN_HEADS, LANES - nbd)))
    gate_col = lambda a: jnp.pad(a.reshape(depth, 2 * DN_HEADS, 1), ((0, 0), (2 * DN_HEADS, 0), (0, 0)))
    alog_r, bias_r = gate_row(dn_a_log), gate_row(dn_dt_bias)
    alog_c, bias_c = gate_col(dn_a_log), gate_col(dn_dt_bias)
    tile2 = lambda a: row3(jnp.tile(a, (1, 2)))
    gq, gk = tile2(da_q_norm_g), tile2(da_k_norm_g)
    b_ada3, norm_g3 = row3(b_ada), row3(norm_g)
    conv_b3, ln_g3, ln_b3 = row3(conv_b), row3(conv_ln_g), row3(conv_ln_b)
    dng3, subln3 = row3(dn_norm_g), row3(da_subln_g)
    w_branch16, w_out16 = w_branch.astype(BF16), w_out.astype(BF16)

    for layer in range(depth):
        last = layer == depth - 1
        lam_init = 0.8 - 0.6 * math.exp(-0.3 * layer)
        wl = w_in[layer]
        cols = lambda lo, hi: wl[:, lo:hi].astype(BF16)
        weights = (cols(0, bw), cols(bw, 2 * bw), cols(2 * bw, e_conv),
                   cols(e_conv, e_dn),
                   jnp.pad(wl[:, e_dn:e_bd], ((0, 0), (0, LANES - nbd))).astype(BF16),
                   cols(e_bd, e_bd + bw), cols(e_bd + bw, e_bd + 2 * bw),
                   cols(e_bd + 2 * bw, e_da), cols(e_da, wl.shape[1]))
        mod = _adaln(cvec, w_ada, b_ada3, layer)
        y_conv, pd, pbd, qn, kn, pa, pm = _inproj(xs, mod, norm_g3, weights, conv_w, conv_b3, ln_g3, ln_b3,
                                                  cos, sin_a, sin_b, gq, gk, layer, n_lat_blocks, ctx_row)

        out_blocks = n_lat_blocks if last else n_blocks

        pbdt = jnp.swapaxes(pbd[:, :, 0:nbd].reshape(bsz, n_chunks, DN_CHUNK, nbd), 2, 3)
        qg, kg, u, w, intra, sd = _deltanet_stage_a(pd, pbd, pbdt, dn_conv_w, alog_r, bias_r, alog_c, bias_c,
                                                    layer, seq // dna_rows, t // dna_rows)
        o_fwd, o_bwd = _deltanet_stage_b(qg, kg, u, w, intra, sd, n_lat_chunks, n_chunks)

        yd_lat = _diff_attention(qn, kn, pa, da_lambda, subln3, layer, lam_init,
                                 tq=TQ, q_rows=seq, q_off=0, k_rows=t, k_off=0)
        yd_ctx = _diff_attention(qn, kn, pa, da_lambda, subln3, layer, lam_init,
                                 tq=ctx_len, q_rows=ctx_len, q_off=seq, k_rows=ctx_len, k_off=seq)

        xs = _merge(xs, y_conv, o_fwd, o_bwd, pd, yd_lat, yd_ctx, pm, mod, dng3, w_branch16, w_out16, layer,
                    n_lat_blocks, out_blocks, ctx_row)
    return xs
```

```python
import functools
import math

import jax
import jax.numpy as jnp
from jax import lax
from jax.experimental import pallas as pl
from jax.experimental.pallas import tpu as pltpu

F32 = jnp.float32
BF16 = jnp.bfloat16
HIGHEST = lax.Precision.HIGHEST

GRID_W = 64
N_BRANCH = 3
CONV_K = 31
DN_HEADS = 4
DN_CHUNK = 64
SHORT_K = 3
DA_HEADS = 4
ROPE_BASE = 10000.0
RMS_EPS = 1e-6
LN_EPS = 1e-5

LANES = 128
SUBLANES = 8
SUBLANES_BF16 = 16
HALO = SUBLANES_BF16
VMEM_LIMIT = 56 * 1024 * 1024

TM_PROJ = 256
PROJ_COLS = 512
CONV_ROWS = 64
DNA_CHUNKS = 4
DNB_CHUNKS = 4
TQ = 512
ATT_KEYS = 4096
ATT_ROWS = 512


def _silu(x):
    return x * jax.nn.sigmoid(x)


def _softplus(x):
    return jnp.maximum(x, 0.0) + jnp.log(1.0 + jnp.exp(-jnp.abs(x)))


def _dot(a, b):
    return jnp.dot(a, b, preferred_element_type=F32)


def _dot_hi(a, b):
    return jnp.dot(a, b, preferred_element_type=F32, precision=HIGHEST)


def _dot_nt(a, b):
    return lax.dot_general(a, b, (((1,), (1,)), ((), ())), preferred_element_type=F32)


def _dot_tn(a, b):
    return lax.dot_general(a, b, (((0,), (0,)), ((), ())), preferred_element_type=F32)


def _params(*sem):
    return pltpu.CompilerParams(dimension_semantics=sem, vmem_limit_bytes=VMEM_LIMIT)


def _adaln_kernel(c_ref, w_ref, b_ref, o_ref):
    o_ref[...] = _dot_hi(_silu(c_ref[...]), w_ref[0]) + b_ref[0]


def _adaln(cvec, w_ada, b_ada3, layer):
    d = cvec.shape[1]
    return pl.pallas_call(
        _adaln_kernel,
        grid=(3,),
        in_specs=[
            pl.BlockSpec((8, d), lambda j: (0, 0)),
            pl.BlockSpec((1, d, d), lambda j: (layer, 0, j)),
            pl.BlockSpec((1, 1, d), lambda j: (layer, 0, j)),
        ],
        out_specs=pl.BlockSpec((8, d), lambda j: (0, j)),
        out_shape=jax.ShapeDtypeStruct((8, 3 * d), F32),
        compiler_params=_params("parallel"),
        name="adaln",
    )(cvec, w_ada, b_ada3)


def _inproj_kernel(x_ref, xc_ref, xp_ref, xn_ref, mod_ref, g_ref, wv, wg, wz, wd, wb, wq, wk, wvz, wm,
                   cw_ref, cb_ref, lng, lnb, cos, sin_a, sin_b, gq, gk,
                   yc_o, od, ob, oq, ok, ovz, om, obt, hb_s, buf, zc_s, *, n_lat_blocks, n_blocks, ctx_row):
    d = x_ref.shape[-1]
    tm = x_ref.shape[1]
    b = pl.program_id(0)
    i = pl.program_id(1)
    row = jnp.where(i >= n_lat_blocks, ctx_row, b)
    m = mod_ref[pl.ds(row, 1), :]
    shift = m[:, 0:d]
    scale = m[:, d:2 * d]

    def modulated(x):
        ms = jnp.mean(x * x, axis=-1, keepdims=True)
        return ((x * lax.rsqrt(ms + RMS_EPS) * g_ref[0]) * (1.0 + scale) + shift).astype(BF16)

    hb_s[0:HALO, :] = modulated(xp_ref[0])
    hb_s[HALO:HALO + tm, :] = modulated(jnp.where(i >= n_lat_blocks, xc_ref[0], x_ref[0]))
    hb_s[HALO + tm:HALO + tm + HALO, :] = modulated(xn_ref[0])
    hb_ext = lambda: hb_s[...]
    hb = lambda: hb_s[HALO:HALO + tm, :]

    matmul_tasks, vector_tasks = [], []

    def project(w, o):
        n = w.shape[1]
        for c0 in range(0, n, PROJ_COLS):
            c1 = min(c0 + PROJ_COLS, n)

            def task(c0=c0, c1=c1):
                o[0, :, c0:c1] = _dot(hb(),w[:, c0:c1]).astype(o.dtype)
            matmul_tasks.append(task)

    has_prev, has_next = _segment_edges(i, n_lat_blocks, n_blocks)
    a = _dot(hb_ext(),wv[...]) * jax.nn.sigmoid(_dot(hb_ext(),wg[...]))
    zc_s[...] = _dot(hb(),wz[...])
    buf[0, HALO:HALO + tm, :] = a[HALO:HALO + tm]
    buf[0, 0:HALO, :] = jnp.where(has_prev, a[0:HALO], 0.0)
    buf[0, HALO + tm:HALO + tm + HALO, :] = jnp.where(has_next, a[HALO + tm:HALO + tm + HALO], 0.0)
    span = buf.shape[1] - SUBLANES

    def shift_task(j):
        buf[j, 0:span, :] = buf[0, j:j + span, :]

    pad = CONV_K // 2

    def conv_task(r):
        acc = jnp.zeros((CONV_ROWS, buf.shape[2]), F32) + cb_ref[0]
        for k in range(CONV_K):
            whole, part = divmod(HALO - pad + k, SUBLANES)
            start = r + whole * SUBLANES
            acc = acc + buf[part, start:start + CONV_ROWS, :] * cw_ref[0, k:k + 1, :]
        mu = jnp.mean(acc, axis=-1, keepdims=True)
        xc = acc - mu
        y = xc * lax.rsqrt(jnp.mean(xc * xc, axis=-1, keepdims=True) + LN_EPS)
        y = y * lng[0] + lnb[0]
        yc_o[0, r:r + CONV_ROWS, :] = (_silu(y) * _silu(zc_s[r:r + CONV_ROWS, :])).astype(yc_o.dtype)

    vector_tasks += [functools.partial(shift_task, j) for j in range(1, SUBLANES)]
    vector_tasks += [functools.partial(conv_task, r) for r in range(0, tm, CONV_ROWS)]

    project(wd, od)

    def gate_logits_task():
        r = _dot(hb(), wb[...])
        ob[0] = r
        rt = r.T
        for c in range(tm // DN_CHUNK):
            obt[0, c] = rt[0:4 * DN_HEADS, c * DN_CHUNK:(c + 1) * DN_CHUNK]
    matmul_tasks.append(gate_logits_task)

    dh = LANES // 2
    first_map = lax.broadcasted_iota(jnp.int32, (tm, LANES), 1) < dh
    cs, sa, sb = cos[...], sin_a[...], sin_b[...]

    def norm_rope(x, gain, scale_):
        sq = x * x
        s0 = jnp.sum(jnp.where(first_map, sq, 0.0), axis=-1, keepdims=True)
        s1 = jnp.sum(jnp.where(first_map, 0.0, sq), axis=-1, keepdims=True)
        ms = jnp.where(first_map, s0, s1) * (1.0 / dh)
        y = x * lax.rsqrt(ms + RMS_EPS) * gain
        y = y * cs + pltpu.roll(y, LANES - dh // 2, 1) * sa + pltpu.roll(y, dh // 2, 1) * sb
        return y * scale_

    q_scale = dh ** -0.5 * math.log2(math.e)
    def qk_task(w, o, gain, scale_, h):
        hs = slice(h * LANES, (h + 1) * LANES)
        o[0, :, hs] = norm_rope(_dot(hb(),w[:, hs]), gain[0], scale_).astype(o.dtype)

    for h in range(DA_HEADS):
        matmul_tasks.append(functools.partial(qk_task, wq, oq, gq, q_scale, h))
        matmul_tasks.append(functools.partial(qk_task, wk, ok, gk, 1.0, h))

    project(wvz, ovz)
    project(wm, om)

    done = 0
    for n, task in enumerate(matmul_tasks):
        task()
        while done * len(matmul_tasks) < (n + 1) * len(vector_tasks):
            vector_tasks[done]()
            done += 1


def _token_specs(tm, d, n_lat_blocks, ctx_block0, n_ctx_blocks):
    lat = pl.BlockSpec((1, tm, d), lambda b, i: (b, jnp.minimum(i, n_lat_blocks - 1), 0))
    ctx = pl.BlockSpec((1, tm, d),
                       lambda b, i: (b, ctx_block0 + jnp.clip(i - n_lat_blocks, 0, n_ctx_blocks - 1), 0))
    return lat, ctx


def _inproj(tokens, t, mod, norm_g, weights, conv_w, conv_b, ln_g, ln_b, cos, sin_a, sin_b, gq, gk, layer,
            n_lat_blocks, ctx_row):
    x_lat, x_ctx, ctx_block0 = tokens
    bsz, _, d = x_lat.shape
    tm = TM_PROJ
    nblk = t // tm
    wv, wg, wz, wd, wb, wq, wk, wvz, wm = weights
    bw = wv.shape[1]
    resident = lambda w: pl.BlockSpec(w.shape, lambda b, i: (0, 0), pipeline_mode=pl.Buffered(1))
    out = lambda n: pl.BlockSpec((1, tm, n), lambda b, i: (b, i, 0))
    lat_spec, ctx_spec = _token_specs(tm, d, n_lat_blocks, ctx_block0, nblk - n_lat_blocks)
    assert nblk - n_lat_blocks == 1
    xp, xn = _halo_specs(tm, d, 0, n_lat_blocks * tm // HALO)
    vec = lambda n: pl.BlockSpec((1, 1, n), lambda b, i: (layer, 0, 0))
    tab = pl.BlockSpec((tm, LANES), lambda b, i: (i, 0))
    widths = (bw, wd.shape[1], wb.shape[1], wq.shape[1], wk.shape[1], wvz.shape[1], wm.shape[1])
    dtypes = (BF16, BF16, F32, BF16, BF16, BF16, BF16)
    return pl.pallas_call(
        functools.partial(_inproj_kernel, n_lat_blocks=n_lat_blocks, n_blocks=nblk, ctx_row=ctx_row),
        grid=(bsz, nblk),
        in_specs=[
            lat_spec, ctx_spec, xp, xn,
            pl.BlockSpec(mod.shape, lambda b, i: (0, 0)),
            vec(d),
            resident(wv), resident(wg), resident(wz), resident(wd), resident(wb), resident(wq), resident(wk),
            resident(wvz), resident(wm),
            pl.BlockSpec((1, CONV_K, bw), lambda b, i: (layer, 0, 0)), vec(bw), vec(bw), vec(bw),
            tab, tab, tab, vec(LANES), vec(LANES),
        ],
        out_specs=[out(n) for n in widths] + [
            pl.BlockSpec((1, tm // DN_CHUNK, 4 * DN_HEADS, DN_CHUNK), lambda b, i: (b, i, 0, 0))],
        out_shape=[jax.ShapeDtypeStruct((bsz, t, n), dt) for n, dt in zip(widths, dtypes)] + [
            jax.ShapeDtypeStruct((bsz, t // DN_CHUNK, 4 * DN_HEADS, DN_CHUNK), F32)],
        scratch_shapes=[pltpu.VMEM((tm + 2 * HALO, d), BF16),
                        pltpu.VMEM((SUBLANES, tm + 2 * HALO, bw), F32),
                        pltpu.VMEM((tm, bw), F32)],
        compiler_params=_params("parallel", "parallel"),
        name="inproj",
    )(x_lat, x_ctx, x_lat, x_lat, mod, norm_g, wv, wg, wz, wd, wb, wq, wk, wvz, wm,
      conv_w, conv_b, ln_g, ln_b, cos, sin_a, sin_b, gq, gk)


def _segment_edges(i, n_lat, n_tot):
    has_prev = jnp.logical_and(i != 0, i != n_lat)
    has_next = jnp.logical_and(i != n_lat - 1, i != n_tot - 1)
    return has_prev, has_next


def _halo_specs(rows, width, col, n_halo_blocks):
    per = rows // HALO
    prev = pl.BlockSpec((1, HALO, width), lambda b, i: (b, jnp.maximum(i * per - 1, 0), col))
    nxt = pl.BlockSpec((1, HALO, width), lambda b, i: (b, jnp.minimum((i + 1) * per, n_halo_blocks - 1), col))
    return prev, nxt


def _tri_inverses(nmats, eye, blk16, blk32):
    b = lambda a: a.astype(BF16)
    nds = [jnp.where(blk16, n, 0.0) for n in nmats]
    ts = [eye - nd for nd in nds]
    pws = nds
    for _ in range(3):
        pwbs = [b(p) for p in pws]
        pws = [_dot(p, p) for p in pwbs]
        tbs = [b(t) for t in ts]
        pwbs = [b(p) for p in pws]
        prods = [_dot(t, p) for t, p in zip(tbs, pwbs)]
        ts = [t + pr for t, pr in zip(ts, prods)]
    level32 = jnp.logical_and(blk32, jnp.logical_not(blk16))
    level64 = jnp.logical_not(blk32)
    for level in (level32, level64):
        cs = [b(jnp.where(level, n, 0.0)) for n in nmats]
        tbs = [b(t) for t in ts]
        mids = [_dot(c, t) for c, t in zip(cs, tbs)]
        mids = [b(m) for m in mids]
        prods = [_dot(t, m) for t, m in zip(tbs, mids)]
        ts = [t - pr for t, pr in zip(ts, prods)]
    return ts


def _dna_kernel(q, k, v, pq, pk, pv, nq, nk, nv, bd, bdt, cw, alog_r, bias_r, alog_c, bias_c,
                qg_o, kg_o, u_o, w_o, in_o, sd_o, buf, act_s, *, n_lat, n_tot):
    i = pl.program_id(1)
    has_prev, has_next = _segment_edges(i, n_lat, n_tot)
    cc = DN_CHUNK
    rows = q.shape[1]
    bw = q.shape[-1]
    hd = bw // DN_HEADS
    nh = DN_HEADS
    for idx, (m_, p_, n_) in enumerate(((q, pq, nq), (k, pk, nk), (v, pv, nv))):
        cols = slice(idx * bw, (idx + 1) * bw)
        buf[HALO:HALO + rows, cols] = m_[0].astype(F32)
        buf[0:HALO, cols] = jnp.where(has_prev, p_[0].astype(F32), 0.0)
        buf[HALO + rows:HALO + rows + HALO, cols] = jnp.where(has_next, n_[0].astype(F32), 0.0)

    ii = lax.broadcasted_iota(jnp.int32, (cc, cc), 0)
    jj = lax.broadcasted_iota(jnp.int32, (cc, cc), 1)
    low = jj <= ii
    upp = jj >= ii
    lowf = low.astype(F32)
    uppf = upp.astype(F32)
    eye = (ii == jj).astype(F32)
    blk16 = (ii // 16) == (jj // 16)
    blk32 = (ii // 32) == (jj // 32)
    incl = (low, upp)
    strict = (jj < ii, jj > ii)
    last_row = (cc - 1, 0)

    n_ch = rows // cc
    for r0 in range(0, rows, cc):
        conv = (buf[HALO + r0:HALO + r0 + cc, :] * cw[0, 1:2, :]
                + buf[HALO - 1 + r0:HALO - 1 + r0 + cc, :] * cw[0, 0:1, :]
                + buf[HALO + 1 + r0:HALO + 1 + r0 + cc, :] * cw[0, 2:3, :])
        act_s[r0:r0 + cc, :] = _silu(conv)
    beta_cs, gcum_cs, gcum_rs = [], [], []
    heads = []
    for ch in range(n_ch):
        r0 = ch * cc
        act = act_s[r0:r0 + cc, :]
        x = bd[0, r0:r0 + cc, :]
        beta_cs.append(jax.nn.sigmoid(x))
        g_c = -jnp.exp(alog_r[0]) * _softplus(x + bias_r[0])
        xt = bdt[0, ch]
        g_r = -jnp.exp(alog_c[0]) * _softplus(xt + bias_c[0])
        gcum_cs.append((_dot_hi(lowf, g_c), _dot_hi(uppf, g_c)))
        gcum_rs.append((_dot_hi(g_r, uppf), _dot_hi(g_r, lowf)))
        for h in range(nh):
            qh = act[:, h * hd:(h + 1) * hd]
            kh = act[:, bw + h * hd:bw + (h + 1) * hd]
            vh = act[:, 2 * bw + h * hd:2 * bw + (h + 1) * hd]
            qh = qh * lax.rsqrt(jnp.sum(qh * qh, axis=-1, keepdims=True) + RMS_EPS) * (hd ** -0.5)
            kh = kh * lax.rsqrt(jnp.sum(kh * kh, axis=-1, keepdims=True) + RMS_EPS)
            heads.append((ch, h, qh, kh, vh))
    k16 = [kh.astype(BF16) for (_, _, _, kh, _) in heads]
    q16 = [qh.astype(BF16) for (_, _, qh, _, _) in heads]
    kks = [_dot_nt(kb, kb) for kb in k16]
    qks = [_dot_nt(qb, kb) for qb, kb in zip(q16, k16)]

    inst = []
    nmats = []
    for (ch, h, _, _, _), kk in zip(heads, kks):
        for d in range(2):
            col = 2 * nh + nh * d + h
            gi = gcum_cs[ch][d][:, col:col + 1]
            gj = gcum_rs[ch][d][col:col + 1, :]
            dm = jnp.where(incl[d], jnp.exp(jnp.where(incl[d], gi - gj, 0.0)), 0.0)
            beta = beta_cs[ch][:, nh * d + h:nh * d + h + 1]
            nmats.append(jnp.where(strict[d], beta * kk * dm, 0.0))
            inst.append((ch, h, d, gi, beta, dm))
    ainvs = _tri_inverses(nmats, eye, blk16, blk32)

    rhss = []
    for (ch, h, d, gi, beta, _), ainv in zip(inst, ainvs):
        _, _, _, kh, vh = heads[ch * nh + h]
        rhss.append(jnp.concatenate([vh * beta, kh * (beta * jnp.exp(gi))], axis=1).astype(BF16))
    a16 = [a.astype(BF16) for a in ainvs]
    sols = [_dot(a, r) for a, r in zip(a16, rhss)]

    sd_rows = [[[] for _ in range(2)] for _ in range(n_ch)]
    for (ch, h, d, gi, _, dm), sol in zip(inst, sols):
        _, _, qh, kh, _ = heads[ch * nh + h]
        rs = slice(ch * cc, (ch + 1) * cc)
        hs = slice(h * hd, (h + 1) * hd)
        glast = gi[last_row[d]:last_row[d] + 1, :]
        u_o[d, 0, rs, hs] = sol[:, 0:hd].astype(u_o.dtype)
        w_o[d, 0, rs, hs] = sol[:, hd:2 * hd].astype(w_o.dtype)
        qg_o[d, 0, rs, hs] = (qh * jnp.exp(gi)).astype(qg_o.dtype)
        kg_o[d, 0, rs, hs] = (kh * jnp.exp(glast - gi)).astype(kg_o.dtype)
        in_o[d, 0, rs, h * cc:(h + 1) * cc] = (qks[ch * nh + h] * dm).astype(in_o.dtype)
        sd_rows[ch][d].append(jnp.broadcast_to(jnp.exp(glast), (1, LANES)))
    for ch in range(n_ch):
        for d in range(2):
            sd_o[d, 0, ch] = jnp.concatenate(sd_rows[ch][d] + [jnp.zeros((8 - nh, LANES), F32)], axis=0)


def _deltanet_stage_a(pd, pbd, pbdt, dn_conv_w, alog_r, bias_r, alog_c, bias_c, layer, n_lat, n_tot):
    bsz, t, _ = pd.shape
    bw = dn_conv_w.shape[-1] // 3
    cc = DN_CHUNK
    rows = DNA_CHUNKS * cc
    main = lambda col: pl.BlockSpec((1, rows, bw), lambda b, i: (b, i, col))
    halos = [_halo_specs(rows, bw, col, t // HALO) for col in range(3)]
    small = lambda a: pl.BlockSpec((1,) + a.shape[1:], lambda b, i: (layer,) + (0,) * (a.ndim - 1))
    tok = lambda n, dt: jax.ShapeDtypeStruct((2, bsz, t, n), dt)
    tok_spec = lambda n: pl.BlockSpec((2, 1, rows, n), lambda b, i: (0, b, i, 0))
    return pl.pallas_call(
        functools.partial(_dna_kernel, n_lat=n_lat, n_tot=n_tot),
        grid=(bsz, n_tot),
        in_specs=[main(0), main(1), main(2),
                  halos[0][0], halos[1][0], halos[2][0], halos[0][1], halos[1][1], halos[2][1],
                  pl.BlockSpec((1, rows, LANES), lambda b, i: (b, i, 0)),
                  pl.BlockSpec((1, DNA_CHUNKS, 4 * DN_HEADS, cc), lambda b, i: (b, i, 0, 0)),
                  small(dn_conv_w), small(alog_r), small(bias_r), small(alog_c), small(bias_c)],
        out_specs=[tok_spec(bw), tok_spec(bw), tok_spec(bw), tok_spec(bw), tok_spec(DN_HEADS * cc),
                   pl.BlockSpec((2, 1, DNA_CHUNKS, 8, LANES), lambda b, i: (0, b, i, 0, 0))],
        out_shape=[tok(bw, BF16), tok(bw, BF16), tok(bw, BF16), tok(bw, BF16), tok(DN_HEADS * cc, BF16),
                   jax.ShapeDtypeStruct((2, bsz, t // cc, 8, LANES), F32)],
        scratch_shapes=[pltpu.VMEM((rows + 2 * HALO, 3 * bw), F32), pltpu.VMEM((rows, 3 * bw), F32)],
        compiler_params=_params("parallel", "parallel"),
        name="deltanet_a",
    )(pd, pd, pd, pd, pd, pd, pd, pd, pd, pbd, pbdt, dn_conv_w, alog_r, bias_r, alog_c, bias_c)


def _dnb_kernel(qg_f, kg_f, u_f, w_f, in_f, sd_f, qg_b, kg_b, u_b, w_b, in_b, sd_b, of_ref, ob_ref, state):
    cc = DN_CHUNK
    hd = state.shape[-1]
    bsz = state.shape[1]

    @pl.when(pl.program_id(0) == 0)
    def _():
        state[...] = jnp.zeros_like(state)

    dirs = ((qg_f, kg_f, u_f, w_f, in_f, sd_f, of_ref), (qg_b, kg_b, u_b, w_b, in_b, sd_b, ob_ref))
    chains = [(d, b, h) for d in range(2) for b in range(bsz) for h in range(DN_HEADS)]
    hs = lambda h: slice(h * hd, (h + 1) * hd)
    n_ch = qg_f.shape[2] // cc
    s32 = [state[d, b, h] for d, b, h in chains]
    for step in range(n_ch):
        chunk = (step, n_ch - 1 - step)
        rs = [slice(c * cc, (c + 1) * cc) for c in chunk]
        s16 = [s.astype(BF16) for s in s32]
        ws = [_dot(dirs[d][3][0, b, rs[d], hs(h)], s) for (d, b, h), s in zip(chains, s16)]
        qs = [_dot(dirs[d][0][0, b, rs[d], hs(h)], s) for (d, b, h), s in zip(chains, s16)]
        vnew = [(dirs[d][2][0, b, rs[d], hs(h)].astype(F32) - x).astype(BF16) for (d, b, h), x in zip(chains, ws)]
        intra = [_dot(dirs[d][4][0, b, rs[d], h * cc:(h + 1) * cc], v) for (d, b, h), v in zip(chains, vnew)]
        upd = [_dot_tn(dirs[d][1][0, b, rs[d], hs(h)], v) for (d, b, h), v in zip(chains, vnew)]
        for (d, b, h), o1, o2 in zip(chains, qs, intra):
            dirs[d][6][b, rs[d], hs(h)] = (o1 + o2).astype(dirs[d][6].dtype)
        s32 = [s * dirs[d][5][0, b, chunk[d], h:h + 1, :] + up for (d, b, h), s, up in zip(chains, s32, upd)]
    for (d, b, h), s in zip(chains, s32):
        state[d, b, h] = s


def _deltanet_stage_b(qg, kg, u, w, intra, sd, n_lat, n_tot):
    _, bsz, t, bw = qg.shape
    rows = DNB_CHUNKS * DN_CHUNK
    hd = bw // DN_HEADS
    n_ctx = n_tot - n_lat
    block_f = lambda s: jnp.where(s < n_ctx, n_lat + s, s - n_ctx)
    block_b = lambda s: n_tot - 1 - s

    def specs(d, block):
        tok = lambda n: pl.BlockSpec((1, bsz, rows, n), lambda s: (d, 0, block(s), 0))
        return [tok(bw), tok(bw), tok(bw), tok(bw), tok(DN_HEADS * DN_CHUNK),
                pl.BlockSpec((1, bsz, DNB_CHUNKS, 8, LANES), lambda s: (d, 0, block(s), 0, 0))]

    out = lambda block: pl.BlockSpec((bsz, rows, bw), lambda s: (0, block(s), 0))
    args = (qg, kg, u, w, intra, sd)
    return pl.pallas_call(
        _dnb_kernel,
        grid=(n_tot,),
        in_specs=specs(0, block_f) + specs(1, block_b),
        out_specs=[out(block_f), out(block_b)],
        out_shape=[jax.ShapeDtypeStruct((bsz, t, bw), BF16)] * 2,
        scratch_shapes=[pltpu.VMEM((2, bsz, DN_HEADS, hd, hd), F32)],
        compiler_params=_params("arbitrary"),
        name="deltanet_b",
    )(*args, *args)


def _attn_kernel(q_ref, k_ref, v_ref, z_ref, lam_ref, g_ref, o_ref, qs, vx, m_s, acc, *, lam_init):
    tq = q_ref.shape[1]
    n_keys = k_ref.shape[1]
    dh = LANES // 2
    n_full, rem = divmod(n_keys, ATT_KEYS)

    @pl.when(pl.program_id(2) == 0)
    def _():
        vx[:, 0:LANES] = v_ref[0]
        vx[:, LANES:2 * LANES] = jnp.ones((n_keys, LANES), vx.dtype)

    q = q_ref[0].astype(F32)
    lane = lax.broadcasted_iota(jnp.int32, q.shape, 1)
    qs[0:tq, :] = jnp.where(lane < dh, q, 0.0).astype(qs.dtype)
    qs[tq:2 * tq, :] = jnp.where(lane >= dh, q, 0.0).astype(qs.dtype)
    m_s[...] = jnp.full_like(m_s, -jnp.inf)
    acc[...] = jnp.zeros_like(acc)

    key_blocks = [(j * ATT_KEYS, ATT_KEYS) for j in range(n_full)] + ([(n_full * ATT_KEYS, rem)] if rem else [])
    groups = [slice(r0, r0 + ATT_ROWS) for r0 in range(0, 2 * tq, ATT_ROWS)]
    tasks = [(k0, size, rows) for k0, size in key_blocks for rows in groups]
    scores = lambda k0, size, rows: _dot_nt(qs[rows, :], k_ref[0, k0:k0 + size, :])
    s_next = scores(*tasks[0])
    for i, (k0, size, rows) in enumerate(tasks):
        s = s_next
        if i + 1 < len(tasks):
            s_next = scores(*tasks[i + 1])
        m_prev = m_s[rows, :]
        m_new = jnp.maximum(m_prev, jnp.max(s, axis=1, keepdims=True))
        alpha = jnp.exp2(m_prev - m_new)
        p = jnp.exp2(s - m_new[:, 0:1])
        pv = _dot(p.astype(BF16), vx[k0:k0 + size, :])
        acc[rows, 0:LANES] = alpha * acc[rows, 0:LANES] + pv[:, 0:LANES]
        acc[rows, LANES:2 * LANES] = alpha * acc[rows, LANES:2 * LANES] + pv[:, LANES:2 * LANES]
        m_s[rows, :] = m_new

    lm = lam_ref[0]
    lam = (jnp.exp(jnp.sum(lm[0:1] * lm[1:2], axis=1, keepdims=True))
           - jnp.exp(jnp.sum(lm[2:3] * lm[3:4], axis=1, keepdims=True)) + lam_init)
    on = acc[:, 0:LANES] / acc[:, LANES:2 * LANES]
    o = on[0:tq] - lam * on[tq:2 * tq]
    y = o * lax.rsqrt(jnp.mean(o * o, axis=-1, keepdims=True) + RMS_EPS) * g_ref[0] * (1.0 - lam_init)
    o_ref[0] = (y * _silu(z_ref[0].astype(F32))).astype(o_ref.dtype)


def _diff_attention(qn, kn, pa, da_lambda, subln_g, layer, lam_init, *, tq, q_rows, q_off, k_rows, k_off):
    bsz = qn.shape[0]
    v_col = 0
    z_col = DA_HEADS
    qb, kb = q_off // tq, k_off // k_rows
    return pl.pallas_call(
        functools.partial(_attn_kernel, lam_init=lam_init),
        grid=(bsz, DA_HEADS, q_rows // tq),
        in_specs=[pl.BlockSpec((1, tq, LANES), lambda b, h, i: (b, i + qb, h)),
                  pl.BlockSpec((1, k_rows, LANES), lambda b, h, i: (b, kb, h)),
                  pl.BlockSpec((1, k_rows, LANES), lambda b, h, i: (b, kb, v_col + h)),
                  pl.BlockSpec((1, tq, LANES), lambda b, h, i: (b, i + qb, z_col + h)),
                  pl.BlockSpec((1,) + da_lambda.shape[1:], lambda b, h, i: (layer, 0, 0)),
                  pl.BlockSpec((1, 1, LANES), lambda b, h, i: (layer, 0, 0))],
        out_specs=pl.BlockSpec((1, tq, LANES), lambda b, h, i: (b, i, h)),
        out_shape=jax.ShapeDtypeStruct((bsz, q_rows, DA_HEADS * LANES), BF16),
        scratch_shapes=[pltpu.VMEM((2 * tq, LANES), BF16),
                        pltpu.VMEM((k_rows, 2 * LANES), BF16),
                        pltpu.VMEM((2 * tq, LANES), F32),
                        pltpu.VMEM((2 * tq, 2 * LANES), F32)],
        compiler_params=_params("parallel", "parallel", "arbitrary"),
        name="diff_attn",
    )(qn, kn, pa, pa, da_lambda, subln_g)


def _merge_kernel(x_ref, xc_ref, yc, of, ob, dz, ydl, ydc, mg, mod_ref, dng, wb, wo, o_ref, *,
                  n_lat_blocks, ctx_row):
    d = x_ref.shape[-1]
    b = pl.program_id(0)
    i = pl.program_id(1)
    is_ctx = i >= n_lat_blocks
    row = jnp.where(is_ctx, ctx_row, b)
    gate = mod_ref[pl.ds(row, 1), 2 * d:3 * d]
    o = of[0].astype(F32) + ob[0].astype(F32)
    hd = dng.shape[-1]
    parts = []
    for h in range(DN_HEADS):
        oh = o[:, h * hd:(h + 1) * hd]
        parts.append(oh * lax.rsqrt(jnp.mean(oh * oh, axis=-1, keepdims=True) + RMS_EPS) * dng[0])
    ydn = (jnp.concatenate(parts, axis=1) * _silu(dz[0].astype(F32))).astype(BF16)
    yda = jnp.where(is_ctx, ydc[0], ydl[0])
    merged = (jax.nn.sigmoid(mg[0, :, 0:d].astype(F32)) * _dot(yc[0], wb[0, 0])
              + jax.nn.sigmoid(mg[0, :, d:2 * d].astype(F32)) * _dot(ydn, wb[0, 1])
              + jax.nn.sigmoid(mg[0, :, 2 * d:3 * d].astype(F32)) * _dot(yda, wb[0, 2]))
    x = jnp.where(is_ctx, xc_ref[0], x_ref[0])
    o_ref[0] = x + gate * _dot(merged.astype(BF16), wo[0])


def _merge(tokens, y_conv, o_fwd, o_bwd, pd, yd_lat, yd_ctx, pm, mod, dn_norm_g, w_branch, w_out, layer,
           n_lat_blocks, n_blocks, ctx_row):
    x_lat, x_ctx, ctx_block0 = tokens
    bsz, _, d = x_lat.shape
    tm = TM_PROJ
    bw = y_conv.shape[-1]
    n_ctx_blocks = yd_ctx.shape[1] // tm
    tok = lambda n, col=0: pl.BlockSpec((1, tm, n), lambda b, i: (b, i, col))
    lat_spec, ctx_spec = _token_specs(tm, d, n_lat_blocks, ctx_block0, n_ctx_blocks)
    return pl.pallas_call(
        functools.partial(_merge_kernel, n_lat_blocks=n_lat_blocks, ctx_row=ctx_row),
        grid=(bsz, n_blocks),
        in_specs=[lat_spec, ctx_spec, tok(bw), tok(bw), tok(bw), tok(bw, 3),
                  pl.BlockSpec((1, tm, bw), lambda b, i: (b, jnp.minimum(i, n_lat_blocks - 1), 0)),
                  pl.BlockSpec((1, tm, bw),
                               lambda b, i: (b, jnp.clip(i - n_lat_blocks, 0, n_ctx_blocks - 1), 0)),
                  tok(N_BRANCH * d),
                  pl.BlockSpec(mod.shape, lambda b, i: (0, 0)),
                  pl.BlockSpec((1, 1, dn_norm_g.shape[-1]), lambda b, i: (layer, 0, 0)),
                  pl.BlockSpec((1,) + w_branch.shape[1:], lambda b, i: (layer, 0, 0, 0)),
                  pl.BlockSpec((1,) + w_out.shape[1:], lambda b, i: (layer, 0, 0))],
        out_specs=tok(d),
        out_shape=jax.ShapeDtypeStruct((bsz, n_blocks * tm, d), F32),
        compiler_params=_params("parallel", "parallel"),
        name="merge",
    )(x_lat, x_ctx, y_conv, o_fwd, o_bwd, pd, yd_lat, yd_ctx, pm, mod, dn_norm_g, w_branch, w_out)


def _rope_tables(seq, ctx_len, dh):
    n_freq = dh // 4
    inv_freq = ROPE_BASE ** (-jnp.arange(n_freq, dtype=F32) / n_freq)
    n_rows = seq // GRID_W
    row_ang = jnp.arange(n_rows, dtype=F32)[:, None] * inv_freq
    col_ang = jnp.arange(GRID_W, dtype=F32)[:, None] * inv_freq
    cos_r, sin_r, cos_c, sin_c = lax.optimization_barrier(
        (jnp.cos(row_ang), jnp.sin(row_ang), jnp.cos(col_ang), jnp.sin(col_ang)))

    def table(by_row, by_col):
        r = jnp.broadcast_to(by_row[:, None, :], (n_rows, GRID_W, n_freq)).reshape(seq, n_freq)
        c = jnp.broadcast_to(by_col[None, :, :], (n_rows, GRID_W, n_freq)).reshape(seq, n_freq)
        return jnp.tile(jnp.concatenate([r, c], axis=-1), (1, 2 * LANES // dh))

    cos = table(cos_r, cos_c)
    sin = table(sin_r, sin_c)
    first_half = (jnp.arange(LANES) % dh) < dh // 2
    sin_a = jnp.where(first_half, -sin, 0.0)
    sin_b = jnp.where(first_half, 0.0, sin)
    pad = lambda tbl, fill: jnp.concatenate([tbl, jnp.full((ctx_len, LANES), fill, F32)], axis=0)
    return pad(cos, 1.0), pad(sin_a, 0.0), pad(sin_b, 0.0)


def kernel(x, c, ctx, c_ctx, w_ada, b_ada, norm_g, w_in, conv_w, conv_b, conv_ln_g, conv_ln_b, dn_conv_w,
           dn_a_log, dn_dt_bias, dn_norm_g, da_q_norm_g, da_k_norm_g, da_lambda, da_subln_g, w_branch, w_out):
    bsz, seq, d = x.shape
    ctx_len = ctx.shape[1]
    depth = w_in.shape[0]
    bw = d // 2
    dh = bw // (2 * DA_HEADS)
    t = seq + ctx_len
    dna_rows = DNA_CHUNKS * DN_CHUNK
    dnb_rows = DNB_CHUNKS * DN_CHUNK
    assert seq % dnb_rows == 0 and ctx_len % dnb_rows == 0
    assert 2 * dh == LANES and bw // DN_HEADS == LANES and bsz + 1 <= 8
    assert seq % TQ == 0 and seq % ctx_len == 0 and seq % TM_PROJ == 0 and ctx_len % TM_PROJ == 0
    assert seq % dna_rows == 0 and ctx_len % dna_rows == 0 and (2 * TQ) % ATT_ROWS == 0
    n_lat_blocks, n_blocks = seq // TM_PROJ, t // TM_PROJ
    ctx_row = bsz

    tokens = (x, ctx, 0)
    cvec =jnp.concatenate([c, c_ctx[None, :], jnp.zeros((8 - bsz - 1, d), F32)], axis=0)
    cos, sin_a, sin_b = _rope_tables(seq, ctx_len, dh)

    e_conv, e_dn = 3 * bw, 7 * bw
    e_bd = e_dn + 4 * DN_HEADS
    e_da = e_bd + 4 * bw
    nbd = 4 * DN_HEADS
    row3 = lambda a: a.reshape(depth, 1, a.shape[-1])
    gate_row = lambda a: jnp.pad(a.reshape(depth, 1, 2 * DN_HEADS), ((0, 0), (0, 0), (2 * DN_HEADS, LANES - nbd)))
    gate_col = lambda a: jnp.pad(a.reshape(depth, 2 * DN_HEADS, 1), ((0, 0), (2 * DN_HEADS, 0), (0, 0)))
    alog_r, bias_r = gate_row(dn_a_log), gate_row(dn_dt_bias)
    alog_c, bias_c = gate_col(dn_a_log), gate_col(dn_dt_bias)
    tile2 = lambda a: row3(jnp.tile(a, (1, 2)))
    gq, gk = tile2(da_q_norm_g), tile2(da_k_norm_g)
    b_ada3, norm_g3 = row3(b_ada), row3(norm_g)
    conv_b3, ln_g3, ln_b3 = row3(conv_b), row3(conv_ln_g), row3(conv_ln_b)
    dng3, subln3 = row3(dn_norm_g), row3(da_subln_g)
    w_branch16, w_out16 = w_branch.astype(BF16), w_out.astype(BF16)

    for layer in range(depth):
        last = layer == depth - 1
        lam_init = 0.8 - 0.6 * math.exp(-0.3 * layer)
        wl = w_in[layer]
        cols = lambda lo, hi: wl[:, lo:hi].astype(BF16)
        weights = (cols(0, bw), cols(bw, 2 * bw), cols(2 * bw, e_conv),
                   cols(e_conv, e_dn),
                   jnp.pad(wl[:, e_dn:e_bd], ((0, 0), (0, LANES - nbd))).astype(BF16),
                   cols(e_bd, e_bd + bw), cols(e_bd + bw, e_bd + 2 * bw),
                   cols(e_bd + 2 * bw, e_da), cols(e_da, wl.shape[1]))
        mod = _adaln(cvec, w_ada, b_ada3, layer)
        y_conv, pd, pbd, qn, kn, pa, pm, pbdt = _inproj(tokens, t, mod, norm_g3, weights, conv_w, conv_b3, ln_g3, ln_b3,
                                                  cos, sin_a, sin_b, gq, gk, layer, n_lat_blocks, ctx_row)

        out_blocks = n_lat_blocks if last else n_blocks

        qg, kg, u, w, intra, sd = _deltanet_stage_a(pd, pbd, pbdt, dn_conv_w, alog_r, bias_r, alog_c, bias_c,
                                                    layer, seq // dna_rows, t // dna_rows)
        o_fwd, o_bwd = _deltanet_stage_b(qg, kg, u, w, intra, sd, seq // dnb_rows, t // dnb_rows)

        yd_lat = _diff_attention(qn, kn, pa, da_lambda, subln3, layer, lam_init,
                                 tq=TQ, q_rows=seq, q_off=0, k_rows=t, k_off=0)
        yd_ctx = _diff_attention(qn, kn, pa, da_lambda, subln3, layer, lam_init,
                                 tq=ctx_len, q_rows=ctx_len, q_off=seq, k_rows=ctx_len, k_off=seq)

        xs = _merge(tokens, y_conv, o_fwd, o_bwd, pd, yd_lat, yd_ctx, pm, mod, dng3, w_branch16, w_out16, layer,
                    n_lat_blocks, out_blocks, ctx_row)
        tokens = (xs, xs, n_lat_blocks)
    return xs
```

```python
import functools
import math

import jax
import jax.numpy as jnp
from jax import lax
from jax.experimental import pallas as pl
from jax.experimental.pallas import tpu as pltpu

F32 = jnp.float32
BF16 = jnp.bfloat16
HIGHEST = lax.Precision.HIGHEST

GRID_W = 64
N_BRANCH = 3
CONV_K = 31
DN_HEADS = 4
DN_CHUNK = 64
SHORT_K = 3
DA_HEADS = 4
ROPE_BASE = 10000.0
RMS_EPS = 1e-6
LN_EPS = 1e-5

LANES = 128
SUBLANES = 8
SUBLANES_BF16 = 16
HALO = SUBLANES_BF16
VMEM_LIMIT = 56 * 1024 * 1024

TM_PROJ = 256
PROJ_COLS = 512
CONV_ROWS = 64
DNA_CHUNKS = 4
DNB_CHUNKS = 4
TQ = 512
ATT_KEYS = 4096
ATT_ROWS = 512


def _silu(x):
    return x * jax.nn.sigmoid(x)


def _softplus(x):
    return jnp.maximum(x, 0.0) + jnp.log(1.0 + jnp.exp(-jnp.abs(x)))


def _dot(a, b):
    return jnp.dot(a, b, preferred_element_type=F32)


def _dot_hi(a, b):
    return jnp.dot(a, b, preferred_element_type=F32, precision=HIGHEST)


def _dot_nt(a, b):
    return lax.dot_general(a, b, (((1,), (1,)), ((), ())), preferred_element_type=F32)


def _dot_tn(a, b):
    return lax.dot_general(a, b, (((0,), (0,)), ((), ())), preferred_element_type=F32)


def _params(*sem):
    return pltpu.CompilerParams(dimension_semantics=sem, vmem_limit_bytes=VMEM_LIMIT)


def _adaln_kernel(c_ref, w_ref, b_ref, o_ref):
    o_ref[...] = _dot_hi(_silu(c_ref[...]), w_ref[0]) + b_ref[0]


def _adaln(cvec, w_ada, b_ada3, layer):
    d = cvec.shape[1]
    return pl.pallas_call(
        _adaln_kernel,
        grid=(3,),
        in_specs=[
            pl.BlockSpec((8, d), lambda j: (0, 0)),
            pl.BlockSpec((1, d, d), lambda j: (layer, 0, j)),
            pl.BlockSpec((1, 1, d), lambda j: (layer, 0, j)),
        ],
        out_specs=pl.BlockSpec((8, d), lambda j: (0, j)),
        out_shape=jax.ShapeDtypeStruct((8, 3 * d), F32),
        compiler_params=_params("parallel"),
        name="adaln",
    )(cvec, w_ada, b_ada3)


def _inproj_kernel(x_ref, xc_ref, xp_ref, xn_ref, mod_ref, g_ref, wv, wg, wz, wd, wb, wq, wk, wvz, wm,
                   cw_ref, cb_ref, lng, lnb, cos, sin_a, sin_b, gq, gk,
                   yc_o, od, ob, oq, ok, ovz, om, obt, hb_s, buf, zc_s, *, n_lat_blocks, n_blocks, ctx_row):
    d = x_ref.shape[-1]
    tm = x_ref.shape[1]
    b = pl.program_id(0)
    i = pl.program_id(1)
    row = jnp.where(i >= n_lat_blocks, ctx_row, b)
    m = mod_ref[pl.ds(row, 1), :]
    shift = m[:, 0:d]
    scale = m[:, d:2 * d]

    def modulated(x):
        ms = jnp.mean(x * x, axis=-1, keepdims=True)
        return ((x * lax.rsqrt(ms + RMS_EPS) * g_ref[0]) * (1.0 + scale) + shift).astype(BF16)

    hb_s[0:HALO, :] = modulated(xp_ref[0])
    hb_s[HALO:HALO + tm, :] = modulated(jnp.where(i >= n_lat_blocks, xc_ref[0], x_ref[0]))
    hb_s[HALO + tm:HALO + tm + HALO, :] = modulated(xn_ref[0])
    hb_ext = lambda: hb_s[...]
    hb = lambda: hb_s[HALO:HALO + tm, :]

    def project(w, o):
        n = w.shape[1]
        for c0 in range(0, n, PROJ_COLS):
            c1 = min(c0 + PROJ_COLS, n)
            o[0, :, c0:c1] = _dot(hb(), w[:, c0:c1]).astype(o.dtype)

    has_prev, has_next = _segment_edges(i, n_lat_blocks, n_blocks)
    a = _dot(hb_ext(), wv[...]) * jax.nn.sigmoid(_dot(hb_ext(), wg[...]))
    zc_s[...] = _dot(hb(), wz[...])
    buf[0, HALO:HALO + tm, :] = a[HALO:HALO + tm]
    buf[0, 0:HALO, :] = jnp.where(has_prev, a[0:HALO], 0.0)
    buf[0, HALO + tm:HALO + tm + HALO, :] = jnp.where(has_next, a[HALO + tm:HALO + tm + HALO], 0.0)
    span = buf.shape[1] - SUBLANES
    for j in range(1, SUBLANES):
        buf[j, 0:span, :] = buf[0, j:j + span, :]

    project(wd, od)

    logits = _dot(hb(), wb[...])
    ob[0] = logits
    logits_t = logits.T
    for c in range(tm // DN_CHUNK):
        obt[0, c] = logits_t[0:4 * DN_HEADS, c * DN_CHUNK:(c + 1) * DN_CHUNK]

    dh = LANES // 2
    first_map = lax.broadcasted_iota(jnp.int32, (tm, LANES), 1) < dh
    cs, sa, sb = cos[...], sin_a[...], sin_b[...]

    def norm_rope(x, gain, scale_):
        sq = x * x
        s0 = jnp.sum(jnp.where(first_map, sq, 0.0), axis=-1, keepdims=True)
        s1 = jnp.sum(jnp.where(first_map, 0.0, sq), axis=-1, keepdims=True)
        ms = jnp.where(first_map, s0, s1) * (1.0 / dh)
        y = x * lax.rsqrt(ms + RMS_EPS) * gain
        y = y * cs + pltpu.roll(y, LANES - dh // 2, 1) * sa + pltpu.roll(y, dh // 2, 1) * sb
        return y * scale_

    q_scale = dh ** -0.5 * math.log2(math.e)
    for h in range(DA_HEADS):
        hs = slice(h * LANES, (h + 1) * LANES)
        oq[0, :, hs] = norm_rope(_dot(hb(), wq[:, hs]), gq[0], q_scale).astype(oq.dtype)
        ok[0, :, hs] = norm_rope(_dot(hb(), wk[:, hs]), gk[0], 1.0).astype(ok.dtype)

    project(wvz, ovz)
    project(wm, om)

    pad = CONV_K // 2
    for r in range(0, tm, CONV_ROWS):
        acc = jnp.zeros((CONV_ROWS, buf.shape[2]), F32) + cb_ref[0]
        for k in range(CONV_K):
            whole, part = divmod(HALO - pad + k, SUBLANES)
            start = r + whole * SUBLANES
            acc = acc + buf[part, start:start + CONV_ROWS, :] * cw_ref[0, k:k + 1, :]
        mu = jnp.mean(acc, axis=-1, keepdims=True)
        xc = acc - mu
        y = xc * lax.rsqrt(jnp.mean(xc * xc, axis=-1, keepdims=True) + LN_EPS)
        y = y * lng[0] + lnb[0]
        yc_o[0, r:r + CONV_ROWS, :] = (_silu(y) * _silu(zc_s[r:r + CONV_ROWS, :])).astype(yc_o.dtype)


def _token_specs(tm, d, n_lat_blocks, ctx_block0, n_ctx_blocks):
    lat = pl.BlockSpec((1, tm, d), lambda b, i: (b, jnp.minimum(i, n_lat_blocks - 1), 0))
    ctx = pl.BlockSpec((1, tm, d),
                       lambda b, i: (b, ctx_block0 + jnp.clip(i - n_lat_blocks, 0, n_ctx_blocks - 1), 0))
    return lat, ctx


def _inproj(tokens, t, mod, norm_g, weights, conv_w, conv_b, ln_g, ln_b, cos, sin_a, sin_b, gq, gk, layer,
            n_lat_blocks, ctx_row):
    x_lat, x_ctx, ctx_block0 = tokens
    bsz, _, d = x_lat.shape
    tm = TM_PROJ
    nblk = t // tm
    wv, wg, wz, wd, wb, wq, wk, wvz, wm = weights
    bw = wv.shape[1]
    resident = lambda w: pl.BlockSpec(w.shape, lambda b, i: (0, 0), pipeline_mode=pl.Buffered(1))
    out = lambda n: pl.BlockSpec((1, tm, n), lambda b, i: (b, i, 0))
    lat_spec, ctx_spec = _token_specs(tm, d, n_lat_blocks, ctx_block0, nblk - n_lat_blocks)
    assert nblk - n_lat_blocks == 1
    xp, xn = _halo_specs(tm, d, 0, n_lat_blocks * tm // HALO)
    vec = lambda n: pl.BlockSpec((1, 1, n), lambda b, i: (layer, 0, 0))
    tab = pl.BlockSpec((tm, LANES), lambda b, i: (i, 0))
    widths = (bw, wd.shape[1], wb.shape[1], wq.shape[1], wk.shape[1], wvz.shape[1], wm.shape[1])
    dtypes = (BF16, BF16, F32, BF16, BF16, BF16, BF16)
    return pl.pallas_call(
        functools.partial(_inproj_kernel, n_lat_blocks=n_lat_blocks, n_blocks=nblk, ctx_row=ctx_row),
        grid=(bsz, nblk),
        in_specs=[
            lat_spec, ctx_spec, xp, xn,
            pl.BlockSpec(mod.shape, lambda b, i: (0, 0)),
            vec(d),
            resident(wv), resident(wg), resident(wz), resident(wd), resident(wb), resident(wq), resident(wk),
            resident(wvz), resident(wm),
            pl.BlockSpec((1, CONV_K, bw), lambda b, i: (layer, 0, 0)), vec(bw), vec(bw), vec(bw),
            tab, tab, tab, vec(LANES), vec(LANES),
        ],
        out_specs=[out(n) for n in widths] + [
            pl.BlockSpec((1, tm // DN_CHUNK, 4 * DN_HEADS, DN_CHUNK), lambda b, i: (b, i, 0, 0))],
        out_shape=[jax.ShapeDtypeStruct((bsz, t, n), dt) for n, dt in zip(widths, dtypes)] + [
            jax.ShapeDtypeStruct((bsz, t // DN_CHUNK, 4 * DN_HEADS, DN_CHUNK), F32)],
        scratch_shapes=[pltpu.VMEM((tm + 2 * HALO, d), BF16),
                        pltpu.VMEM((SUBLANES, tm + 2 * HALO, bw), F32),
                        pltpu.VMEM((tm, bw), F32)],
        compiler_params=_params("parallel", "parallel"),
        name="inproj",
    )(x_lat, x_ctx, x_lat, x_lat, mod, norm_g, wv, wg, wz, wd, wb, wq, wk, wvz, wm,
      conv_w, conv_b, ln_g, ln_b, cos, sin_a, sin_b, gq, gk)


def _segment_edges(i, n_lat, n_tot):
    has_prev = jnp.logical_and(i != 0, i != n_lat)
    has_next = jnp.logical_and(i != n_lat - 1, i != n_tot - 1)
    return has_prev, has_next


def _halo_specs(rows, width, col, n_halo_blocks):
    per = rows // HALO
    prev = pl.BlockSpec((1, HALO, width), lambda b, i: (b, jnp.maximum(i * per - 1, 0), col))
    nxt = pl.BlockSpec((1, HALO, width), lambda b, i: (b, jnp.minimum((i + 1) * per, n_halo_blocks - 1), col))
    return prev, nxt


def _split3(x):
    hi = x.astype(BF16)
    rest = x - hi.astype(F32)
    mid = rest.astype(BF16)
    lo = (rest - mid.astype(F32)).astype(BF16)
    return hi, mid, lo


def _tri_inverses(nmats, eye, blk16, blk32):
    b = lambda a: a.astype(BF16)
    nds = [jnp.where(blk16, n, 0.0) for n in nmats]
    ts = [eye - nd for nd in nds]
    pws = nds
    for _ in range(3):
        pwbs = [b(p) for p in pws]
        pws = [_dot(p, p) for p in pwbs]
        tbs = [b(t) for t in ts]
        pwbs = [b(p) for p in pws]
        prods = [_dot(t, p) for t, p in zip(tbs, pwbs)]
        ts = [t + pr for t, pr in zip(ts, prods)]
    level32 = jnp.logical_and(blk32, jnp.logical_not(blk16))
    level64 = jnp.logical_not(blk32)
    for level in (level32, level64):
        cs = [b(jnp.where(level, n, 0.0)) for n in nmats]
        tbs = [b(t) for t in ts]
        mids = [_dot(c, t) for c, t in zip(cs, tbs)]
        mids = [b(m) for m in mids]
        prods = [_dot(t, m) for t, m in zip(tbs, mids)]
        ts = [t - pr for t, pr in zip(ts, prods)]
    return ts


def _dna_kernel(q, k, v, pq, pk, pv, nq, nk, nv, bd, bdt, cw, alog_r, bias_r, alog_c, bias_c,
                qg_o, kg_o, u_o, w_o, in_o, sd_o, act_s, *, n_lat, n_tot):
    i = pl.program_id(1)
    has_prev, has_next = _segment_edges(i, n_lat, n_tot)
    cc = DN_CHUNK
    rows = q.shape[1]
    bw = q.shape[-1]
    hd = bw // DN_HEADS
    nh = DN_HEADS
    ri = lax.broadcasted_iota(jnp.int32, (rows, rows), 0)
    rj = lax.broadcasted_iota(jnp.int32, (rows, rows), 1)
    shift_dn = (rj == ri - 1).astype(BF16)
    shift_up = (rj == ri + 1).astype(BF16)
    sub = lax.broadcasted_iota(jnp.int32, (SUBLANES, bw), 0)
    for idx, (m_, p_, n_) in enumerate(((q, pq, nq), (k, pk, nk), (v, pv, nv))):
        cols = slice(idx * bw, (idx + 1) * bw)
        x16 = m_[0]
        prev = _dot(shift_dn, x16)
        nxt = _dot(shift_up, x16)
        halo_prev = jnp.where(has_prev, p_[0, HALO - 1:HALO, :].astype(F32), 0.0)
        halo_next = jnp.where(has_next, n_[0, 0:1, :].astype(F32), 0.0)
        prev = jnp.concatenate([jnp.where(sub == 0, halo_prev, prev[0:SUBLANES]), prev[SUBLANES:]], axis=0)
        nxt = jnp.concatenate([nxt[0:rows - SUBLANES],
                               jnp.where(sub == SUBLANES - 1, halo_next, nxt[rows - SUBLANES:])], axis=0)
        conv = x16.astype(F32) * cw[0, 1:2, cols] + prev * cw[0, 0:1, cols] + nxt * cw[0, 2:3, cols]
        act_s[:, cols] = _silu(conv)

    ii = lax.broadcasted_iota(jnp.int32, (cc, cc), 0)
    jj = lax.broadcasted_iota(jnp.int32, (cc, cc), 1)
    low = jj <= ii
    upp = jj >= ii
    low16 = low.astype(BF16)
    upp16 = upp.astype(BF16)
    eye =(ii == jj).astype(F32)
    blk16 = (ii // 16) == (jj // 16)
    blk32 = (ii // 32) == (jj // 32)
    incl = (low, upp)
    strict = (jj < ii, jj > ii)
    last_row = (cc - 1, 0)

    n_ch = rows // cc
    beta_cs, g_cs, g_rs = [], [], []
    for ch in range(n_ch):
        x = bd[0, ch * cc:(ch + 1) * cc, :]
        beta_cs.append(jax.nn.sigmoid(x))
        g_cs.append(-jnp.exp(alog_r[0]) * _softplus(x + bias_r[0]))
        xt = bdt[0, ch]
        g_rs.append(-jnp.exp(alog_c[0]) * _softplus(xt + bias_c[0]))
    parts_c = _split3(jnp.concatenate(g_cs, axis=1))
    parts_r = _split3(jnp.concatenate(g_rs, axis=0))
    gf_c = sum(_dot(low16, p) for p in parts_c)
    gb_c = sum(_dot(upp16, p) for p in parts_c)
    gf_r = sum(_dot(p, upp16) for p in parts_r)
    gb_r = sum(_dot(p, low16) for p in parts_r)
    nbd = 4 * nh
    gcum_cs = [(gf_c[:, ch * LANES:(ch + 1) * LANES], gb_c[:, ch * LANES:(ch + 1) * LANES]) for ch in range(n_ch)]
    gcum_rs = [(gf_r[ch * nbd:(ch + 1) * nbd, :], gb_r[ch * nbd:(ch + 1) * nbd, :]) for ch in range(n_ch)]
    heads = []
    for ch in range(n_ch):
        act = act_s[ch * cc:(ch + 1) * cc, :]
        for h in range(nh):
            qh = act[:, h * hd:(h + 1) * hd]
            kh = act[:, bw + h * hd:bw + (h + 1) * hd]
            vh = act[:, 2 * bw + h * hd:2 * bw + (h + 1) * hd]
            qh = qh * lax.rsqrt(jnp.sum(qh * qh, axis=-1, keepdims=True) + RMS_EPS) * (hd ** -0.5)
            kh = kh * lax.rsqrt(jnp.sum(kh * kh, axis=-1, keepdims=True) + RMS_EPS)
            heads.append((ch, h, qh, kh, vh))
    k16 = [kh.astype(BF16) for (_, _, _, kh, _) in heads]
    q16 = [qh.astype(BF16) for (_, _, qh, _, _) in heads]
    kks = [_dot_nt(kb, kb) for kb in k16]
    qks = [_dot_nt(qb, kb) for qb, kb in zip(q16, k16)]

    inst = []
    nmats = []
    for (ch, h, _, _, _), kk in zip(heads, kks):
        for d in range(2):
            col = 2 * nh + nh * d + h
            gi = gcum_cs[ch][d][:, col:col + 1]
            gj = gcum_rs[ch][d][col:col + 1, :]
            dm = jnp.where(incl[d], jnp.exp(jnp.where(incl[d], gi - gj, 0.0)), 0.0)
            beta = beta_cs[ch][:, nh * d + h:nh * d + h + 1]
            nmats.append(jnp.where(strict[d], beta * kk * dm, 0.0))
            inst.append((ch, h, d, gi, beta, dm))
    ainvs = _tri_inverses(nmats, eye, blk16, blk32)

    rhss = []
    for (ch, h, d, gi, beta, _), ainv in zip(inst, ainvs):
        _, _, _, kh, vh = heads[ch * nh + h]
        rhss.append(jnp.concatenate([vh * beta, kh * (beta * jnp.exp(gi))], axis=1).astype(BF16))
    a16 = [a.astype(BF16) for a in ainvs]
    sols = [_dot(a, r) for a, r in zip(a16, rhss)]

    sd_rows = [[[] for _ in range(2)] for _ in range(n_ch)]
    for (ch, h, d, gi, _, dm), sol in zip(inst, sols):
        _, _, qh, kh, _ = heads[ch * nh + h]
        rs = slice(ch * cc, (ch + 1) * cc)
        hs = slice(h * hd, (h + 1) * hd)
        glast = gi[last_row[d]:last_row[d] + 1, :]
        u_o[d, 0, rs, hs] = sol[:, 0:hd].astype(u_o.dtype)
        w_o[d, 0, rs, hs] = sol[:, hd:2 * hd].astype(w_o.dtype)
        qg_o[d, 0, rs, hs] = (qh * jnp.exp(gi)).astype(qg_o.dtype)
        kg_o[d, 0, rs, hs] = (kh * jnp.exp(glast - gi)).astype(kg_o.dtype)
        in_o[d, 0, rs, h * cc:(h + 1) * cc] = (qks[ch * nh + h] * dm).astype(in_o.dtype)
        sd_rows[ch][d].append(jnp.broadcast_to(jnp.exp(glast), (1, LANES)))
    for ch in range(n_ch):
        for d in range(2):
            sd_o[d, 0, ch] = jnp.concatenate(sd_rows[ch][d] + [jnp.zeros((8 - nh, LANES), F32)], axis=0)


def _deltanet_stage_a(pd, pbd, pbdt, dn_conv_w, alog_r, bias_r, alog_c, bias_c, layer, n_lat, n_tot):
    bsz, t, _ = pd.shape
    bw = dn_conv_w.shape[-1] // 3
    cc = DN_CHUNK
    rows = DNA_CHUNKS * cc
    main = lambda col: pl.BlockSpec((1, rows, bw), lambda b, i: (b, i, col))
    halos = [_halo_specs(rows, bw, col, t // HALO) for col in range(3)]
    small = lambda a: pl.BlockSpec((1,) + a.shape[1:], lambda b, i: (layer,) + (0,) * (a.ndim - 1))
    tok = lambda n, dt: jax.ShapeDtypeStruct((2, bsz, t, n), dt)
    tok_spec = lambda n: pl.BlockSpec((2, 1, rows, n), lambda b, i: (0, b, i, 0))
    return pl.pallas_call(
        functools.partial(_dna_kernel, n_lat=n_lat, n_tot=n_tot),
        grid=(bsz, n_tot),
        in_specs=[main(0), main(1), main(2),
                  halos[0][0], halos[1][0], halos[2][0], halos[0][1], halos[1][1], halos[2][1],
                  pl.BlockSpec((1, rows, LANES), lambda b, i: (b, i, 0)),
                  pl.BlockSpec((1, DNA_CHUNKS, 4 * DN_HEADS, cc), lambda b, i: (b, i, 0, 0)),
                  small(dn_conv_w), small(alog_r), small(bias_r), small(alog_c), small(bias_c)],
        out_specs=[tok_spec(bw), tok_spec(bw), tok_spec(bw), tok_spec(bw), tok_spec(DN_HEADS * cc),
                   pl.BlockSpec((2, 1, DNA_CHUNKS, 8, LANES), lambda b, i: (0, b, i, 0, 0))],
        out_shape=[tok(bw, BF16), tok(bw, BF16), tok(bw, BF16), tok(bw, BF16), tok(DN_HEADS * cc, BF16),
                   jax.ShapeDtypeStruct((2, bsz, t // cc, 8, LANES), F32)],
        scratch_shapes=[pltpu.VMEM((rows, 3 * bw), F32)],
        compiler_params=_params("parallel", "parallel"),
        name="deltanet_a",
    )(pd, pd, pd, pd, pd, pd, pd, pd, pd, pbd, pbdt, dn_conv_w, alog_r, bias_r, alog_c, bias_c)


def _dnb_kernel(qg_f, kg_f, u_f, w_f, in_f, sd_f, qg_b, kg_b, u_b, w_b, in_b, sd_b, of_ref, ob_ref, state):
    cc = DN_CHUNK
    hd = state.shape[-1]
    bsz = state.shape[1]

    @pl.when(pl.program_id(0) == 0)
    def _():
        state[...] = jnp.zeros_like(state)

    dirs = ((qg_f, kg_f, u_f, w_f, in_f, sd_f, of_ref), (qg_b, kg_b, u_b, w_b, in_b, sd_b, ob_ref))
    chains = [(d, b, h) for d in range(2) for b in range(bsz) for h in range(DN_HEADS)]
    hs = lambda h: slice(h * hd, (h + 1) * hd)
    n_ch = qg_f.shape[2] // cc
    s32 = [state[d, b, h] for d, b, h in chains]
    for step in range(n_ch):
        chunk = (step, n_ch - 1 - step)
        rs = [slice(c * cc, (c + 1) * cc) for c in chunk]
        s16 = [s.astype(BF16) for s in s32]
        ws = [_dot(dirs[d][3][0, b, rs[d], hs(h)], s) for (d, b, h), s in zip(chains, s16)]
        qs = [_dot(dirs[d][0][0, b, rs[d], hs(h)], s) for (d, b, h), s in zip(chains, s16)]
        vnew = [(dirs[d][2][0, b, rs[d], hs(h)].astype(F32) - x).astype(BF16) for (d, b, h), x in zip(chains, ws)]
        intra = [_dot(dirs[d][4][0, b, rs[d], h * cc:(h + 1) * cc], v) for (d, b, h), v in zip(chains, vnew)]
        upd = [_dot_tn(dirs[d][1][0, b, rs[d], hs(h)], v) for (d, b, h), v in zip(chains, vnew)]
        for (d, b, h), o1, o2 in zip(chains, qs, intra):
            dirs[d][6][b, rs[d], hs(h)] = (o1 + o2).astype(dirs[d][6].dtype)
        s32 = [s * dirs[d][5][0, b, chunk[d], h:h + 1, :] + up for (d, b, h), s, up in zip(chains, s32, upd)]
    for (d, b, h), s in zip(chains, s32):
        state[d, b, h] = s


def _deltanet_stage_b(qg, kg, u, w, intra, sd, n_lat, n_tot):
    _, bsz, t, bw = qg.shape
    rows = DNB_CHUNKS * DN_CHUNK
    hd = bw // DN_HEADS
    n_ctx = n_tot - n_lat
    block_f = lambda s: jnp.where(s < n_ctx, n_lat + s, s - n_ctx)
    block_b = lambda s: n_tot - 1 - s

    def specs(d, block):
        tok = lambda n: pl.BlockSpec((1, bsz, rows, n), lambda s: (d, 0, block(s), 0))
        return [tok(bw), tok(bw), tok(bw), tok(bw), tok(DN_HEADS * DN_CHUNK),
                pl.BlockSpec((1, bsz, DNB_CHUNKS, 8, LANES), lambda s: (d, 0, block(s), 0, 0))]

    out = lambda block: pl.BlockSpec((bsz, rows, bw), lambda s: (0, block(s), 0))
    args = (qg, kg, u, w, intra, sd)
    return pl.pallas_call(
        _dnb_kernel,
        grid=(n_tot,),
        in_specs=specs(0, block_f) + specs(1, block_b),
        out_specs=[out(block_f), out(block_b)],
        out_shape=[jax.ShapeDtypeStruct((bsz, t, bw), BF16)] * 2,
        scratch_shapes=[pltpu.VMEM((2, bsz, DN_HEADS, hd, hd), F32)],
        compiler_params=_params("arbitrary"),
        name="deltanet_b",
    )(*args, *args)


def _attn_kernel(q_ref, k_ref, v_ref, z_ref, lam_ref, g_ref, o_ref, qs, vx, m_s, acc, *, lam_init):
    tq = q_ref.shape[1]
    n_keys = k_ref.shape[1]
    dh = LANES // 2
    n_full, rem = divmod(n_keys, ATT_KEYS)

    @pl.when(pl.program_id(2) == 0)
    def _():
        vx[:, 0:LANES] = v_ref[0]
        vx[:, LANES:2 * LANES] = jnp.ones((n_keys, LANES), vx.dtype)

    q = q_ref[0].astype(F32)
    lane = lax.broadcasted_iota(jnp.int32, q.shape, 1)
    qs[0:tq, :] = jnp.where(lane < dh, q, 0.0).astype(qs.dtype)
    qs[tq:2 * tq, :] = jnp.where(lane >= dh, q, 0.0).astype(qs.dtype)
    m_s[...] = jnp.full_like(m_s, -jnp.inf)
    acc[...] = jnp.zeros_like(acc)

    key_blocks = [(j * ATT_KEYS, ATT_KEYS) for j in range(n_full)] + ([(n_full * ATT_KEYS, rem)] if rem else [])
    groups = [slice(r0, r0 + ATT_ROWS) for r0 in range(0, 2 * tq, ATT_ROWS)]
    tasks = [(k0, size, rows) for k0, size in key_blocks for rows in groups]
    scores = lambda k0, size, rows: _dot_nt(qs[rows, :], k_ref[0, k0:k0 + size, :])
    s_next = scores(*tasks[0])
    for i, (k0, size, rows) in enumerate(tasks):
        s = s_next
        if i + 1 < len(tasks):
            s_next = scores(*tasks[i + 1])
        m_prev = m_s[rows, :]
        m_new = jnp.maximum(m_prev, jnp.max(s, axis=1, keepdims=True))
        alpha = jnp.exp2(m_prev - m_new)
        p = jnp.exp2(s - m_new[:, 0:1])
        pv = _dot(p.astype(BF16), vx[k0:k0 + size, :])
        acc[rows, 0:LANES] = alpha * acc[rows, 0:LANES] + pv[:, 0:LANES]
        acc[rows, LANES:2 * LANES] = alpha * acc[rows, LANES:2 * LANES] + pv[:, LANES:2 * LANES]
        m_s[rows, :] = m_new

    lm = lam_ref[0]
    lam = (jnp.exp(jnp.sum(lm[0:1] * lm[1:2], axis=1, keepdims=True))
           - jnp.exp(jnp.sum(lm[2:3] * lm[3:4], axis=1, keepdims=True)) + lam_init)
    on = acc[:, 0:LANES] / acc[:, LANES:2 * LANES]
    o = on[0:tq] - lam * on[tq:2 * tq]
    y = o * lax.rsqrt(jnp.mean(o * o, axis=-1, keepdims=True) + RMS_EPS) * g_ref[0] * (1.0 - lam_init)
    o_ref[0] = (y * _silu(z_ref[0].astype(F32))).astype(o_ref.dtype)


def _diff_attention(qn, kn, pa, da_lambda, subln_g, layer, lam_init, *, tq, q_rows, q_off, k_rows, k_off):
    bsz = qn.shape[0]
    v_col = 0
    z_col = DA_HEADS
    qb, kb = q_off // tq, k_off // k_rows
    return pl.pallas_call(
        functools.partial(_attn_kernel, lam_init=lam_init),
        grid=(bsz, DA_HEADS, q_rows // tq),
        in_specs=[pl.BlockSpec((1, tq, LANES), lambda b, h, i: (b, i + qb, h)),
                  pl.BlockSpec((1, k_rows, LANES), lambda b, h, i: (b, kb, h)),
                  pl.BlockSpec((1, k_rows, LANES), lambda b, h, i: (b, kb, v_col + h)),
                  pl.BlockSpec((1, tq, LANES), lambda b, h, i: (b, i + qb, z_col + h)),
                  pl.BlockSpec((1,) + da_lambda.shape[1:], lambda b, h, i: (layer, 0, 0)),
                  pl.BlockSpec((1, 1, LANES), lambda b, h, i: (layer, 0, 0))],
        out_specs=pl.BlockSpec((1, tq, LANES), lambda b, h, i: (b, i, h)),
        out_shape=jax.ShapeDtypeStruct((bsz, q_rows, DA_HEADS * LANES), BF16),
        scratch_shapes=[pltpu.VMEM((2 * tq, LANES), BF16),
                        pltpu.VMEM((k_rows, 2 * LANES), BF16),
                        pltpu.VMEM((2 * tq, LANES), F32),
                        pltpu.VMEM((2 * tq, 2 * LANES), F32)],
        compiler_params=_params("parallel", "parallel", "arbitrary"),
        name="diff_attn",
    )(qn, kn, pa, pa, da_lambda, subln_g)


def _merge_kernel(x_ref, xc_ref, yc, of, ob, dz, ydl, ydc, mg, mod_ref, dng, wb, wo, o_ref, *,
                  n_lat_blocks, ctx_row):
    d = x_ref.shape[-1]
    b = pl.program_id(0)
    i = pl.program_id(1)
    is_ctx = i >= n_lat_blocks
    row = jnp.where(is_ctx, ctx_row, b)
    gate = mod_ref[pl.ds(row, 1), 2 * d:3 * d]
    o = of[0].astype(F32) + ob[0].astype(F32)
    hd = dng.shape[-1]
    parts = []
    for h in range(DN_HEADS):
        oh = o[:, h * hd:(h + 1) * hd]
        parts.append(oh * lax.rsqrt(jnp.mean(oh * oh, axis=-1, keepdims=True) + RMS_EPS) * dng[0])
    ydn = (jnp.concatenate(parts, axis=1) * _silu(dz[0].astype(F32))).astype(BF16)
    yda = jnp.where(is_ctx, ydc[0], ydl[0])
    merged = (jax.nn.sigmoid(mg[0, :, 0:d].astype(F32)) * _dot(yc[0], wb[0, 0])
              + jax.nn.sigmoid(mg[0, :, d:2 * d].astype(F32)) * _dot(ydn, wb[0, 1])
              + jax.nn.sigmoid(mg[0, :, 2 * d:3 * d].astype(F32)) * _dot(yda, wb[0, 2]))
    x = jnp.where(is_ctx, xc_ref[0], x_ref[0])
    o_ref[0] = x + gate * _dot(merged.astype(BF16), wo[0])


def _merge(tokens, y_conv, o_fwd, o_bwd, pd, yd_lat, yd_ctx, pm, mod, dn_norm_g, w_branch, w_out, layer,
           n_lat_blocks, n_blocks, ctx_row):
    x_lat, x_ctx, ctx_block0 = tokens
    bsz, _, d = x_lat.shape
    tm = TM_PROJ
    bw = y_conv.shape[-1]
    n_ctx_blocks = yd_ctx.shape[1] // tm
    tok = lambda n, col=0: pl.BlockSpec((1, tm, n), lambda b, i: (b, i, col))
    lat_spec, ctx_spec = _token_specs(tm, d, n_lat_blocks, ctx_block0, n_ctx_blocks)
    return pl.pallas_call(
        functools.partial(_merge_kernel, n_lat_blocks=n_lat_blocks, ctx_row=ctx_row),
        grid=(bsz, n_blocks),
        in_specs=[lat_spec, ctx_spec, tok(bw), tok(bw), tok(bw), tok(bw, 3),
                  pl.BlockSpec((1, tm, bw), lambda b, i: (b, jnp.minimum(i, n_lat_blocks - 1), 0)),
                  pl.BlockSpec((1, tm, bw),
                               lambda b, i: (b, jnp.clip(i - n_lat_blocks, 0, n_ctx_blocks - 1), 0)),
                  tok(N_BRANCH * d),
                  pl.BlockSpec(mod.shape, lambda b, i: (0, 0)),
                  pl.BlockSpec((1, 1, dn_norm_g.shape[-1]), lambda b, i: (layer, 0, 0)),
                  pl.BlockSpec((1,) + w_branch.shape[1:], lambda b, i: (layer, 0, 0, 0)),
                  pl.BlockSpec((1,) + w_out.shape[1:], lambda b, i: (layer, 0, 0))],
        out_specs=tok(d),
        out_shape=jax.ShapeDtypeStruct((bsz, n_blocks * tm, d), F32),
        compiler_params=_params("parallel", "parallel"),
        name="merge",
    )(x_lat, x_ctx, y_conv, o_fwd, o_bwd, pd, yd_lat, yd_ctx, pm, mod, dn_norm_g, w_branch, w_out)


def _rope_tables(seq, ctx_len, dh):
    n_freq = dh // 4
    inv_freq = ROPE_BASE ** (-jnp.arange(n_freq, dtype=F32) / n_freq)
    n_rows = seq // GRID_W
    row_ang = jnp.arange(n_rows, dtype=F32)[:, None] * inv_freq
    col_ang = jnp.arange(GRID_W, dtype=F32)[:, None] * inv_freq
    cos_r, sin_r, cos_c, sin_c = lax.optimization_barrier(
        (jnp.cos(row_ang), jnp.sin(row_ang), jnp.cos(col_ang), jnp.sin(col_ang)))

    def table(by_row, by_col):
        r = jnp.broadcast_to(by_row[:, None, :], (n_rows, GRID_W, n_freq)).reshape(seq, n_freq)
        c = jnp.broadcast_to(by_col[None, :, :], (n_rows, GRID_W, n_freq)).reshape(seq, n_freq)
        return jnp.tile(jnp.concatenate([r, c], axis=-1), (1, 2 * LANES // dh))

    cos = table(cos_r, cos_c)
    sin = table(sin_r, sin_c)
    first_half = (jnp.arange(LANES) % dh) < dh // 2
    sin_a = jnp.where(first_half, -sin, 0.0)
    sin_b = jnp.where(first_half, 0.0, sin)
    pad = lambda tbl, fill: jnp.concatenate([tbl, jnp.full((ctx_len, LANES), fill, F32)], axis=0)
    return pad(cos, 1.0), pad(sin_a, 0.0), pad(sin_b, 0.0)


def kernel(x, c, ctx, c_ctx, w_ada, b_ada, norm_g, w_in, conv_w, conv_b, conv_ln_g, conv_ln_b, dn_conv_w,
           dn_a_log, dn_dt_bias, dn_norm_g, da_q_norm_g, da_k_norm_g, da_lambda, da_subln_g, w_branch, w_out):
    bsz, seq, d = x.shape
    ctx_len = ctx.shape[1]
    depth = w_in.shape[0]
    bw = d // 2
    dh = bw // (2 * DA_HEADS)
    t = seq + ctx_len
    dna_rows = DNA_CHUNKS * DN_CHUNK
    dnb_rows = DNB_CHUNKS * DN_CHUNK
    assert seq % dnb_rows == 0 and ctx_len % dnb_rows == 0
    assert 2 * dh == LANES and bw // DN_HEADS == LANES and bsz + 1 <= 8
    assert seq % TQ == 0 and seq % ctx_len == 0 and seq % TM_PROJ == 0 and ctx_len % TM_PROJ == 0
    assert seq % dna_rows == 0 and ctx_len % dna_rows == 0 and (2 * TQ) % ATT_ROWS == 0
    n_lat_blocks, n_blocks = seq // TM_PROJ, t // TM_PROJ
    ctx_row = bsz

    tokens = (x, ctx, 0)
    cvec =jnp.concatenate([c, c_ctx[None, :], jnp.zeros((8 - bsz - 1, d), F32)], axis=0)
    cos, sin_a, sin_b = _rope_tables(seq, ctx_len, dh)

    e_conv, e_dn = 3 * bw, 7 * bw
    e_bd = e_dn + 4 * DN_HEADS
    e_da = e_bd + 4 * bw
    nbd = 4 * DN_HEADS
    row3 = lambda a: a.reshape(depth, 1, a.shape[-1])
    gate_row = lambda a: jnp.pad(a.reshape(depth, 1, 2 * DN_HEADS), ((0, 0), (0, 0), (2 * DN_HEADS, LANES - nbd)))
    gate_col = lambda a: jnp.pad(a.reshape(depth, 2 * DN_HEADS, 1), ((0, 0), (2 * DN_HEADS, 0), (0, 0)))
    alog_r, bias_r = gate_row(dn_a_log), gate_row(dn_dt_bias)
    alog_c, bias_c = gate_col(dn_a_log), gate_col(dn_dt_bias)
    tile2 = lambda a: row3(jnp.tile(a, (1, 2)))
    gq, gk = tile2(da_q_norm_g), tile2(da_k_norm_g)
    b_ada3, norm_g3 = row3(b_ada), row3(norm_g)
    conv_b3, ln_g3, ln_b3 = row3(conv_b), row3(conv_ln_g), row3(conv_ln_b)
    dng3, subln3 = row3(dn_norm_g), row3(da_subln_g)
    w_branch16, w_out16 = w_branch.astype(BF16), w_out.astype(BF16)

    for layer in range(depth):
        last = layer == depth - 1
        lam_init = 0.8 - 0.6 * math.exp(-0.3 * layer)
        wl = w_in[layer]
        cols = lambda lo, hi: wl[:, lo:hi].astype(BF16)
        weights = (cols(0, bw), cols(bw, 2 * bw), cols(2 * bw, e_conv),
                   cols(e_conv, e_dn),
                   jnp.pad(wl[:, e_dn:e_bd], ((0, 0), (0, LANES - nbd))).astype(BF16),
                   cols(e_bd, e_bd + bw), cols(e_bd + bw, e_bd + 2 * bw),
                   cols(e_bd + 2 * bw, e_da), cols(e_da, wl.shape[1]))
        mod = _adaln(cvec, w_ada, b_ada3, layer)
        y_conv, pd, pbd, qn, kn, pa, pm, pbdt = _inproj(tokens, t, mod, norm_g3, weights, conv_w, conv_b3, ln_g3, ln_b3,
                                                  cos, sin_a, sin_b, gq, gk, layer, n_lat_blocks, ctx_row)

        out_blocks = n_lat_blocks if last else n_blocks

        qg, kg, u, w, intra, sd = _deltanet_stage_a(pd, pbd, pbdt, dn_conv_w, alog_r, bias_r, alog_c, bias_c,
                                                    layer, seq // dna_rows, t // dna_rows)
        o_fwd, o_bwd = _deltanet_stage_b(qg, kg, u, w, intra, sd, seq // dnb_rows, t // dnb_rows)

        yd_lat = _diff_attention(qn, kn, pa, da_lambda, subln3, layer, lam_init,
                                 tq=TQ, q_rows=seq, q_off=0, k_rows=t, k_off=0)
        yd_ctx = _diff_attention(qn, kn, pa, da_lambda, subln3, layer, lam_init,
                                 tq=ctx_len, q_rows=ctx_len, q_off=seq, k_rows=ctx_len, k_off=seq)

        xs = _merge(tokens, y_conv, o_fwd, o_bwd, pd, yd_lat, yd_ctx, pm, mod, dng3, w_branch16, w_out16, layer,
                    n_lat_blocks, out_blocks, ctx_row)
        tokens = (xs, xs, n_lat_blocks)
    return xs
```

```python
import functools
import math

import jax
import jax.numpy as jnp
from jax import lax
from jax.experimental import pallas as pl
from jax.experimental.pallas import tpu as pltpu

F32 = jnp.float32
BF16 = jnp.bfloat16
HIGHEST = lax.Precision.HIGHEST

GRID_W = 64
N_BRANCH = 3
CONV_K = 31
DN_HEADS = 4
DN_CHUNK = 64
SHORT_K = 3
DA_HEADS = 4
ROPE_BASE = 10000.0
RMS_EPS = 1e-6
LN_EPS = 1e-5

LANES = 128
SUBLANES = 8
SUBLANES_BF16 = 16
HALO = SUBLANES_BF16
VMEM_LIMIT = 56 * 1024 * 1024

TM_PROJ = 256
PROJ_COLS = 512
CONV_ROWS = 64
DNA_CHUNKS = 4
DNB_CHUNKS = 4
TQ = 512
ATT_KEYS = 4096
ATT_ROWS = 512


def _silu(x):
    return x * jax.nn.sigmoid(x)


def _softplus(x):
    return jnp.maximum(x, 0.0) + jnp.log(1.0 + jnp.exp(-jnp.abs(x)))


def _dot(a, b):
    return jnp.dot(a, b, preferred_element_type=F32)


def _dot_hi(a, b):
    return jnp.dot(a, b, preferred_element_type=F32, precision=HIGHEST)


def _dot_nt(a, b):
    return lax.dot_general(a, b, (((1,), (1,)), ((), ())), preferred_element_type=F32)


def _dot_tn(a, b):
    return lax.dot_general(a, b, (((0,), (0,)), ((), ())), preferred_element_type=F32)


def _params(*sem):
    return pltpu.CompilerParams(dimension_semantics=sem, vmem_limit_bytes=VMEM_LIMIT)


def _adaln_kernel(c_ref, w_ref, b_ref, o_ref):
    o_ref[...] = _dot_hi(_silu(c_ref[...]), w_ref[0]) + b_ref[0]


def _adaln(cvec, w_ada, b_ada3, layer):
    d = cvec.shape[1]
    return pl.pallas_call(
        _adaln_kernel,
        grid=(3,),
        in_specs=[
            pl.BlockSpec((8, d), lambda j: (0, 0)),
            pl.BlockSpec((1, d, d), lambda j: (layer, 0, j)),
            pl.BlockSpec((1, 1, d), lambda j: (layer, 0, j)),
        ],
        out_specs=pl.BlockSpec((8, d), lambda j: (0, j)),
        out_shape=jax.ShapeDtypeStruct((8, 3 * d), F32),
        compiler_params=_params("parallel"),
        name="adaln",
    )(cvec, w_ada, b_ada3)


def _inproj_kernel(x_ref, xc_ref, xp_ref, xn_ref, mod_ref, g_ref, wv, wg, wz, wd, wb, wq, wk, wvz, wm,
                   cw_ref, cb_ref, lng, lnb, cos, sin_a, sin_b, gq, gk,
                   yc_o, od, ob, oq, ok, ovz, om, obt, hb_s, buf, zc_s, *, n_lat_blocks, n_blocks, ctx_row):
    d = x_ref.shape[-1]
    tm = x_ref.shape[1]
    b = pl.program_id(0)
    i = pl.program_id(1)
    row = jnp.where(i >= n_lat_blocks, ctx_row, b)
    m = mod_ref[pl.ds(row, 1), :]
    shift = m[:, 0:d]
    scale = m[:, d:2 * d]

    def modulated(x):
        ms = jnp.mean(x * x, axis=-1, keepdims=True)
        return ((x * lax.rsqrt(ms + RMS_EPS) * g_ref[0]) * (1.0 + scale) + shift).astype(BF16)

    hb_s[0:HALO, :] = modulated(xp_ref[0])
    hb_s[HALO:HALO + tm, :] = modulated(jnp.where(i >= n_lat_blocks, xc_ref[0], x_ref[0]))
    hb_s[HALO + tm:HALO + tm + HALO, :] = modulated(xn_ref[0])
    hb_ext = lambda: hb_s[...]
    hb = lambda: hb_s[HALO:HALO + tm, :]

    def project(w, o):
        n = w.shape[1]
        for c0 in range(0, n, PROJ_COLS):
            c1 = min(c0 + PROJ_COLS, n)
            o[0, :, c0:c1] = _dot(hb(), w[:, c0:c1]).astype(o.dtype)

    has_prev, has_next = _segment_edges(i, n_lat_blocks, n_blocks)
    a = _dot(hb_ext(), wv[...]) * jax.nn.sigmoid(_dot(hb_ext(), wg[...]))
    zc_s[...] = _dot(hb(), wz[...])
    buf[0, HALO:HALO + tm, :] = a[HALO:HALO + tm]
    buf[0, 0:HALO, :] = jnp.where(has_prev, a[0:HALO], 0.0)
    buf[0, HALO + tm:HALO + tm + HALO, :] = jnp.where(has_next, a[HALO + tm:HALO + tm + HALO], 0.0)
    span = buf.shape[1] - SUBLANES
    for j in range(1, SUBLANES):
        buf[j, 0:span, :] = buf[0, j:j + span, :]

    project(wd, od)

    logits = _dot(hb(), wb[...])
    ob[0] = logits
    logits_t = logits.T
    for c in range(tm // DN_CHUNK):
        obt[0, c] = logits_t[0:4 * DN_HEADS, c * DN_CHUNK:(c + 1) * DN_CHUNK]

    dh = LANES // 2
    first_map = lax.broadcasted_iota(jnp.int32, (tm, LANES), 1) < dh
    cs, sa, sb = cos[...], sin_a[...], sin_b[...]

    def norm_rope(x, gain, scale_):
        sq = x * x
        s0 = jnp.sum(jnp.where(first_map, sq, 0.0), axis=-1, keepdims=True)
        s1 = jnp.sum(jnp.where(first_map, 0.0, sq), axis=-1, keepdims=True)
        ms = jnp.where(first_map, s0, s1) * (1.0 / dh)
        y = x * lax.rsqrt(ms + RMS_EPS) * gain
        y = y * cs + pltpu.roll(y, LANES - dh // 2, 1) * sa + pltpu.roll(y, dh // 2, 1) * sb
        return y * scale_

    q_scale = dh ** -0.5 * math.log2(math.e)
    for h in range(DA_HEADS):
        hs = slice(h * LANES, (h + 1) * LANES)
        oq[0, :, hs] = norm_rope(_dot(hb(), wq[:, hs]), gq[0], q_scale).astype(oq.dtype)
        ok[0, :, hs] = norm_rope(_dot(hb(), wk[:, hs]), gk[0], 1.0).astype(ok.dtype)

    project(wvz, ovz)
    project(wm, om)

    pad = CONV_K // 2
    for r in range(0, tm, CONV_ROWS):
        acc = jnp.zeros((CONV_ROWS, buf.shape[2]), F32) + cb_ref[0]
        for k in range(CONV_K):
            whole, part = divmod(HALO - pad + k, SUBLANES)
            start = r + whole * SUBLANES
            acc = acc + buf[part, start:start + CONV_ROWS, :] * cw_ref[0, k:k + 1, :]
        mu = jnp.mean(acc, axis=-1, keepdims=True)
        xc = acc - mu
        y = xc * lax.rsqrt(jnp.mean(xc * xc, axis=-1, keepdims=True) + LN_EPS)
        y = y * lng[0] + lnb[0]
        yc_o[0, r:r + CONV_ROWS, :] = (_silu(y) * _silu(zc_s[r:r + CONV_ROWS, :])).astype(yc_o.dtype)


def _token_specs(tm, d, n_lat_blocks, ctx_block0, n_ctx_blocks):
    lat = pl.BlockSpec((1, tm, d), lambda b, i: (b, jnp.minimum(i, n_lat_blocks - 1), 0))
    ctx = pl.BlockSpec((1, tm, d),
                       lambda b, i: (b, ctx_block0 + jnp.clip(i - n_lat_blocks, 0, n_ctx_blocks - 1), 0))
    return lat, ctx


def _inproj(tokens, t, mod, norm_g, weights, conv_w, conv_b, ln_g, ln_b, cos, sin_a, sin_b, gq, gk, layer,
            n_lat_blocks, ctx_row):
    x_lat, x_ctx, ctx_block0 = tokens
    bsz, _, d = x_lat.shape
    tm = TM_PROJ
    nblk = t // tm
    wv, wg, wz, wd, wb, wq, wk, wvz, wm = weights
    bw = wv.shape[1]
    resident = lambda w: pl.BlockSpec(w.shape, lambda b, i: (0, 0), pipeline_mode=pl.Buffered(1))
    out = lambda n: pl.BlockSpec((1, tm, n), lambda b, i: (b, i, 0))
    lat_spec, ctx_spec = _token_specs(tm, d, n_lat_blocks, ctx_block0, nblk - n_lat_blocks)
    assert nblk - n_lat_blocks == 1
    xp, xn = _halo_specs(tm, d, 0, n_lat_blocks * tm // HALO)
    vec = lambda n: pl.BlockSpec((1, 1, n), lambda b, i: (layer, 0, 0))
    tab = pl.BlockSpec((tm, LANES), lambda b, i: (i, 0))
    widths = (bw, wd.shape[1], wb.shape[1], wq.shape[1], wk.shape[1], wvz.shape[1], wm.shape[1])
    dtypes = (BF16, BF16, F32, BF16, BF16, BF16, BF16)
    return pl.pallas_call(
        functools.partial(_inproj_kernel, n_lat_blocks=n_lat_blocks, n_blocks=nblk, ctx_row=ctx_row),
        grid=(bsz, nblk),
        in_specs=[
            lat_spec, ctx_spec, xp, xn,
            pl.BlockSpec(mod.shape, lambda b, i: (0, 0)),
            vec(d),
            resident(wv), resident(wg), resident(wz), resident(wd), resident(wb), resident(wq), resident(wk),
            resident(wvz), resident(wm),
            pl.BlockSpec((1, CONV_K, bw), lambda b, i: (layer, 0, 0)), vec(bw), vec(bw), vec(bw),
            tab, tab, tab, vec(LANES), vec(LANES),
        ],
        out_specs=[out(n) for n in widths] + [
            pl.BlockSpec((1, tm // DN_CHUNK, 4 * DN_HEADS, DN_CHUNK), lambda b, i: (b, i, 0, 0))],
        out_shape=[jax.ShapeDtypeStruct((bsz, t, n), dt) for n, dt in zip(widths, dtypes)] + [
            jax.ShapeDtypeStruct((bsz, t // DN_CHUNK, 4 * DN_HEADS, DN_CHUNK), F32)],
        scratch_shapes=[pltpu.VMEM((tm + 2 * HALO, d), BF16),
                        pltpu.VMEM((SUBLANES, tm + 2 * HALO, bw), F32),
                        pltpu.VMEM((tm, bw), F32)],
        compiler_params=_params("parallel", "parallel"),
        name="inproj",
    )(x_lat, x_ctx, x_lat, x_lat, mod, norm_g, wv, wg, wz, wd, wb, wq, wk, wvz, wm,
      conv_w, conv_b, ln_g, ln_b, cos, sin_a, sin_b, gq, gk)


def _segment_edges(i, n_lat, n_tot):
    has_prev = jnp.logical_and(i != 0, i != n_lat)
    has_next = jnp.logical_and(i != n_lat - 1, i != n_tot - 1)
    return has_prev, has_next


def _halo_specs(rows, width, col, n_halo_blocks):
    per = rows // HALO
    prev = pl.BlockSpec((1, HALO, width), lambda b, i: (b, jnp.maximum(i * per - 1, 0), col))
    nxt = pl.BlockSpec((1, HALO, width), lambda b, i: (b, jnp.minimum((i + 1) * per, n_halo_blocks - 1), col))
    return prev, nxt


def _split3(x):
    hi = x.astype(BF16)
    rest = x - hi.astype(F32)
    mid = rest.astype(BF16)
    lo = (rest - mid.astype(F32)).astype(BF16)
    return hi, mid, lo


def _tri_inverses(nmats, eye, ii, jj):
    b16 = lambda a: a.astype(BF16)
    same = lambda size: (ii // size) == (jj // size)
    ts = [eye - jnp.where(same(2), n, 0.0) for n in nmats]
    size = 2
    while size < nmats[0].shape[0]:
        level = jnp.logical_and(same(2 * size), jnp.logical_not(same(size)))
        cs = [b16(jnp.where(level, n, 0.0)) for n in nmats]
        tbs = [b16(t) for t in ts]
        mids = [_dot(c, t) for c, t in zip(cs, tbs)]
        mids = [b16(m) for m in mids]
        prods = [_dot(t, m) for t, m in zip(tbs, mids)]
        ts = [t - pr for t, pr in zip(ts, prods)]
        size *= 2
    return ts


def _dna_kernel(q, k, v, pq, pk, pv, nq, nk, nv, bd, bdt, cw, alog_r, bias_r, alog_c, bias_c,
                qg_o, kg_o, u_o, w_o, in_o, sd_o, act_s, *, n_lat, n_tot):
    i = pl.program_id(1)
    has_prev, has_next = _segment_edges(i, n_lat, n_tot)
    cc = DN_CHUNK
    rows = q.shape[1]
    bw = q.shape[-1]
    hd = bw // DN_HEADS
    nh = DN_HEADS
    ri = lax.broadcasted_iota(jnp.int32, (rows, rows), 0)
    rj = lax.broadcasted_iota(jnp.int32, (rows, rows), 1)
    shift_dn = (rj == ri - 1).astype(BF16)
    shift_up = (rj == ri + 1).astype(BF16)
    sub = lax.broadcasted_iota(jnp.int32, (SUBLANES, bw), 0)
    for idx, (m_, p_, n_) in enumerate(((q, pq, nq), (k, pk, nk), (v, pv, nv))):
        cols = slice(idx * bw, (idx + 1) * bw)
        x16 = m_[0]
        prev = _dot(shift_dn, x16)
        nxt = _dot(shift_up, x16)
        halo_prev = jnp.where(has_prev, p_[0, HALO - 1:HALO, :].astype(F32), 0.0)
        halo_next = jnp.where(has_next, n_[0, 0:1, :].astype(F32), 0.0)
        prev = jnp.concatenate([jnp.where(sub == 0, halo_prev, prev[0:SUBLANES]), prev[SUBLANES:]], axis=0)
        nxt = jnp.concatenate([nxt[0:rows - SUBLANES],
                               jnp.where(sub == SUBLANES - 1, halo_next, nxt[rows - SUBLANES:])], axis=0)
        conv = x16.astype(F32) * cw[0, 1:2, cols] + prev * cw[0, 0:1, cols] + nxt * cw[0, 2:3, cols]
        act_s[:, cols] = _silu(conv)

    ii = lax.broadcasted_iota(jnp.int32, (cc, cc), 0)
    jj = lax.broadcasted_iota(jnp.int32, (cc, cc), 1)
    low = jj <= ii
    upp = jj >= ii
    low16 = low.astype(BF16)
    upp16 = upp.astype(BF16)
    eye = (ii == jj).astype(F32)
    incl = (low, upp)
    strict = (jj < ii, jj > ii)
    last_row = (cc - 1, 0)

    n_ch = rows // cc
    beta_cs, g_cs, g_rs = [], [], []
    for ch in range(n_ch):
        x = bd[0, ch * cc:(ch + 1) * cc, :]
        beta_cs.append(jax.nn.sigmoid(x))
        g_cs.append(-jnp.exp(alog_r[0]) * _softplus(x + bias_r[0]))
        xt = bdt[0, ch]
        g_rs.append(-jnp.exp(alog_c[0]) * _softplus(xt + bias_c[0]))
    parts_c = _split3(jnp.concatenate(g_cs, axis=1))
    parts_r = _split3(jnp.concatenate(g_rs, axis=0))
    gf_c = sum(_dot(low16, p) for p in parts_c)
    gb_c = sum(_dot(upp16, p) for p in parts_c)
    gf_r = sum(_dot(p, upp16) for p in parts_r)
    gb_r = sum(_dot(p, low16) for p in parts_r)
    nbd = 4 * nh
    gcum_cs = [(gf_c[:, ch * LANES:(ch + 1) * LANES], gb_c[:, ch * LANES:(ch + 1) * LANES]) for ch in range(n_ch)]
    gcum_rs = [(gf_r[ch * nbd:(ch + 1) * nbd, :], gb_r[ch * nbd:(ch + 1) * nbd, :]) for ch in range(n_ch)]
    heads = []
    for ch in range(n_ch):
        act = act_s[ch * cc:(ch + 1) * cc, :]
        for h in range(nh):
            qh = act[:, h * hd:(h + 1) * hd]
            kh = act[:, bw + h * hd:bw + (h + 1) * hd]
            vh = act[:, 2 * bw + h * hd:2 * bw + (h + 1) * hd]
            qh = qh * lax.rsqrt(jnp.sum(qh * qh, axis=-1, keepdims=True) + RMS_EPS) * (hd ** -0.5)
            kh = kh * lax.rsqrt(jnp.sum(kh * kh, axis=-1, keepdims=True) + RMS_EPS)
            heads.append((ch, h, qh, kh, vh))
    k16 = [kh.astype(BF16) for (_, _, _, kh, _) in heads]
    q16 = [qh.astype(BF16) for (_, _, qh, _, _) in heads]
    kks = [_dot_nt(kb, kb) for kb in k16]
    qks = [_dot_nt(qb, kb) for qb, kb in zip(q16, k16)]

    inst = []
    nmats = []
    for (ch, h, _, _, _), kk in zip(heads, kks):
        for d in range(2):
            col = 2 * nh + nh * d + h
            gi = gcum_cs[ch][d][:, col:col + 1]
            gj = gcum_rs[ch][d][col:col + 1, :]
            dm = jnp.where(incl[d], jnp.exp(jnp.where(incl[d], gi - gj, 0.0)), 0.0)
            beta = beta_cs[ch][:, nh * d + h:nh * d + h + 1]
            nmats.append(jnp.where(strict[d], beta * kk * dm, 0.0))
            inst.append((ch, h, d, gi, beta, dm))
    ainvs = _tri_inverses(nmats, eye, ii, jj)

    rhss = []
    for (ch, h, d, gi, beta, _), ainv in zip(inst, ainvs):
        _, _, _, kh, vh = heads[ch * nh + h]
        rhss.append(jnp.concatenate([vh * beta, kh * (beta * jnp.exp(gi))], axis=1).astype(BF16))
    a16 = [a.astype(BF16) for a in ainvs]
    sols = [_dot(a, r) for a, r in zip(a16, rhss)]

    sd_rows = [[[] for _ in range(2)] for _ in range(n_ch)]
    for (ch, h, d, gi, _, dm), sol in zip(inst, sols):
        _, _, qh, kh, _ = heads[ch * nh + h]
        rs = slice(ch * cc, (ch + 1) * cc)
        hs = slice(h * hd, (h + 1) * hd)
        glast = gi[last_row[d]:last_row[d] + 1, :]
        u_o[d, 0, rs, hs] = sol[:, 0:hd].astype(u_o.dtype)
        w_o[d, 0, rs, hs] = sol[:, hd:2 * hd].astype(w_o.dtype)
        qg_o[d, 0, rs, hs] = (qh * jnp.exp(gi)).astype(qg_o.dtype)
        kg_o[d, 0, rs, hs] = (kh * jnp.exp(glast - gi)).astype(kg_o.dtype)
        in_o[d, 0, rs, h * cc:(h + 1) * cc] = (qks[ch * nh + h] * dm).astype(in_o.dtype)
        sd_rows[ch][d].append(jnp.broadcast_to(jnp.exp(glast), (1, LANES)))
    for ch in range(n_ch):
        for d in range(2):
            sd_o[d, 0, ch] = jnp.concatenate(sd_rows[ch][d] + [jnp.zeros((8 - nh, LANES), F32)], axis=0)


def _deltanet_stage_a(pd, pbd, pbdt, dn_conv_w, alog_r, bias_r, alog_c, bias_c, layer, n_lat, n_tot):
    bsz, t, _ = pd.shape
    bw = dn_conv_w.shape[-1] // 3
    cc = DN_CHUNK
    rows = DNA_CHUNKS * cc
    main = lambda col: pl.BlockSpec((1, rows, bw), lambda b, i: (b, i, col))
    halos = [_halo_specs(rows, bw, col, t // HALO) for col in range(3)]
    small = lambda a: pl.BlockSpec((1,) + a.shape[1:], lambda b, i: (layer,) + (0,) * (a.ndim - 1))
    tok = lambda n, dt: jax.ShapeDtypeStruct((2, bsz, t, n), dt)
    tok_spec = lambda n: pl.BlockSpec((2, 1, rows, n), lambda b, i: (0, b, i, 0))
    return pl.pallas_call(
        functools.partial(_dna_kernel, n_lat=n_lat, n_tot=n_tot),
        grid=(bsz, n_tot),
        in_specs=[main(0), main(1), main(2),
                  halos[0][0], halos[1][0], halos[2][0], halos[0][1], halos[1][1], halos[2][1],
                  pl.BlockSpec((1, rows, LANES), lambda b, i: (b, i, 0)),
                  pl.BlockSpec((1, DNA_CHUNKS, 4 * DN_HEADS, cc), lambda b, i: (b, i, 0, 0)),
                  small(dn_conv_w), small(alog_r), small(bias_r), small(alog_c), small(bias_c)],
        out_specs=[tok_spec(bw), tok_spec(bw), tok_spec(bw), tok_spec(bw), tok_spec(DN_HEADS * cc),
                   pl.BlockSpec((2, 1, DNA_CHUNKS, 8, LANES), lambda b, i: (0, b, i, 0, 0))],
        out_shape=[tok(bw, BF16), tok(bw, BF16), tok(bw, BF16), tok(bw, BF16), tok(DN_HEADS * cc, BF16),
                   jax.ShapeDtypeStruct((2, bsz, t // cc, 8, LANES), F32)],
        scratch_shapes=[pltpu.VMEM((rows, 3 * bw), F32)],
        compiler_params=_params("parallel", "parallel"),
        name="deltanet_a",
    )(pd, pd, pd, pd, pd, pd, pd, pd, pd, pbd, pbdt, dn_conv_w, alog_r, bias_r, alog_c, bias_c)


def _dnb_kernel(qg_f, kg_f, u_f, w_f, in_f, sd_f, qg_b, kg_b, u_b, w_b, in_b, sd_b, of_ref, ob_ref, state):
    cc = DN_CHUNK
    hd = state.shape[-1]
    bsz = state.shape[1]

    @pl.when(pl.program_id(0) == 0)
    def _():
        state[...] = jnp.zeros_like(state)

    dirs = ((qg_f, kg_f, u_f, w_f, in_f, sd_f, of_ref), (qg_b, kg_b, u_b, w_b, in_b, sd_b, ob_ref))
    chains = [(d, b, h) for d in range(2) for b in range(bsz) for h in range(DN_HEADS)]
    hs = lambda h: slice(h * hd, (h + 1) * hd)
    n_ch = qg_f.shape[2] // cc
    s32 = [state[d, b, h] for d, b, h in chains]
    for step in range(n_ch):
        chunk = (step, n_ch - 1 - step)
        rs = [slice(c * cc, (c + 1) * cc) for c in chunk]
        s16 = [s.astype(BF16) for s in s32]
        ws = [_dot(dirs[d][3][0, b, rs[d], hs(h)], s) for (d, b, h), s in zip(chains, s16)]
        qs = [_dot(dirs[d][0][0, b, rs[d], hs(h)], s) for (d, b, h), s in zip(chains, s16)]
        vnew = [(dirs[d][2][0, b, rs[d], hs(h)].astype(F32) - x).astype(BF16) for (d, b, h), x in zip(chains, ws)]
        intra = [_dot(dirs[d][4][0, b, rs[d], h * cc:(h + 1) * cc], v) for (d, b, h), v in zip(chains, vnew)]
        upd = [_dot_tn(dirs[d][1][0, b, rs[d], hs(h)], v) for (d, b, h), v in zip(chains, vnew)]
        for (d, b, h), o1, o2 in zip(chains, qs, intra):
            dirs[d][6][b, rs[d], hs(h)] = (o1 + o2).astype(dirs[d][6].dtype)
        s32 = [s * dirs[d][5][0, b, chunk[d], h:h + 1, :] + up for (d, b, h), s, up in zip(chains, s32, upd)]
    for (d, b, h), s in zip(chains, s32):
        state[d, b, h] = s


def _deltanet_stage_b(qg, kg, u, w, intra, sd, n_lat, n_tot):
    _, bsz, t, bw = qg.shape
    rows = DNB_CHUNKS * DN_CHUNK
    hd = bw // DN_HEADS
    n_ctx = n_tot - n_lat
    block_f = lambda s: jnp.where(s < n_ctx, n_lat + s, s - n_ctx)
    block_b = lambda s: n_tot - 1 - s

    def specs(d, block):
        tok = lambda n: pl.BlockSpec((1, bsz, rows, n), lambda s: (d, 0, block(s), 0))
        return [tok(bw), tok(bw), tok(bw), tok(bw), tok(DN_HEADS * DN_CHUNK),
                pl.BlockSpec((1, bsz, DNB_CHUNKS, 8, LANES), lambda s: (d, 0, block(s), 0, 0))]

    out = lambda block: pl.BlockSpec((bsz, rows, bw), lambda s: (0, block(s), 0))
    args = (qg, kg, u, w, intra, sd)
    return pl.pallas_call(
        _dnb_kernel,
        grid=(n_tot,),
        in_specs=specs(0, block_f) + specs(1, block_b),
        out_specs=[out(block_f), out(block_b)],
        out_shape=[jax.ShapeDtypeStruct((bsz, t, bw), BF16)] * 2,
        scratch_shapes=[pltpu.VMEM((2, bsz, DN_HEADS, hd, hd), F32)],
        compiler_params=_params("arbitrary"),
        name="deltanet_b",
    )(*args, *args)


def _attn_kernel(q_ref, k_ref, v_ref, z_ref, lam_ref, g_ref, o_ref, qs, vx, m_s, acc, *, lam_init):
    tq = q_ref.shape[1]
    n_keys = k_ref.shape[1]
    dh = LANES // 2
    n_full, rem = divmod(n_keys, ATT_KEYS)

    @pl.when(pl.program_id(2) == 0)
    def _():
        vx[:, 0:LANES] = v_ref[0]
        vx[:, LANES:2 * LANES] = jnp.ones((n_keys, LANES), vx.dtype)

    q = q_ref[0].astype(F32)
    lane = lax.broadcasted_iota(jnp.int32, q.shape, 1)
    qs[0:tq, :] = jnp.where(lane < dh, q, 0.0).astype(qs.dtype)
    qs[tq:2 * tq, :] = jnp.where(lane >= dh, q, 0.0).astype(qs.dtype)
    m_s[...] = jnp.full_like(m_s, -jnp.inf)
    acc[...] = jnp.zeros_like(acc)

    key_blocks = [(j * ATT_KEYS, ATT_KEYS) for j in range(n_full)] + ([(n_full * ATT_KEYS, rem)] if rem else [])
    groups = [slice(r0, r0 + ATT_ROWS) for r0 in range(0, 2 * tq, ATT_ROWS)]
    tasks = [(k0, size, rows) for k0, size in key_blocks for rows in groups]
    scores = lambda k0, size, rows: _dot_nt(qs[rows, :], k_ref[0, k0:k0 + size, :])
    s_next = scores(*tasks[0])
    for i, (k0, size, rows) in enumerate(tasks):
        s = s_next
        if i + 1 < len(tasks):
            s_next = scores(*tasks[i + 1])
        m_prev = m_s[rows, :]
        m_new = jnp.maximum(m_prev, jnp.max(s, axis=1, keepdims=True))
        alpha = jnp.exp2(m_prev - m_new)
        p = jnp.exp2(s - m_new[:, 0:1])
        pv = _dot(p.astype(BF16), vx[k0:k0 + size, :])
        acc[rows, 0:LANES] = alpha * acc[rows, 0:LANES] + pv[:, 0:LANES]
        acc[rows, LANES:2 * LANES] = alpha * acc[rows, LANES:2 * LANES] + pv[:, LANES:2 * LANES]
        m_s[rows, :] = m_new

    lm = lam_ref[0]
    lam = (jnp.exp(jnp.sum(lm[0:1] * lm[1:2], axis=1, keepdims=True))
           - jnp.exp(jnp.sum(lm[2:3] * lm[3:4], axis=1, keepdims=True)) + lam_init)
    on = acc[:, 0:LANES] / acc[:, LANES:2 * LANES]
    o = on[0:tq] - lam * on[tq:2 * tq]
    y = o * lax.rsqrt(jnp.mean(o * o, axis=-1, keepdims=True) + RMS_EPS) * g_ref[0] * (1.0 - lam_init)
    o_ref[0] = (y * _silu(z_ref[0].astype(F32))).astype(o_ref.dtype)


def _diff_attention(qn, kn, pa, da_lambda, subln_g, layer, lam_init, *, tq, q_rows, q_off, k_rows, k_off):
    bsz = qn.shape[0]
    v_col = 0
    z_col = DA_HEADS
    qb, kb = q_off // tq, k_off // k_rows
    return pl.pallas_call(
        functools.partial(_attn_kernel, lam_init=lam_init),
        grid=(bsz, DA_HEADS, q_rows // tq),
        in_specs=[pl.BlockSpec((1, tq, LANES), lambda b, h, i: (b, i + qb, h)),
                  pl.BlockSpec((1, k_rows, LANES), lambda b, h, i: (b, kb, h)),
                  pl.BlockSpec((1, k_rows, LANES), lambda b, h, i: (b, kb, v_col + h)),
                  pl.BlockSpec((1, tq, LANES), lambda b, h, i: (b, i + qb, z_col + h)),
                  pl.BlockSpec((1,) + da_lambda.shape[1:], lambda b, h, i: (layer, 0, 0)),
                  pl.BlockSpec((1, 1, LANES), lambda b, h, i: (layer, 0, 0))],
        out_specs=pl.BlockSpec((1, tq, LANES), lambda b, h, i: (b, i, h)),
        out_shape=jax.ShapeDtypeStruct((bsz, q_rows, DA_HEADS * LANES), BF16),
        scratch_shapes=[pltpu.VMEM((2 * tq, LANES), BF16),
                        pltpu.VMEM((k_rows, 2 * LANES), BF16),
                        pltpu.VMEM((2 * tq, LANES), F32),
                        pltpu.VMEM((2 * tq, 2 * LANES), F32)],
        compiler_params=_params("parallel", "parallel", "arbitrary"),
        name="diff_attn",
    )(qn, kn, pa, pa, da_lambda, subln_g)


def _merge_kernel(x_ref, xc_ref, yc, of, ob, dz, ydl, ydc, mg, mod_ref, dng, wb, wo, o_ref, *,
                  n_lat_blocks, ctx_row):
    d = x_ref.shape[-1]
    b = pl.program_id(0)
    i = pl.program_id(1)
    is_ctx = i >= n_lat_blocks
    row = jnp.where(is_ctx, ctx_row, b)
    gate = mod_ref[pl.ds(row, 1), 2 * d:3 * d]
    o = of[0].astype(F32) + ob[0].astype(F32)
    hd = dng.shape[-1]
    parts = []
    for h in range(DN_HEADS):
        oh = o[:, h * hd:(h + 1) * hd]
        parts.append(oh * lax.rsqrt(jnp.mean(oh * oh, axis=-1, keepdims=True) + RMS_EPS) * dng[0])
    ydn = (jnp.concatenate(parts, axis=1) * _silu(dz[0].astype(F32))).astype(BF16)
    yda = jnp.where(is_ctx, ydc[0], ydl[0])
    merged = (jax.nn.sigmoid(mg[0, :, 0:d].astype(F32)) * _dot(yc[0], wb[0, 0])
              + jax.nn.sigmoid(mg[0, :, d:2 * d].astype(F32)) * _dot(ydn, wb[0, 1])
              + jax.nn.sigmoid(mg[0, :, 2 * d:3 * d].astype(F32)) * _dot(yda, wb[0, 2]))
    x = jnp.where(is_ctx, xc_ref[0], x_ref[0])
    o_ref[0] = x + gate * _dot(merged.astype(BF16), wo[0])


def _merge(tokens, y_conv, o_fwd, o_bwd, pd, yd_lat, yd_ctx, pm, mod, dn_norm_g, w_branch, w_out, layer,
           n_lat_blocks, n_blocks, ctx_row):
    x_lat, x_ctx, ctx_block0 = tokens
    bsz, _, d = x_lat.shape
    tm = TM_PROJ
    bw = y_conv.shape[-1]
    n_ctx_blocks = yd_ctx.shape[1] // tm
    tok = lambda n, col=0: pl.BlockSpec((1, tm, n), lambda b, i: (b, i, col))
    lat_spec, ctx_spec = _token_specs(tm, d, n_lat_blocks, ctx_block0, n_ctx_blocks)
    return pl.pallas_call(
        functools.partial(_merge_kernel, n_lat_blocks=n_lat_blocks, ctx_row=ctx_row),
        grid=(bsz, n_blocks),
        in_specs=[lat_spec, ctx_spec, tok(bw), tok(bw), tok(bw), tok(bw, 3),
                  pl.BlockSpec((1, tm, bw), lambda b, i: (b, jnp.minimum(i, n_lat_blocks - 1), 0)),
                  pl.BlockSpec((1, tm, bw),
                               lambda b, i: (b, jnp.clip(i - n_lat_blocks, 0, n_ctx_blocks - 1), 0)),
                  tok(N_BRANCH * d),
                  pl.BlockSpec(mod.shape, lambda b, i: (0, 0)),
                  pl.BlockSpec((1, 1, dn_norm_g.shape[-1]), lambda b, i: (layer, 0, 0)),
                  pl.BlockSpec((1,) + w_branch.shape[1:], lambda b, i: (layer, 0, 0, 0)),
                  pl.BlockSpec((1,) + w_out.shape[1:], lambda b, i: (layer, 0, 0))],
        out_specs=tok(d),
        out_shape=jax.ShapeDtypeStruct((bsz, n_blocks * tm, d), F32),
        compiler_params=_params("parallel", "parallel"),
        name="merge",
    )(x_lat, x_ctx, y_conv, o_fwd, o_bwd, pd, yd_lat, yd_ctx, pm, mod, dn_norm_g, w_branch, w_out)


def _rope_tables(seq, ctx_len, dh):
    n_freq = dh // 4
    inv_freq = ROPE_BASE ** (-jnp.arange(n_freq, dtype=F32) / n_freq)
    n_rows = seq // GRID_W
    row_ang = jnp.arange(n_rows, dtype=F32)[:, None] * inv_freq
    col_ang = jnp.arange(GRID_W, dtype=F32)[:, None] * inv_freq
    cos_r, sin_r, cos_c, sin_c = lax.optimization_barrier(
        (jnp.cos(row_ang), jnp.sin(row_ang), jnp.cos(col_ang), jnp.sin(col_ang)))

    def table(by_row, by_col):
        r = jnp.broadcast_to(by_row[:, None, :], (n_rows, GRID_W, n_freq)).reshape(seq, n_freq)
        c = jnp.broadcast_to(by_col[None, :, :], (n_rows, GRID_W, n_freq)).reshape(seq, n_freq)
        return jnp.tile(jnp.concatenate([r, c], axis=-1), (1, 2 * LANES // dh))

    cos = table(cos_r, cos_c)
    sin = table(sin_r, sin_c)
    first_half = (jnp.arange(LANES) % dh) < dh // 2
    sin_a = jnp.where(first_half, -sin, 0.0)
    sin_b = jnp.where(first_half, 0.0, sin)
    pad = lambda tbl, fill: jnp.concatenate([tbl, jnp.full((ctx_len, LANES), fill, F32)], axis=0)
    return pad(cos, 1.0), pad(sin_a, 0.0), pad(sin_b, 0.0)


def kernel(x, c, ctx, c_ctx, w_ada, b_ada, norm_g, w_in, conv_w, conv_b, conv_ln_g, conv_ln_b, dn_conv_w,
           dn_a_log, dn_dt_bias, dn_norm_g, da_q_norm_g, da_k_norm_g, da_lambda, da_subln_g, w_branch, w_out):
    bsz, seq, d = x.shape
    ctx_len = ctx.shape[1]
    depth = w_in.shape[0]
    bw = d // 2
    dh = bw // (2 * DA_HEADS)
    t = seq + ctx_len
    dna_rows = DNA_CHUNKS * DN_CHUNK
    dnb_rows = DNB_CHUNKS * DN_CHUNK
    assert seq % dnb_rows == 0 and ctx_len % dnb_rows == 0
    assert 2 * dh == LANES and bw // DN_HEADS == LANES and bsz + 1 <= 8
    assert seq % TQ == 0 and seq % ctx_len == 0 and seq % TM_PROJ == 0 and ctx_len % TM_PROJ == 0
    assert seq % dna_rows == 0 and ctx_len % dna_rows == 0 and (2 * TQ) % ATT_ROWS == 0
    n_lat_blocks, n_blocks = seq // TM_PROJ, t // TM_PROJ
    ctx_row = bsz

    tokens = (x, ctx, 0)
    cvec =jnp.concatenate([c, c_ctx[None, :], jnp.zeros((8 - bsz - 1, d), F32)], axis=0)
    cos, sin_a, sin_b = _rope_tables(seq, ctx_len, dh)

    e_conv, e_dn = 3 * bw, 7 * bw
    e_bd = e_dn + 4 * DN_HEADS
    e_da = e_bd + 4 * bw
    nbd = 4 * DN_HEADS
    row3 = lambda a: a.reshape(depth, 1, a.shape[-1])
    gate_row = lambda a: jnp.pad(a.reshape(depth, 1, 2 * DN_HEADS), ((0, 0), (0, 0), (2 * DN_HEADS, LANES - nbd)))
    gate_col = lambda a: jnp.pad(a.reshape(depth, 2 * DN_HEADS, 1), ((0, 0), (2 * DN_HEADS, 0), (0, 0)))
    alog_r, bias_r = gate_row(dn_a_log), gate_row(dn_dt_bias)
    alog_c, bias_c = gate_col(dn_a_log), gate_col(dn_dt_bias)
    tile2 = lambda a: row3(jnp.tile(a, (1, 2)))
    gq, gk = tile2(da_q_norm_g), tile2(da_k_norm_g)
    b_ada3, norm_g3 = row3(b_ada), row3(norm_g)
    conv_b3, ln_g3, ln_b3 = row3(conv_b), row3(conv_ln_g), row3(conv_ln_b)
    dng3, subln3 = row3(dn_norm_g), row3(da_subln_g)
    w_branch16, w_out16 = w_branch.astype(BF16), w_out.astype(BF16)

    for layer in range(depth):
        last = layer == depth - 1
        lam_init = 0.8 - 0.6 * math.exp(-0.3 * layer)
        wl = w_in[layer]
        cols = lambda lo, hi: wl[:, lo:hi].astype(BF16)
        weights = (cols(0, bw), cols(bw, 2 * bw), cols(2 * bw, e_conv),
                   cols(e_conv, e_dn),
                   jnp.pad(wl[:, e_dn:e_bd], ((0, 0), (0, LANES - nbd))).astype(BF16),
                   cols(e_bd, e_bd + bw), cols(e_bd + bw, e_bd + 2 * bw),
                   cols(e_bd + 2 * bw, e_da), cols(e_da, wl.shape[1]))
        mod = _adaln(cvec, w_ada, b_ada3, layer)
        y_conv, pd, pbd, qn, kn, pa, pm, pbdt = _inproj(tokens, t, mod, norm_g3, weights, conv_w, conv_b3, ln_g3, ln_b3,
                                                  cos, sin_a, sin_b, gq, gk, layer, n_lat_blocks, ctx_row)

        out_blocks = n_lat_blocks if last else n_blocks

        qg, kg, u, w, intra, sd = _deltanet_stage_a(pd, pbd, pbdt, dn_conv_w, alog_r, bias_r, alog_c, bias_c,
                                                    layer, seq // dna_rows, t // dna_rows)
        o_fwd, o_bwd = _deltanet_stage_b(qg, kg, u, w, intra, sd, seq // dnb_rows, t // dnb_rows)

        yd_lat = _diff_attention(qn, kn, pa, da_lambda, subln3, layer, lam_init,
                                 tq=TQ, q_rows=seq, q_off=0, k_rows=t, k_off=0)
        yd_ctx = _diff_attention(qn, kn, pa, da_lambda, subln3, layer, lam_init,
                                 tq=ctx_len, q_rows=ctx_len, q_off=seq, k_rows=ctx_len, k_off=seq)

        xs = _merge(tokens, y_conv, o_fwd, o_bwd, pd, yd_lat, yd_ctx, pm, mod, dng3, w_branch16, w_out16, layer,
                    n_lat_blocks, out_blocks, ctx_row)
        tokens = (xs, xs, n_lat_blocks)
    return xs
```

```python
import functools
import math

import jax
import jax.numpy as jnp
from jax import lax
from jax.experimental import pallas as pl
from jax.experimental.pallas import tpu as pltpu

F32 = jnp.float32
BF16 = jnp.bfloat16
HIGHEST = lax.Precision.HIGHEST

GRID_W = 64
N_BRANCH = 3
CONV_K = 31
DN_HEADS = 4
DN_CHUNK = 64
SHORT_K = 3
DA_HEADS = 4
ROPE_BASE = 10000.0
RMS_EPS = 1e-6
LN_EPS = 1e-5

LANES = 128
SUBLANES = 8
SUBLANES_BF16 = 16
HALO = SUBLANES_BF16
VMEM_LIMIT = 56 * 1024 * 1024

TM_PROJ = 256
PROJ_COLS = 512
CONV_ROWS = 64
DNA_CHUNKS = 4
DNB_CHUNKS = 4
TQ = 512
ATT_KEYS = 4096
ATT_ROWS = 512


def _silu(x):
    return x * jax.nn.sigmoid(x)


def _softplus(x):
    return jnp.maximum(x, 0.0) + jnp.log(1.0 + jnp.exp(-jnp.abs(x)))


def _dot(a, b):
    return jnp.dot(a, b, preferred_element_type=F32)


def _dot_hi(a, b):
    return jnp.dot(a, b, preferred_element_type=F32, precision=HIGHEST)


def _dot_nt(a, b):
    return lax.dot_general(a, b, (((1,), (1,)), ((), ())), preferred_element_type=F32)


def _dot_tn(a, b):
    return lax.dot_general(a, b, (((0,), (0,)), ((), ())), preferred_element_type=F32)


def _params(*sem):
    return pltpu.CompilerParams(dimension_semantics=sem, vmem_limit_bytes=VMEM_LIMIT)


def _adaln_kernel(c_ref, w_ref, b_ref, o_ref):
    o_ref[...] = _dot_hi(_silu(c_ref[...]), w_ref[0]) + b_ref[0]


def _adaln(cvec, w_ada, b_ada3, layer):
    d = cvec.shape[1]
    return pl.pallas_call(
        _adaln_kernel,
        grid=(3,),
        in_specs=[
            pl.BlockSpec((8, d), lambda j: (0, 0)),
            pl.BlockSpec((1, d, d), lambda j: (layer, 0, j)),
            pl.BlockSpec((1, 1, d), lambda j: (layer, 0, j)),
        ],
        out_specs=pl.BlockSpec((8, d), lambda j: (0, j)),
        out_shape=jax.ShapeDtypeStruct((8, 3 * d), F32),
        compiler_params=_params("parallel"),
        name="adaln",
    )(cvec, w_ada, b_ada3)


def _inproj_kernel(x_ref, xc_ref, xp_ref, xn_ref, mod_ref, g_ref, wv, wg, wz, wd, wb, wq, wk, wvz, wm,
                   cw_ref, cb_ref, lng, lnb, cos, sin_a, sin_b, gq, gk,
                   yc_o, od, ob, oq, ok, ovz, om, obt, hb_s, buf, zc_s, *, n_lat_blocks, n_blocks, ctx_row):
    d = x_ref.shape[-1]
    tm = x_ref.shape[1]
    b = pl.program_id(0)
    i = pl.program_id(1)
    row = jnp.where(i >= n_lat_blocks, ctx_row, b)
    m = mod_ref[pl.ds(row, 1), :]
    shift = m[:, 0:d]
    scale = m[:, d:2 * d]

    def modulated(x):
        ms = jnp.mean(x * x, axis=-1, keepdims=True)
        return ((x * lax.rsqrt(ms + RMS_EPS) * g_ref[0]) * (1.0 + scale) + shift).astype(BF16)

    hb_s[0:HALO, :] = modulated(xp_ref[0])
    hb_s[HALO:HALO + tm, :] = modulated(jnp.where(i >= n_lat_blocks, xc_ref[0], x_ref[0]))
    hb_s[HALO + tm:HALO + tm + HALO, :] = modulated(xn_ref[0])
    hb_ext = lambda: hb_s[...]
    hb = lambda: hb_s[HALO:HALO + tm, :]

    def project(w, o):
        n = w.shape[1]
        for c0 in range(0, n, PROJ_COLS):
            c1 = min(c0 + PROJ_COLS, n)
            o[0, :, c0:c1] = _dot(hb(), w[:, c0:c1]).astype(o.dtype)

    has_prev, has_next = _segment_edges(i, n_lat_blocks, n_blocks)
    a = _dot(hb_ext(), wv[...]) * jax.nn.sigmoid(_dot(hb_ext(), wg[...]))
    zc_s[...] = _dot(hb(), wz[...])
    buf[0, HALO:HALO + tm, :] = a[HALO:HALO + tm]
    buf[0, 0:HALO, :] = jnp.where(has_prev, a[0:HALO], 0.0)
    buf[0, HALO + tm:HALO + tm + HALO, :] = jnp.where(has_next, a[HALO + tm:HALO + tm + HALO], 0.0)
    span = buf.shape[1] - SUBLANES
    for j in range(1, SUBLANES):
        buf[j, 0:span, :] = buf[0, j:j + span, :]

    project(wd, od)

    logits = _dot(hb(), wb[...])
    ob[0] = logits
    logits_t = logits.T
    for c in range(tm // DN_CHUNK):
        obt[0, c] = logits_t[0:4 * DN_HEADS, c * DN_CHUNK:(c + 1) * DN_CHUNK]

    dh = LANES // 2
    first_map = lax.broadcasted_iota(jnp.int32, (tm, LANES), 1) < dh
    cs, sa, sb = cos[...], sin_a[...], sin_b[...]

    def norm_rope(x, gain, scale_):
        sq = x * x
        s0 = jnp.sum(jnp.where(first_map, sq, 0.0), axis=-1, keepdims=True)
        s1 = jnp.sum(jnp.where(first_map, 0.0, sq), axis=-1, keepdims=True)
        ms = jnp.where(first_map, s0, s1) * (1.0 / dh)
        y = x * lax.rsqrt(ms + RMS_EPS) * gain
        y = y * cs + pltpu.roll(y, LANES - dh // 2, 1) * sa + pltpu.roll(y, dh // 2, 1) * sb
        return y * scale_

    q_scale = dh ** -0.5 * math.log2(math.e)
    for h in range(DA_HEADS):
        hs = slice(h * LANES, (h + 1) * LANES)
        oq[0, :, hs] = norm_rope(_dot(hb(), wq[:, hs]), gq[0], q_scale).astype(oq.dtype)
        ok[0, :, hs] = norm_rope(_dot(hb(), wk[:, hs]), gk[0], 1.0).astype(ok.dtype)

    project(wvz, ovz)
    project(wm, om)

    pad = CONV_K // 2
    for r in range(0, tm, CONV_ROWS):
        acc = jnp.zeros((CONV_ROWS, buf.shape[2]), F32) + cb_ref[0]
        for k in range(CONV_K):
            whole, part = divmod(HALO - pad + k, SUBLANES)
            start = r + whole * SUBLANES
            acc = acc + buf[part, start:start + CONV_ROWS, :] * cw_ref[0, k:k + 1, :]
        mu = jnp.mean(acc, axis=-1, keepdims=True)
        xc = acc - mu
        y = xc * lax.rsqrt(jnp.mean(xc * xc, axis=-1, keepdims=True) + LN_EPS)
        y = y * lng[0] + lnb[0]
        yc_o[0, r:r + CONV_ROWS, :] = (_silu(y) * _silu(zc_s[r:r + CONV_ROWS, :])).astype(yc_o.dtype)


def _token_specs(tm, d, n_lat_blocks, ctx_block0, n_ctx_blocks):
    lat = pl.BlockSpec((1, tm, d), lambda b, i: (b, jnp.minimum(i, n_lat_blocks - 1), 0))
    ctx = pl.BlockSpec((1, tm, d),
                       lambda b, i: (b, ctx_block0 + jnp.clip(i - n_lat_blocks, 0, n_ctx_blocks - 1), 0))
    return lat, ctx


def _inproj(tokens, t, mod, norm_g, weights, conv_w, conv_b, ln_g, ln_b, cos, sin_a, sin_b, gq, gk, layer,
            n_lat_blocks, ctx_row):
    x_lat, x_ctx, ctx_block0 = tokens
    bsz, _, d = x_lat.shape
    tm = TM_PROJ
    nblk = t // tm
    wv, wg, wz, wd, wb, wq, wk, wvz, wm = weights
    bw = wv.shape[1]
    resident = lambda w: pl.BlockSpec(w.shape, lambda b, i: (0, 0), pipeline_mode=pl.Buffered(1))
    out = lambda n: pl.BlockSpec((1, tm, n), lambda b, i: (b, i, 0))
    lat_spec, ctx_spec = _token_specs(tm, d, n_lat_blocks, ctx_block0, nblk - n_lat_blocks)
    assert nblk - n_lat_blocks == 1
    xp, xn = _halo_specs(tm, d, 0, n_lat_blocks * tm // HALO)
    vec = lambda n: pl.BlockSpec((1, 1, n), lambda b, i: (layer, 0, 0))
    tab = pl.BlockSpec((tm, LANES), lambda b, i: (i, 0))
    widths = (bw, wd.shape[1], wb.shape[1], wq.shape[1], wk.shape[1], wvz.shape[1], wm.shape[1])
    dtypes = (BF16, BF16, F32, BF16, BF16, BF16, BF16)
    return pl.pallas_call(
        functools.partial(_inproj_kernel, n_lat_blocks=n_lat_blocks, n_blocks=nblk, ctx_row=ctx_row),
        grid=(bsz, nblk),
        in_specs=[
            lat_spec, ctx_spec, xp, xn,
            pl.BlockSpec(mod.shape, lambda b, i: (0, 0)),
            vec(d),
            resident(wv), resident(wg), resident(wz), resident(wd), resident(wb), resident(wq), resident(wk),
            resident(wvz), resident(wm),
            pl.BlockSpec((1, CONV_K, bw), lambda b, i: (layer, 0, 0)), vec(bw), vec(bw), vec(bw),
            tab, tab, tab, vec(LANES), vec(LANES),
        ],
        out_specs=[out(n) for n in widths] + [
            pl.BlockSpec((1, tm // DN_CHUNK, 4 * DN_HEADS, DN_CHUNK), lambda b, i: (b, i, 0, 0))],
        out_shape=[jax.ShapeDtypeStruct((bsz, t, n), dt) for n, dt in zip(widths, dtypes)] + [
            jax.ShapeDtypeStruct((bsz, t // DN_CHUNK, 4 * DN_HEADS, DN_CHUNK), F32)],
        scratch_shapes=[pltpu.VMEM((tm + 2 * HALO, d), BF16),
                        pltpu.VMEM((SUBLANES, tm + 2 * HALO, bw), F32),
                        pltpu.VMEM((tm, bw), F32)],
        compiler_params=_params("parallel", "parallel"),
        name="inproj",
    )(x_lat, x_ctx, x_lat, x_lat, mod, norm_g, wv, wg, wz, wd, wb, wq, wk, wvz, wm,
      conv_w, conv_b, ln_g, ln_b, cos, sin_a, sin_b, gq, gk)


def _segment_edges(i, n_lat, n_tot):
    has_prev = jnp.logical_and(i != 0, i != n_lat)
    has_next = jnp.logical_and(i != n_lat - 1, i != n_tot - 1)
    return has_prev, has_next


def _halo_specs(rows, width, col, n_halo_blocks):
    per = rows // HALO
    prev = pl.BlockSpec((1, HALO, width), lambda b, i: (b, jnp.maximum(i * per - 1, 0), col))
    nxt = pl.BlockSpec((1, HALO, width), lambda b, i: (b, jnp.minimum((i + 1) * per, n_halo_blocks - 1), col))
    return prev, nxt


def _split3(x):
    hi = x.astype(BF16)
    rest = x - hi.astype(F32)
    mid = rest.astype(BF16)
    lo = (rest - mid.astype(F32)).astype(BF16)
    return hi, mid, lo


def _tri_inverses(nmats, eye, ii, jj):
    b16 = lambda a: a.astype(BF16)
    same = lambda size: (ii // size) == (jj // size)
    ts = [eye - jnp.where(same(2), n, 0.0) for n in nmats]
    size = 2
    while size < nmats[0].shape[0]:
        level = jnp.logical_and(same(2 * size), jnp.logical_not(same(size)))
        cs = [b16(jnp.where(level, n, 0.0)) for n in nmats]
        tbs = [b16(t) for t in ts]
        mids = [_dot(c, t) for c, t in zip(cs, tbs)]
        mids = [b16(m) for m in mids]
        prods = [_dot(t, m) for t, m in zip(tbs, mids)]
        ts = [t - pr for t, pr in zip(ts, prods)]
        size *= 2
    return ts


def _dna_kernel(q, k, v, pq, pk, pv, nq, nk, nv, bd, bdt, cw, alog_r, bias_r, alog_c, bias_c,
                qg_o, kg_o, u_o, w_o, in_o, sd_o, act_s, *, n_lat, n_tot):
    i = pl.program_id(1)
    has_prev, has_next = _segment_edges(i, n_lat, n_tot)
    cc = DN_CHUNK
    rows = q.shape[1]
    bw = q.shape[-1]
    hd = bw // DN_HEADS
    nh = DN_HEADS
    ri = lax.broadcasted_iota(jnp.int32, (rows, rows), 0)
    rj = lax.broadcasted_iota(jnp.int32, (rows, rows), 1)
    shift_dn = (rj == ri - 1).astype(BF16)
    shift_up = (rj == ri + 1).astype(BF16)
    sub = lax.broadcasted_iota(jnp.int32, (SUBLANES, bw), 0)
    for idx, (m_, p_, n_) in enumerate(((q, pq, nq), (k, pk, nk), (v, pv, nv))):
        cols = slice(idx * bw, (idx + 1) * bw)
        x16 = m_[0]
        prev = _dot(shift_dn, x16)
        nxt = _dot(shift_up, x16)
        halo_prev = jnp.where(has_prev, p_[0, HALO - 1:HALO, :].astype(F32), 0.0)
        halo_next = jnp.where(has_next, n_[0, 0:1, :].astype(F32), 0.0)
        prev = jnp.concatenate([jnp.where(sub == 0, halo_prev, prev[0:SUBLANES]), prev[SUBLANES:]], axis=0)
        nxt = jnp.concatenate([nxt[0:rows - SUBLANES],
                               jnp.where(sub == SUBLANES - 1, halo_next, nxt[rows - SUBLANES:])], axis=0)
        conv = x16.astype(F32) * cw[0, 1:2, cols] + prev * cw[0, 0:1, cols] + nxt * cw[0, 2:3, cols]
        act_s[:, cols] = _silu(conv)

    ii = lax.broadcasted_iota(jnp.int32, (cc, cc), 0)
    jj = lax.broadcasted_iota(jnp.int32, (cc, cc), 1)
    low = jj <= ii
    upp = jj >= ii
    low16 = low.astype(BF16)
    upp16 = upp.astype(BF16)
    eye = (ii == jj).astype(F32)
    incl = (low, upp)
    strict = (jj < ii, jj > ii)
    last_row = (cc - 1, 0)

    n_ch = rows // cc
    beta_cs, g_cs, g_rs = [], [], []
    for ch in range(n_ch):
        x = bd[0, ch * cc:(ch + 1) * cc, :]
        beta_cs.append(jax.nn.sigmoid(x))
        g_cs.append(-jnp.exp(alog_r[0]) * _softplus(x + bias_r[0]))
        xt = bdt[0, ch]
        g_rs.append(-jnp.exp(alog_c[0]) * _softplus(xt + bias_c[0]))
    parts_c = _split3(jnp.concatenate(g_cs, axis=1))
    parts_r = _split3(jnp.concatenate(g_rs, axis=0))
    gf_c = sum(_dot(low16, p) for p in parts_c)
    gb_c = sum(_dot(upp16, p) for p in parts_c)
    gf_r = sum(_dot(p, upp16) for p in parts_r)
    gb_r = sum(_dot(p, low16) for p in parts_r)
    nbd = 4 * nh
    gcum_cs = [(gf_c[:, ch * LANES:(ch + 1) * LANES], gb_c[:, ch * LANES:(ch + 1) * LANES]) for ch in range(n_ch)]
    gcum_rs = [(gf_r[ch * nbd:(ch + 1) * nbd, :], gb_r[ch * nbd:(ch + 1) * nbd, :]) for ch in range(n_ch)]
    heads = []
    for ch in range(n_ch):
        act = act_s[ch * cc:(ch + 1) * cc, :]
        for h in range(nh):
            qh = act[:, h * hd:(h + 1) * hd]
            kh = act[:, bw + h * hd:bw + (h + 1) * hd]
            vh = act[:, 2 * bw + h * hd:2 * bw + (h + 1) * hd]
            qh = qh * lax.rsqrt(jnp.sum(qh * qh, axis=-1, keepdims=True) + RMS_EPS) * (hd ** -0.5)
            kh = kh * lax.rsqrt(jnp.sum(kh * kh, axis=-1, keepdims=True) + RMS_EPS)
            heads.append((ch, h, qh, kh, vh))
    k16 = [kh.astype(BF16) for (_, _, _, kh, _) in heads]
    q16 = [qh.astype(BF16) for (_, _, qh, _, _) in heads]
    kks = [_dot_nt(kb, kb) for kb in k16]
    qks = [_dot_nt(qb, kb) for qb, kb in zip(q16, k16)]

    inst = []
    nmats = []
    for (ch, h, _, _, _), kk in zip(heads, kks):
        for d in range(2):
            col = 2 * nh + nh * d + h
            gi = gcum_cs[ch][d][:, col:col + 1]
            gj = gcum_rs[ch][d][col:col + 1, :]
            dm = jnp.where(incl[d], jnp.exp(jnp.where(incl[d], gi - gj, 0.0)), 0.0)
            beta = beta_cs[ch][:, nh * d + h:nh * d + h + 1]
            nmats.append(jnp.where(strict[d], beta * kk * dm, 0.0))
            inst.append((ch, h, d, gi, beta, dm))
    ainvs = _tri_inverses(nmats, eye, ii, jj)

    rhss = []
    for (ch, h, d, gi, beta, _), ainv in zip(inst, ainvs):
        _, _, _, kh, vh = heads[ch * nh + h]
        rhss.append(jnp.concatenate([vh * beta, kh * (beta * jnp.exp(gi))], axis=1).astype(BF16))
    a16 = [a.astype(BF16) for a in ainvs]
    sols = [_dot(a, r) for a, r in zip(a16, rhss)]

    sd_rows = [[[] for _ in range(2)] for _ in range(n_ch)]
    for (ch, h, d, gi, _, dm), sol in zip(inst, sols):
        _, _, qh, kh, _ = heads[ch * nh + h]
        rs = slice(ch * cc, (ch + 1) * cc)
        hs = slice(h * hd, (h + 1) * hd)
        glast = gi[last_row[d]:last_row[d] + 1, :]
        u_o[d, 0, rs, hs] = sol[:, 0:hd].astype(u_o.dtype)
        w_o[d, 0, rs, hs] = sol[:, hd:2 * hd].astype(w_o.dtype)
        qg_o[d, 0, rs, hs] = (qh * jnp.exp(gi)).astype(qg_o.dtype)
        kg_o[d, 0, rs, hs] = (kh * jnp.exp(glast - gi)).astype(kg_o.dtype)
        in_o[d, 0, rs, h * cc:(h + 1) * cc] = (qks[ch * nh + h] * dm).astype(in_o.dtype)
        sd_rows[ch][d].append(jnp.broadcast_to(jnp.exp(glast), (1, LANES)))
    for ch in range(n_ch):
        for d in range(2):
            sd_o[d, 0, ch] = jnp.concatenate(sd_rows[ch][d] + [jnp.zeros((8 - nh, LANES), F32)], axis=0)


def _deltanet_stage_a(pd, pbd, pbdt, dn_conv_w, alog_r, bias_r, alog_c, bias_c, layer, n_lat, n_tot):
    bsz, t, _ = pd.shape
    bw = dn_conv_w.shape[-1] // 3
    cc = DN_CHUNK
    rows = DNA_CHUNKS * cc
    main = lambda col: pl.BlockSpec((1, rows, bw), lambda b, i: (b, i, col))
    halos = [_halo_specs(rows, bw, col, t // HALO) for col in range(3)]
    small = lambda a: pl.BlockSpec((1,) + a.shape[1:], lambda b, i: (layer,) + (0,) * (a.ndim - 1))
    tok = lambda n, dt: jax.ShapeDtypeStruct((2, bsz, t, n), dt)
    tok_spec = lambda n: pl.BlockSpec((2, 1, rows, n), lambda b, i: (0, b, i, 0))
    return pl.pallas_call(
        functools.partial(_dna_kernel, n_lat=n_lat, n_tot=n_tot),
        grid=(bsz, n_tot),
        in_specs=[main(0), main(1), main(2),
                  halos[0][0], halos[1][0], halos[2][0], halos[0][1], halos[1][1], halos[2][1],
                  pl.BlockSpec((1, rows, LANES), lambda b, i: (b, i, 0)),
                  pl.BlockSpec((1, DNA_CHUNKS, 4 * DN_HEADS, cc), lambda b, i: (b, i, 0, 0)),
                  small(dn_conv_w), small(alog_r), small(bias_r), small(alog_c), small(bias_c)],
        out_specs=[tok_spec(bw), tok_spec(bw), tok_spec(bw), tok_spec(bw), tok_spec(DN_HEADS * cc),
                   pl.BlockSpec((2, 1, DNA_CHUNKS, 8, LANES), lambda b, i: (0, b, i, 0, 0))],
        out_shape=[tok(bw, BF16), tok(bw, BF16), tok(bw, BF16), tok(bw, BF16), tok(DN_HEADS * cc, BF16),
                   jax.ShapeDtypeStruct((2, bsz, t // cc, 8, LANES), F32)],
        scratch_shapes=[pltpu.VMEM((rows, 3 * bw), F32)],
        compiler_params=_params("parallel", "parallel"),
        name="deltanet_a",
    )(pd, pd, pd, pd, pd, pd, pd, pd, pd, pbd, pbdt, dn_conv_w, alog_r, bias_r, alog_c, bias_c)


def _dnb_kernel(qg_f, kg_f, u_f, w_f, in_f, sd_f, qg_b, kg_b, u_b, w_b, in_b, sd_b, of_ref, ob_ref, state):
    cc = DN_CHUNK
    hd = state.shape[-1]
    bsz = state.shape[1]

    @pl.when(pl.program_id(0) == 0)
    def _():
        state[...] = jnp.zeros_like(state)

    dirs = ((qg_f, kg_f, u_f, w_f, in_f, sd_f, of_ref), (qg_b, kg_b, u_b, w_b, in_b, sd_b, ob_ref))
    chains = [(d, b, h) for d in range(2) for b in range(bsz) for h in range(DN_HEADS)]
    hs = lambda h: slice(h * hd, (h + 1) * hd)
    n_ch = qg_f.shape[2] // cc
    s32 = [state[d, b, h] for d, b, h in chains]
    for step in range(n_ch):
        chunk = (step, n_ch - 1 - step)
        rs = [slice(c * cc, (c + 1) * cc) for c in chunk]
        s16 = [s.astype(BF16) for s in s32]
        ws = [_dot(dirs[d][3][0, b, rs[d], hs(h)], s) for (d, b, h), s in zip(chains, s16)]
        qs = [_dot(dirs[d][0][0, b, rs[d], hs(h)], s) for (d, b, h), s in zip(chains, s16)]
        vnew = [(dirs[d][2][0, b, rs[d], hs(h)].astype(F32) - x).astype(BF16) for (d, b, h), x in zip(chains, ws)]
        intra = [_dot(dirs[d][4][0, b, rs[d], h * cc:(h + 1) * cc], v) for (d, b, h), v in zip(chains, vnew)]
        upd = [_dot_tn(dirs[d][1][0, b, rs[d], hs(h)], v) for (d, b, h), v in zip(chains, vnew)]
        for (d, b, h), o1, o2 in zip(chains, qs, intra):
            dirs[d][6][b, rs[d], hs(h)] = (o1 + o2).astype(dirs[d][6].dtype)
        s32 = [s * dirs[d][5][0, b, chunk[d], h:h + 1, :] + up for (d, b, h), s, up in zip(chains, s32, upd)]
    for (d, b, h), s in zip(chains, s32):
        state[d, b, h] = s


def _deltanet_stage_b(qg, kg, u, w, intra, sd, n_lat, n_tot):
    _, bsz, t, bw = qg.shape
    rows = DNB_CHUNKS * DN_CHUNK
    hd = bw // DN_HEADS
    n_ctx = n_tot - n_lat
    block_f = lambda s: jnp.where(s < n_ctx, n_lat + s, s - n_ctx)
    block_b = lambda s: n_tot - 1 - s

    def specs(d, block):
        tok = lambda n: pl.BlockSpec((1, bsz, rows, n), lambda s: (d, 0, block(s), 0))
        return [tok(bw), tok(bw), tok(bw), tok(bw), tok(DN_HEADS * DN_CHUNK),
                pl.BlockSpec((1, bsz, DNB_CHUNKS, 8, LANES), lambda s: (d, 0, block(s), 0, 0))]

    out = lambda block: pl.BlockSpec((bsz, rows, bw), lambda s: (0, block(s), 0))
    args = (qg, kg, u, w, intra, sd)
    return pl.pallas_call(
        _dnb_kernel,
        grid=(n_tot,),
        in_specs=specs(0, block_f) + specs(1, block_b),
        out_specs=[out(block_f), out(block_b)],
        out_shape=[jax.ShapeDtypeStruct((bsz, t, bw), BF16)] * 2,
        scratch_shapes=[pltpu.VMEM((2, bsz, DN_HEADS, hd, hd), F32)],
        compiler_params=_params("arbitrary"),
        name="deltanet_b",
    )(*args, *args)


def _attn_kernel(q_ref, k_ref, v_ref, z_ref, lam_ref, g_ref, o_ref, qs, vx, m_s, acc, *, lam_init):
    tq = q_ref.shape[1]
    n_keys = k_ref.shape[1]
    dh = LANES // 2
    n_full, rem = divmod(n_keys, ATT_KEYS)

    @pl.when(pl.program_id(2) == 0)
    def _():
        vx[:, 0:LANES] = v_ref[0]
        vx[:, LANES:2 * LANES] = jnp.ones((n_keys, LANES), vx.dtype)

    q = q_ref[0].astype(F32)
    lane = lax.broadcasted_iota(jnp.int32, q.shape, 1)
    qs[0:tq, :] = jnp.where(lane < dh, q, 0.0).astype(qs.dtype)
    qs[tq:2 * tq, :] = jnp.where(lane >= dh, q, 0.0).astype(qs.dtype)
    m_s[...] = jnp.full_like(m_s, -jnp.inf)
    acc[...] = jnp.zeros_like(acc)

    key_blocks = [(j * ATT_KEYS, ATT_KEYS) for j in range(n_full)] + ([(n_full * ATT_KEYS, rem)] if rem else [])
    groups = [slice(r0, r0 + ATT_ROWS) for r0 in range(0, 2 * tq, ATT_ROWS)]
    tasks = [(k0, size, rows) for k0, size in key_blocks for rows in groups]
    scores = lambda k0, size, rows: _dot_nt(qs[rows, :], k_ref[0, k0:k0 + size, :])
    s_next = scores(*tasks[0])
    for i, (k0, size, rows) in enumerate(tasks):
        s = s_next
        if i + 1 < len(tasks):
            s_next = scores(*tasks[i + 1])
        m_prev = m_s[rows, :]
        m_new = jnp.maximum(m_prev, jnp.max(s, axis=1, keepdims=True))
        alpha = jnp.exp2(m_prev - m_new)
        p = jnp.exp2(s - m_new[:, 0:1])
        pv = _dot(p.astype(BF16), vx[k0:k0 + size, :])
        acc[rows, 0:LANES] = alpha * acc[rows, 0:LANES] + pv[:, 0:LANES]
        acc[rows, LANES:2 * LANES] = alpha * acc[rows, LANES:2 * LANES] + pv[:, LANES:2 * LANES]
        m_s[rows, :] = m_new

    lm = lam_ref[0]
    lam = (jnp.exp(jnp.sum(lm[0:1] * lm[1:2], axis=1, keepdims=True))
           - jnp.exp(jnp.sum(lm[2:3] * lm[3:4], axis=1, keepdims=True)) + lam_init)
    on = acc[:, 0:LANES] / acc[:, LANES:2 * LANES]
    o = on[0:tq] - lam * on[tq:2 * tq]
    y = o * lax.rsqrt(jnp.mean(o * o, axis=-1, keepdims=True) + RMS_EPS) * g_ref[0] * (1.0 - lam_init)
    o_ref[0] = (y * _silu(z_ref[0].astype(F32))).astype(o_ref.dtype)


def _diff_attention(qn, kn, pa, da_lambda, subln_g, layer, lam_init, *, tq, q_rows, q_off, k_rows, k_off):
    bsz = qn.shape[0]
    v_col = 0
    z_col = DA_HEADS
    qb, kb = q_off // tq, k_off // k_rows
    return pl.pallas_call(
        functools.partial(_attn_kernel, lam_init=lam_init),
        grid=(bsz, DA_HEADS, q_rows // tq),
        in_specs=[pl.BlockSpec((1, tq, LANES), lambda b, h, i: (b, i + qb, h)),
                  pl.BlockSpec((1, k_rows, LANES), lambda b, h, i: (b, kb, h)),
                  pl.BlockSpec((1, k_rows, LANES), lambda b, h, i: (b, kb, v_col + h)),
                  pl.BlockSpec((1, tq, LANES), lambda b, h, i: (b, i + qb, z_col + h)),
                  pl.BlockSpec((1,) + da_lambda.shape[1:], lambda b, h, i: (layer, 0, 0)),
                  pl.BlockSpec((1, 1, LANES), lambda b, h, i: (layer, 0, 0))],
        out_specs=pl.BlockSpec((1, tq, LANES), lambda b, h, i: (b, i, h)),
        out_shape=jax.ShapeDtypeStruct((bsz, q_rows, DA_HEADS * LANES), BF16),
        scratch_shapes=[pltpu.VMEM((2 * tq, LANES), BF16),
                        pltpu.VMEM((k_rows, 2 * LANES), BF16),
                        pltpu.VMEM((2 * tq, LANES), F32),
                        pltpu.VMEM((2 * tq, 2 * LANES), F32)],
        compiler_params=_params("parallel", "parallel", "arbitrary"),
        name="diff_attn",
    )(qn, kn, pa, pa, da_lambda, subln_g)


def _merge_kernel(x_ref, xc_ref, yc, of, ob, dz, ydl, ydc, mg, mod_ref, dng, wb, wo, o_ref, *,
                  n_lat_blocks, ctx_row):
    d = x_ref.shape[-1]
    b = pl.program_id(0)
    i = pl.program_id(1)
    is_ctx = i >= n_lat_blocks
    row = jnp.where(is_ctx, ctx_row, b)
    gate = mod_ref[pl.ds(row, 1), 2 * d:3 * d]
    o = of[0].astype(F32) + ob[0].astype(F32)
    hd = dng.shape[-1]
    parts = []
    for h in range(DN_HEADS):
        oh = o[:, h * hd:(h + 1) * hd]
        parts.append(oh * lax.rsqrt(jnp.mean(oh * oh, axis=-1, keepdims=True) + RMS_EPS) * dng[0])
    ydn = (jnp.concatenate(parts, axis=1) * _silu(dz[0].astype(F32))).astype(BF16)
    yda = jnp.where(is_ctx, ydc[0], ydl[0])
    merged = (jax.nn.sigmoid(mg[0, :, 0:d].astype(F32)) * _dot(yc[0], wb[0, 0])
              + jax.nn.sigmoid(mg[0, :, d:2 * d].astype(F32)) * _dot(ydn, wb[0, 1])
              + jax.nn.sigmoid(mg[0, :, 2 * d:3 * d].astype(F32)) * _dot(yda, wb[0, 2]))
    x = jnp.where(is_ctx, xc_ref[0], x_ref[0])
    o_ref[0] = x + gate * _dot(merged.astype(BF16), wo[0])


def _merge(tokens, y_conv, o_fwd, o_bwd, pd, yd_lat, yd_ctx, pm, mod, dn_norm_g, w_branch, w_out, layer,
           n_lat_blocks, n_blocks, ctx_row):
    x_lat, x_ctx, ctx_block0 = tokens
    bsz, _, d = x_lat.shape
    tm = TM_PROJ
    bw = y_conv.shape[-1]
    n_ctx_blocks = yd_ctx.shape[1] // tm
    tok = lambda n, col=0: pl.BlockSpec((1, tm, n), lambda b, i: (b, i, col))
    lat_spec, ctx_spec = _token_specs(tm, d, n_lat_blocks, ctx_block0, n_ctx_blocks)
    return pl.pallas_call(
        functools.partial(_merge_kernel, n_lat_blocks=n_lat_blocks, ctx_row=ctx_row),
        grid=(bsz, n_blocks),
        in_specs=[lat_spec, ctx_spec, tok(bw), tok(bw), tok(bw), tok(bw, 3),
                  pl.BlockSpec((1, tm, bw), lambda b, i: (b, jnp.minimum(i, n_lat_blocks - 1), 0)),
                  pl.BlockSpec((1, tm, bw),
                               lambda b, i: (b, jnp.clip(i - n_lat_blocks, 0, n_ctx_blocks - 1), 0)),
                  tok(N_BRANCH * d),
                  pl.BlockSpec(mod.shape, lambda b, i: (0, 0)),
                  pl.BlockSpec((1, 1, dn_norm_g.shape[-1]), lambda b, i: (layer, 0, 0)),
                  pl.BlockSpec((1,) + w_branch.shape[1:], lambda b, i: (layer, 0, 0, 0)),
                  pl.BlockSpec((1,) + w_out.shape[1:], lambda b, i: (layer, 0, 0))],
        out_specs=tok(d),
        out_shape=jax.ShapeDtypeStruct((bsz, n_blocks * tm, d), F32),
        compiler_params=_params("parallel", "parallel"),
        name="merge",
    )(x_lat, x_ctx, y_conv, o_fwd, o_bwd, pd, yd_lat, yd_ctx, pm, mod, dn_norm_g, w_branch, w_out)


def _rope_tables(seq, ctx_len, dh):
    n_freq = dh // 4
    inv_freq = ROPE_BASE ** (-jnp.arange(n_freq, dtype=F32) / n_freq)
    n_rows = seq // GRID_W
    row_ang = jnp.arange(n_rows, dtype=F32)[:, None] * inv_freq
    col_ang = jnp.arange(GRID_W, dtype=F32)[:, None] * inv_freq
    cos_r, sin_r, cos_c, sin_c = lax.optimization_barrier(
        (jnp.cos(row_ang), jnp.sin(row_ang), jnp.cos(col_ang), jnp.sin(col_ang)))

    def table(by_row, by_col):
        r = jnp.broadcast_to(by_row[:, None, :], (n_rows, GRID_W, n_freq)).reshape(seq, n_freq)
        c = jnp.broadcast_to(by_col[None, :, :], (n_rows, GRID_W, n_freq)).reshape(seq, n_freq)
        return jnp.tile(jnp.concatenate([r, c], axis=-1), (1, 2 * LANES // dh))

    cos = table(cos_r, cos_c)
    sin = table(sin_r, sin_c)
    first_half = (jnp.arange(LANES) % dh) < dh // 2
    sin_a = jnp.where(first_half, -sin, 0.0)
    sin_b = jnp.where(first_half, 0.0, sin)
    pad = lambda tbl, fill: jnp.concatenate([tbl, jnp.full((ctx_len, LANES), fill, F32)], axis=0)
    return pad(cos, 1.0), pad(sin_a, 0.0), pad(sin_b, 0.0)


def kernel(x, c, ctx, c_ctx, w_ada, b_ada, norm_g, w_in, conv_w, conv_b, conv_ln_g, conv_ln_b, dn_conv_w,
           dn_a_log, dn_dt_bias, dn_norm_g, da_q_norm_g, da_k_norm_g, da_lambda, da_subln_g, w_branch, w_out):
    bsz, seq, d = x.shape
    ctx_len = ctx.shape[1]
    depth = w_in.shape[0]
    bw = d // 2
    dh = bw // (2 * DA_HEADS)
    t = seq + ctx_len
    dna_rows = DNA_CHUNKS * DN_CHUNK
    dnb_rows = DNB_CHUNKS * DN_CHUNK
    assert conv_w.shape[1] == CONV_K and dn_conv_w.shape[1] == SHORT_K and w_branch.shape[1] == N_BRANCH
    assert seq % dnb_rows == 0 and ctx_len % dnb_rows == 0
    assert 2 * dh == LANES and bw // DN_HEADS == LANES and bsz + 1 <= 8
    assert seq % TQ == 0 and seq % ctx_len == 0 and seq % TM_PROJ == 0 and ctx_len % TM_PROJ == 0
    assert seq % dna_rows == 0 and ctx_len % dna_rows == 0 and (2 * TQ) % ATT_ROWS == 0
    n_lat_blocks, n_blocks = seq // TM_PROJ, t // TM_PROJ
    ctx_row = bsz

    tokens = (x, ctx, 0)
    cvec =jnp.concatenate([c, c_ctx[None, :], jnp.zeros((8 - bsz - 1, d), F32)], axis=0)
    cos, sin_a, sin_b = _rope_tables(seq, ctx_len, dh)

    e_conv, e_dn = 3 * bw, 7 * bw
    e_bd = e_dn + 4 * DN_HEADS
    e_da = e_bd + 4 * bw
    nbd = 4 * DN_HEADS
    row3 = lambda a: a.reshape(depth, 1, a.shape[-1])
    gate_row = lambda a: jnp.pad(a.reshape(depth, 1, 2 * DN_HEADS), ((0, 0), (0, 0), (2 * DN_HEADS, LANES - nbd)))
    gate_col = lambda a: jnp.pad(a.reshape(depth, 2 * DN_HEADS, 1), ((0, 0), (2 * DN_HEADS, 0), (0, 0)))
    alog_r, bias_r = gate_row(dn_a_log), gate_row(dn_dt_bias)
    alog_c, bias_c = gate_col(dn_a_log), gate_col(dn_dt_bias)
    tile2 = lambda a: row3(jnp.tile(a, (1, 2)))
    gq, gk = tile2(da_q_norm_g), tile2(da_k_norm_g)
    b_ada3, norm_g3 = row3(b_ada), row3(norm_g)
    conv_b3, ln_g3, ln_b3 = row3(conv_b), row3(conv_ln_g), row3(conv_ln_b)
    dng3, subln3 = row3(dn_norm_g), row3(da_subln_g)
    w_branch16, w_out16 = w_branch.astype(BF16), w_out.astype(BF16)

    for layer in range(depth):
        last = layer == depth - 1
        lam_init = 0.8 - 0.6 * math.exp(-0.3 * layer)
        wl = w_in[layer]
        cols = lambda lo, hi: wl[:, lo:hi].astype(BF16)
        weights = (cols(0, bw), cols(bw, 2 * bw), cols(2 * bw, e_conv),
                   cols(e_conv, e_dn),
                   jnp.pad(wl[:, e_dn:e_bd], ((0, 0), (0, LANES - nbd))).astype(BF16),
                   cols(e_bd, e_bd + bw), cols(e_bd + bw, e_bd + 2 * bw),
                   cols(e_bd + 2 * bw, e_da), cols(e_da, wl.shape[1]))
        mod = _adaln(cvec, w_ada, b_ada3, layer)
        y_conv, pd, pbd, qn, kn, pa, pm, pbdt = _inproj(tokens, t, mod, norm_g3, weights, conv_w, conv_b3, ln_g3, ln_b3,
                                                  cos, sin_a, sin_b, gq, gk, layer, n_lat_blocks, ctx_row)

        out_blocks = n_lat_blocks if last else n_blocks

        qg, kg, u, w, intra, sd = _deltanet_stage_a(pd, pbd, pbdt, dn_conv_w, alog_r, bias_r, alog_c, bias_c,
                                                    layer, seq // dna_rows, t // dna_rows)
        o_fwd, o_bwd = _deltanet_stage_b(qg, kg, u, w, intra, sd, seq // dnb_rows, t // dnb_rows)

        yd_lat = _diff_attention(qn, kn, pa, da_lambda, subln3, layer, lam_init,
                                 tq=TQ, q_rows=seq, q_off=0, k_rows=t, k_off=0)
        yd_ctx = _diff_attention(qn, kn, pa, da_lambda, subln3, layer, lam_init,
                                 tq=ctx_len, q_rows=ctx_len, q_off=seq, k_rows=ctx_len, k_off=seq)

        xs = _merge(tokens, y_conv, o_fwd, o_bwd, pd, yd_lat, yd_ctx, pm, mod, dng3, w_branch16, w_out16, layer,
                    n_lat_blocks, out_blocks, ctx_row)
        tokens = (xs, xs, n_lat_blocks)
    return xs
```

```python
import functools
import math

import jax
import jax.numpy as jnp
from jax import lax
from jax.experimental import pallas as pl
from jax.experimental.pallas import tpu as pltpu

F32 = jnp.float32
BF16 = jnp.bfloat16
HIGHEST = lax.Precision.HIGHEST

GRID_W = 64
N_BRANCH = 3
CONV_K = 31
DN_HEADS = 4
DN_CHUNK = 64
SHORT_K = 3
DA_HEADS = 4
ROPE_BASE = 10000.0
RMS_EPS = 1e-6
LN_EPS = 1e-5

LANES = 128
SUBLANES = 8
SUBLANES_BF16 = 16
HALO = SUBLANES_BF16
VMEM_LIMIT = 56 * 1024 * 1024

TM_PROJ = 256
PROJ_COLS = 512
CONV_ROWS = 64
DNA_CHUNKS = 4
DNB_CHUNKS = 4
TQ = 512
ATT_KEYS = 4096
ATT_ROWS = 512


def _silu(x):
    return x * jax.nn.sigmoid(x)


def _softplus(x):
    return jnp.maximum(x, 0.0) + jnp.log(1.0 + jnp.exp(-jnp.abs(x)))


def _dot(a, b):
    return jnp.dot(a, b, preferred_element_type=F32)


def _dot_hi(a, b):
    return jnp.dot(a, b, preferred_element_type=F32, precision=HIGHEST)


def _dot_nt(a, b):
    return lax.dot_general(a, b, (((1,), (1,)), ((), ())), preferred_element_type=F32)


def _dot_tn(a, b):
    return lax.dot_general(a, b, (((0,), (0,)), ((), ())), preferred_element_type=F32)


def _params(*sem):
    return pltpu.CompilerParams(dimension_semantics=sem, vmem_limit_bytes=VMEM_LIMIT)


def _adaln_kernel(c_ref, w_ref, b_ref, o_ref):
    o_ref[...] = _dot_hi(_silu(c_ref[...]), w_ref[0]) + b_ref[0]


def _adaln(cvec, w_ada, b_ada3, layer):
    d = cvec.shape[1]
    return pl.pallas_call(
        _adaln_kernel,
        grid=(3,),
        in_specs=[
            pl.BlockSpec((8, d), lambda j: (0, 0)),
            pl.BlockSpec((1, d, d), lambda j: (layer, 0, j)),
            pl.BlockSpec((1, 1, d), lambda j: (layer, 0, j)),
        ],
        out_specs=pl.BlockSpec((8, d), lambda j: (0, j)),
        out_shape=jax.ShapeDtypeStruct((8, 3 * d), F32),
        compiler_params=_params("parallel"),
        name="adaln",
    )(cvec, w_ada, b_ada3)


def _inproj_kernel(x_ref, xc_ref, xp_ref, xn_ref, mod_ref, g_ref, wv, wg, wz, wd, wb, wq, wk, wvz, wm,
                   cw_ref, cb_ref, lng, lnb, cos, sin_a, sin_b, gq, gk,
                   yc_o, od, ob, oq, ok, ovz, om, obt, hb_s, buf, zc_s, *, n_lat_blocks, n_blocks, ctx_row):
    d = x_ref.shape[-1]
    tm = x_ref.shape[1]
    b = pl.program_id(0)
    i = pl.program_id(1)
    row = jnp.where(i >= n_lat_blocks, ctx_row, b)
    m = mod_ref[pl.ds(row, 1), :]
    shift = m[:, 0:d]
    scale = m[:, d:2 * d]

    def modulated(x):
        ms = jnp.mean(x * x, axis=-1, keepdims=True)
        return ((x * lax.rsqrt(ms + RMS_EPS) * g_ref[0]) * (1.0 + scale) + shift).astype(BF16)

    hb_s[0:HALO, :] = modulated(xp_ref[0])
    hb_s[HALO:HALO + tm, :] = modulated(jnp.where(i >= n_lat_blocks, xc_ref[0], x_ref[0]))
    hb_s[HALO + tm:HALO + tm + HALO, :] = modulated(xn_ref[0])
    hb_ext = lambda: hb_s[...]
    hb = lambda: hb_s[HALO:HALO + tm, :]

    def project(w, o):
        n = w.shape[2]
        for c0 in range(0, n, PROJ_COLS):
            c1 = min(c0 + PROJ_COLS, n)
            o[0, :, c0:c1] = _dot(hb(), w[0, :, c0:c1]).astype(o.dtype)

    has_prev, has_next = _segment_edges(i, n_lat_blocks, n_blocks)
    a = _dot(hb_ext(), wv[0]) * jax.nn.sigmoid(_dot(hb_ext(), wg[0]))
    zc_s[...] = _dot(hb(), wz[0])
    buf[0, HALO:HALO + tm, :] = a[HALO:HALO + tm]
    buf[0, 0:HALO, :] = jnp.where(has_prev, a[0:HALO], 0.0)
    buf[0, HALO + tm:HALO + tm + HALO, :] = jnp.where(has_next, a[HALO + tm:HALO + tm + HALO], 0.0)
    span = buf.shape[1] - SUBLANES
    for j in range(1, SUBLANES):
        buf[j, 0:span, :] = buf[0, j:j + span, :]

    project(wd, od)

    logits = _dot(hb(), wb[0])
    ob[0] = logits
    logits_t = logits.T
    for c in range(tm // DN_CHUNK):
        obt[0, c] = logits_t[0:4 * DN_HEADS, c * DN_CHUNK:(c + 1) * DN_CHUNK]

    dh = LANES // 2
    first_map = lax.broadcasted_iota(jnp.int32, (tm, LANES), 1) < dh
    cs, sa, sb = cos[...], sin_a[...], sin_b[...]

    def norm_rope(x, gain, scale_):
        sq = x * x
        s0 = jnp.sum(jnp.where(first_map, sq, 0.0), axis=-1, keepdims=True)
        s1 = jnp.sum(jnp.where(first_map, 0.0, sq), axis=-1, keepdims=True)
        ms = jnp.where(first_map, s0, s1) * (1.0 / dh)
        y = x * lax.rsqrt(ms + RMS_EPS) * gain
        y = y * cs + pltpu.roll(y, LANES - dh // 2, 1) * sa + pltpu.roll(y, dh // 2, 1) * sb
        return y * scale_

    q_scale = dh ** -0.5 * math.log2(math.e)
    for h in range(DA_HEADS):
        hs = slice(h * LANES, (h + 1) * LANES)
        oq[0, :, hs] = norm_rope(_dot(hb(), wq[0, :, hs]), gq[0], q_scale).astype(oq.dtype)
        ok[0, :, hs] = norm_rope(_dot(hb(), wk[0, :, hs]), gk[0], 1.0).astype(ok.dtype)

    project(wvz, ovz)
    project(wm, om)

    pad = CONV_K // 2
    for r in range(0, tm, CONV_ROWS):
        acc = jnp.zeros((CONV_ROWS, buf.shape[2]), F32) + cb_ref[0]
        for k in range(CONV_K):
            whole, part = divmod(HALO - pad + k, SUBLANES)
            start = r + whole * SUBLANES
            acc = acc + buf[part, start:start + CONV_ROWS, :] * cw_ref[0, k:k + 1, :]
        mu = jnp.mean(acc, axis=-1, keepdims=True)
        xc = acc - mu
        y = xc * lax.rsqrt(jnp.mean(xc * xc, axis=-1, keepdims=True) + LN_EPS)
        y = y * lng[0] + lnb[0]
        yc_o[0, r:r + CONV_ROWS, :] = (_silu(y) * _silu(zc_s[r:r + CONV_ROWS, :])).astype(yc_o.dtype)


def _token_specs(tm, d, n_lat_blocks, ctx_block0, n_ctx_blocks):
    lat = pl.BlockSpec((1, tm, d), lambda b, i: (b, jnp.minimum(i, n_lat_blocks - 1), 0))
    ctx = pl.BlockSpec((1, tm, d),
                       lambda b, i: (b, ctx_block0 + jnp.clip(i - n_lat_blocks, 0, n_ctx_blocks - 1), 0))
    return lat, ctx


def _inproj(tokens, t, mod, norm_g, weights, conv_w, conv_b, ln_g, ln_b, cos, sin_a, sin_b, gq, gk, layer,
            n_lat_blocks, ctx_row):
    x_lat, x_ctx, ctx_block0 = tokens
    bsz, _, d = x_lat.shape
    tm = TM_PROJ
    nblk = t // tm
    w_packed, groups = weights
    bw = groups[0][0]
    resident = lambda width, blk: pl.BlockSpec((1, d, width), lambda b, i: (layer, 0, blk),
                                               pipeline_mode=pl.Buffered(1))
    out = lambda n: pl.BlockSpec((1, tm, n), lambda b, i: (b, i, 0))
    lat_spec, ctx_spec = _token_specs(tm, d, n_lat_blocks, ctx_block0, nblk - n_lat_blocks)
    assert nblk - n_lat_blocks == 1
    xp, xn = _halo_specs(tm, d, 0, n_lat_blocks * tm // HALO)
    vec = lambda n: pl.BlockSpec((1, 1, n), lambda b, i: (layer, 0, 0))
    tab = pl.BlockSpec((tm, LANES), lambda b, i: (i, 0))
    widths = (bw,) + tuple(width for width, _ in groups[3:])
    dtypes = (BF16, BF16, F32, BF16, BF16, BF16, BF16)
    return pl.pallas_call(
        functools.partial(_inproj_kernel, n_lat_blocks=n_lat_blocks, n_blocks=nblk, ctx_row=ctx_row),
        grid=(bsz, nblk),
        in_specs=[
            lat_spec, ctx_spec, xp, xn,
            pl.BlockSpec(mod.shape, lambda b, i: (0, 0)),
            vec(d),
            *[resident(width, blk) for width, blk in groups],
            pl.BlockSpec((1, CONV_K, bw), lambda b, i: (layer, 0, 0)), vec(bw), vec(bw), vec(bw),
            tab, tab, tab, vec(LANES), vec(LANES),
        ],
        out_specs=[out(n) for n in widths] + [
            pl.BlockSpec((1, tm // DN_CHUNK, 4 * DN_HEADS, DN_CHUNK), lambda b, i: (b, i, 0, 0))],
        out_shape=[jax.ShapeDtypeStruct((bsz, t, n), dt) for n, dt in zip(widths, dtypes)] + [
            jax.ShapeDtypeStruct((bsz, t // DN_CHUNK, 4 * DN_HEADS, DN_CHUNK), F32)],
        scratch_shapes=[pltpu.VMEM((tm + 2 * HALO, d), BF16),
                        pltpu.VMEM((SUBLANES, tm + 2 * HALO, bw), F32),
                        pltpu.VMEM((tm, bw), F32)],
        compiler_params=_params("parallel", "parallel"),
        name="inproj",
    )(x_lat, x_ctx, x_lat, x_lat, mod, norm_g, *[w_packed] * len(groups),
      conv_w, conv_b, ln_g, ln_b, cos, sin_a, sin_b, gq, gk)


def _segment_edges(i, n_lat, n_tot):
    has_prev = jnp.logical_and(i != 0, i != n_lat)
    has_next = jnp.logical_and(i != n_lat - 1, i != n_tot - 1)
    return has_prev, has_next


def _halo_specs(rows, width, col, n_halo_blocks):
    per = rows // HALO
    prev = pl.BlockSpec((1, HALO, width), lambda b, i: (b, jnp.maximum(i * per - 1, 0), col))
    nxt = pl.BlockSpec((1, HALO, width), lambda b, i: (b, jnp.minimum((i + 1) * per, n_halo_blocks - 1), col))
    return prev, nxt


def _split3(x):
    hi = x.astype(BF16)
    rest = x - hi.astype(F32)
    mid = rest.astype(BF16)
    lo = (rest - mid.astype(F32)).astype(BF16)
    return hi, mid, lo


def _tri_inverses(nmats, eye, ii, jj):
    b16 = lambda a: a.astype(BF16)
    same = lambda size: (ii // size) == (jj // size)
    ts = [eye - jnp.where(same(2), n, 0.0) for n in nmats]
    size = 2
    while size < nmats[0].shape[0]:
        level = jnp.logical_and(same(2 * size), jnp.logical_not(same(size)))
        cs = [b16(jnp.where(level, n, 0.0)) for n in nmats]
        tbs = [b16(t) for t in ts]
        mids = [_dot(c, t) for c, t in zip(cs, tbs)]
        mids = [b16(m) for m in mids]
        prods = [_dot(t, m) for t, m in zip(tbs, mids)]
        ts = [t - pr for t, pr in zip(ts, prods)]
        size *= 2
    return ts


def _dna_kernel(q, k, v, pq, pk, pv, nq, nk, nv, bd, bdt, cw, alog_r, bias_r, alog_c, bias_c,
                qg_o, kg_o, u_o, w_o, in_o, sd_o, act_s, *, n_lat, n_tot):
    i = pl.program_id(1)
    has_prev, has_next = _segment_edges(i, n_lat, n_tot)
    cc = DN_CHUNK
    rows = q.shape[1]
    bw = q.shape[-1]
    hd = bw // DN_HEADS
    nh = DN_HEADS
    ri = lax.broadcasted_iota(jnp.int32, (rows, rows), 0)
    rj = lax.broadcasted_iota(jnp.int32, (rows, rows), 1)
    shift_dn = (rj == ri - 1).astype(BF16)
    shift_up = (rj == ri + 1).astype(BF16)
    sub = lax.broadcasted_iota(jnp.int32, (SUBLANES, bw), 0)
    for idx, (m_, p_, n_) in enumerate(((q, pq, nq), (k, pk, nk), (v, pv, nv))):
        cols = slice(idx * bw, (idx + 1) * bw)
        x16 = m_[0]
        prev = _dot(shift_dn, x16)
        nxt = _dot(shift_up, x16)
        halo_prev = jnp.where(has_prev, p_[0, HALO - 1:HALO, :].astype(F32), 0.0)
        halo_next = jnp.where(has_next, n_[0, 0:1, :].astype(F32), 0.0)
        prev = jnp.concatenate([jnp.where(sub == 0, halo_prev, prev[0:SUBLANES]), prev[SUBLANES:]], axis=0)
        nxt = jnp.concatenate([nxt[0:rows - SUBLANES],
                               jnp.where(sub == SUBLANES - 1, halo_next, nxt[rows - SUBLANES:])], axis=0)
        conv = x16.astype(F32) * cw[0, 1:2, cols] + prev * cw[0, 0:1, cols] + nxt * cw[0, 2:3, cols]
        act_s[:, cols] = _silu(conv)

    ii = lax.broadcasted_iota(jnp.int32, (cc, cc), 0)
    jj = lax.broadcasted_iota(jnp.int32, (cc, cc), 1)
    low = jj <= ii
    upp = jj >= ii
    low16 = low.astype(BF16)
    upp16 = upp.astype(BF16)
    eye = (ii == jj).astype(F32)
    incl = (low, upp)
    strict = (jj < ii, jj > ii)
    last_row = (cc - 1, 0)

    n_ch = rows // cc
    beta_cs, g_cs, g_rs = [], [], []
    for ch in range(n_ch):
        x = bd[0, ch * cc:(ch + 1) * cc, :]
        beta_cs.append(jax.nn.sigmoid(x))
        g_cs.append(-jnp.exp(alog_r[0]) * _softplus(x + bias_r[0]))
        xt = bdt[0, ch]
        g_rs.append(-jnp.exp(alog_c[0]) * _softplus(xt + bias_c[0]))
    parts_c = _split3(jnp.concatenate(g_cs, axis=1))
    parts_r = _split3(jnp.concatenate(g_rs, axis=0))
    gf_c = sum(_dot(low16, p) for p in parts_c)
    gb_c = sum(_dot(upp16, p) for p in parts_c)
    gf_r = sum(_dot(p, upp16) for p in parts_r)
    gb_r = sum(_dot(p, low16) for p in parts_r)
    nbd = 4 * nh
    gcum_cs = [(gf_c[:, ch * LANES:(ch + 1) * LANES], gb_c[:, ch * LANES:(ch + 1) * LANES]) for ch in range(n_ch)]
    gcum_rs = [(gf_r[ch * nbd:(ch + 1) * nbd, :], gb_r[ch * nbd:(ch + 1) * nbd, :]) for ch in range(n_ch)]
    heads = []
    for ch in range(n_ch):
        act = act_s[ch * cc:(ch + 1) * cc, :]
        for h in range(nh):
            qh = act[:, h * hd:(h + 1) * hd]
            kh = act[:, bw + h * hd:bw + (h + 1) * hd]
            vh = act[:, 2 * bw + h * hd:2 * bw + (h + 1) * hd]
            qh = qh * lax.rsqrt(jnp.sum(qh * qh, axis=-1, keepdims=True) + RMS_EPS) * (hd ** -0.5)
            kh = kh * lax.rsqrt(jnp.sum(kh * kh, axis=-1, keepdims=True) + RMS_EPS)
            heads.append((ch, h, qh, kh, vh))
    k16 = [kh.astype(BF16) for (_, _, _, kh, _) in heads]
    q16 = [qh.astype(BF16) for (_, _, qh, _, _) in heads]
    kks = [_dot_nt(kb, kb) for kb in k16]
    qks = [_dot_nt(qb, kb) for qb, kb in zip(q16, k16)]

    inst = []
    nmats = []
    for (ch, h, _, _, _), kk in zip(heads, kks):
        for d in range(2):
            col = 2 * nh + nh * d + h
            gi = gcum_cs[ch][d][:, col:col + 1]
            gj = gcum_rs[ch][d][col:col + 1, :]
            dm = jnp.where(incl[d], jnp.exp(jnp.where(incl[d], gi - gj, 0.0)), 0.0)
            beta = beta_cs[ch][:, nh * d + h:nh * d + h + 1]
            nmats.append(jnp.where(strict[d], beta * kk * dm, 0.0))
            inst.append((ch, h, d, gi, beta, dm))
    ainvs = _tri_inverses(nmats, eye, ii, jj)

    rhss = []
    for (ch, h, d, gi, beta, _), ainv in zip(inst, ainvs):
        _, _, _, kh, vh = heads[ch * nh + h]
        rhss.append(jnp.concatenate([vh * beta, kh * (beta * jnp.exp(gi))], axis=1).astype(BF16))
    a16 = [a.astype(BF16) for a in ainvs]
    sols = [_dot(a, r) for a, r in zip(a16, rhss)]

    sd_rows = [[[] for _ in range(2)] for _ in range(n_ch)]
    for (ch, h, d, gi, _, dm), sol in zip(inst, sols):
        _, _, qh, kh, _ = heads[ch * nh + h]
        rs = slice(ch * cc, (ch + 1) * cc)
        hs = slice(h * hd, (h + 1) * hd)
        glast = gi[last_row[d]:last_row[d] + 1, :]
        u_o[d, 0, rs, hs] = sol[:, 0:hd].astype(u_o.dtype)
        w_o[d, 0, rs, hs] = sol[:, hd:2 * hd].astype(w_o.dtype)
        qg_o[d, 0, rs, hs] = (qh * jnp.exp(gi)).astype(qg_o.dtype)
        kg_o[d, 0, rs, hs] = (kh * jnp.exp(glast - gi)).astype(kg_o.dtype)
        in_o[d, 0, rs, h * cc:(h + 1) * cc] = (qks[ch * nh + h] * dm).astype(in_o.dtype)
        sd_rows[ch][d].append(jnp.broadcast_to(jnp.exp(glast), (1, LANES)))
    for ch in range(n_ch):
        for d in range(2):
            sd_o[d, 0, ch] = jnp.concatenate(sd_rows[ch][d] + [jnp.zeros((8 - nh, LANES), F32)], axis=0)


def _deltanet_stage_a(pd, pbd, pbdt, dn_conv_w, alog_r, bias_r, alog_c, bias_c, layer, n_lat, n_tot):
    bsz, t, _ = pd.shape
    bw = dn_conv_w.shape[-1] // 3
    cc = DN_CHUNK
    rows = DNA_CHUNKS * cc
    main = lambda col: pl.BlockSpec((1, rows, bw), lambda b, i: (b, i, col))
    halos = [_halo_specs(rows, bw, col, t // HALO) for col in range(3)]
    small = lambda a: pl.BlockSpec((1,) + a.shape[1:], lambda b, i: (layer,) + (0,) * (a.ndim - 1))
    tok = lambda n, dt: jax.ShapeDtypeStruct((2, bsz, t, n), dt)
    tok_spec = lambda n: pl.BlockSpec((2, 1, rows, n), lambda b, i: (0, b, i, 0))
    return pl.pallas_call(
        functools.partial(_dna_kernel, n_lat=n_lat, n_tot=n_tot),
        grid=(bsz, n_tot),
        in_specs=[main(0), main(1), main(2),
                  halos[0][0], halos[1][0], halos[2][0], halos[0][1], halos[1][1], halos[2][1],
                  pl.BlockSpec((1, rows, LANES), lambda b, i: (b, i, 0)),
                  pl.BlockSpec((1, DNA_CHUNKS, 4 * DN_HEADS, cc), lambda b, i: (b, i, 0, 0)),
                  small(dn_conv_w), small(alog_r), small(bias_r), small(alog_c), small(bias_c)],
        out_specs=[tok_spec(bw), tok_spec(bw), tok_spec(bw), tok_spec(bw), tok_spec(DN_HEADS * cc),
                   pl.BlockSpec((2, 1, DNA_CHUNKS, 8, LANES), lambda b, i: (0, b, i, 0, 0))],
        out_shape=[tok(bw, BF16), tok(bw, BF16), tok(bw, BF16), tok(bw, BF16), tok(DN_HEADS * cc, BF16),
                   jax.ShapeDtypeStruct((2, bsz, t // cc, 8, LANES), F32)],
        scratch_shapes=[pltpu.VMEM((rows, 3 * bw), F32)],
        compiler_params=_params("parallel", "parallel"),
        name="deltanet_a",
    )(pd, pd, pd, pd, pd, pd, pd, pd, pd, pbd, pbdt, dn_conv_w, alog_r, bias_r, alog_c, bias_c)


def _dnb_kernel(qg_f, kg_f, u_f, w_f, in_f, sd_f, qg_b, kg_b, u_b, w_b, in_b, sd_b, of_ref, ob_ref, state):
    cc = DN_CHUNK
    hd = state.shape[-1]
    bsz = state.shape[1]

    @pl.when(pl.program_id(0) == 0)
    def _():
        state[...] = jnp.zeros_like(state)

    dirs = ((qg_f, kg_f, u_f, w_f, in_f, sd_f, of_ref), (qg_b, kg_b, u_b, w_b, in_b, sd_b, ob_ref))
    chains = [(d, b, h) for d in range(2) for b in range(bsz) for h in range(DN_HEADS)]
    hs = lambda h: slice(h * hd, (h + 1) * hd)
    n_ch = qg_f.shape[2] // cc
    s32 = [state[d, b, h] for d, b, h in chains]
    for step in range(n_ch):
        chunk = (step, n_ch - 1 - step)
        rs = [slice(c * cc, (c + 1) * cc) for c in chunk]
        s16 = [s.astype(BF16) for s in s32]
        ws = [_dot(dirs[d][3][0, b, rs[d], hs(h)], s) for (d, b, h), s in zip(chains, s16)]
        qs = [_dot(dirs[d][0][0, b, rs[d], hs(h)], s) for (d, b, h), s in zip(chains, s16)]
        vnew = [(dirs[d][2][0, b, rs[d], hs(h)].astype(F32) - x).astype(BF16) for (d, b, h), x in zip(chains, ws)]
        intra = [_dot(dirs[d][4][0, b, rs[d], h * cc:(h + 1) * cc], v) for (d, b, h), v in zip(chains, vnew)]
        upd = [_dot_tn(dirs[d][1][0, b, rs[d], hs(h)], v) for (d, b, h), v in zip(chains, vnew)]
        for (d, b, h), o1, o2 in zip(chains, qs, intra):
            dirs[d][6][b, rs[d], hs(h)] = (o1 + o2).astype(dirs[d][6].dtype)
        s32 = [s * dirs[d][5][0, b, chunk[d], h:h + 1, :] + up for (d, b, h), s, up in zip(chains, s32, upd)]
    for (d, b, h), s in zip(chains, s32):
        state[d, b, h] = s


def _deltanet_stage_b(qg, kg, u, w, intra, sd, n_lat, n_tot):
    _, bsz, t, bw = qg.shape
    rows = DNB_CHUNKS * DN_CHUNK
    hd = bw // DN_HEADS
    n_ctx = n_tot - n_lat
    block_f = lambda s: jnp.where(s < n_ctx, n_lat + s, s - n_ctx)
    block_b = lambda s: n_tot - 1 - s

    def specs(d, block):
        tok = lambda n: pl.BlockSpec((1, bsz, rows, n), lambda s: (d, 0, block(s), 0))
        return [tok(bw), tok(bw), tok(bw), tok(bw), tok(DN_HEADS * DN_CHUNK),
                pl.BlockSpec((1, bsz, DNB_CHUNKS, 8, LANES), lambda s: (d, 0, block(s), 0, 0))]

    out = lambda block: pl.BlockSpec((bsz, rows, bw), lambda s: (0, block(s), 0))
    args = (qg, kg, u, w, intra, sd)
    return pl.pallas_call(
        _dnb_kernel,
        grid=(n_tot,),
        in_specs=specs(0, block_f) + specs(1, block_b),
        out_specs=[out(block_f), out(block_b)],
        out_shape=[jax.ShapeDtypeStruct((bsz, t, bw), BF16)] * 2,
        scratch_shapes=[pltpu.VMEM((2, bsz, DN_HEADS, hd, hd), F32)],
        compiler_params=_params("arbitrary"),
        name="deltanet_b",
    )(*args, *args)


def _attn_kernel(q_ref, k_ref, v_ref, z_ref, lam_ref, g_ref, o_ref, qs, vx, m_s, acc, *, lam_init):
    tq = q_ref.shape[1]
    n_keys = k_ref.shape[1]
    dh = LANES // 2
    n_full, rem = divmod(n_keys, ATT_KEYS)

    @pl.when(pl.program_id(2) == 0)
    def _():
        vx[:, 0:LANES] = v_ref[0]
        vx[:, LANES:2 * LANES] = jnp.ones((n_keys, LANES), vx.dtype)

    q = q_ref[0].astype(F32)
    lane = lax.broadcasted_iota(jnp.int32, q.shape, 1)
    qs[0:tq, :] = jnp.where(lane < dh, q, 0.0).astype(qs.dtype)
    qs[tq:2 * tq, :] = jnp.where(lane >= dh, q, 0.0).astype(qs.dtype)
    m_s[...] = jnp.full_like(m_s, -jnp.inf)
    acc[...] = jnp.zeros_like(acc)

    key_blocks = [(j * ATT_KEYS, ATT_KEYS) for j in range(n_full)] + ([(n_full * ATT_KEYS, rem)] if rem else [])
    groups = [slice(r0, r0 + ATT_ROWS) for r0 in range(0, 2 * tq, ATT_ROWS)]
    tasks = [(k0, size, rows) for k0, size in key_blocks for rows in groups]
    scores = lambda k0, size, rows: _dot_nt(qs[rows, :], k_ref[0, k0:k0 + size, :])
    s_next = scores(*tasks[0])
    for i, (k0, size, rows) in enumerate(tasks):
        s = s_next
        if i + 1 < len(tasks):
            s_next = scores(*tasks[i + 1])
        m_prev = m_s[rows, :]
        m_new = jnp.maximum(m_prev, jnp.max(s, axis=1, keepdims=True))
        alpha = jnp.exp2(m_prev - m_new)
        p = jnp.exp2(s - m_new[:, 0:1])
        pv = _dot(p.astype(BF16), vx[k0:k0 + size, :])
        acc[rows, 0:LANES] = alpha * acc[rows, 0:LANES] + pv[:, 0:LANES]
        acc[rows, LANES:2 * LANES] = alpha * acc[rows, LANES:2 * LANES] + pv[:, LANES:2 * LANES]
        m_s[rows, :] = m_new

    lm = lam_ref[0]
    lam = (jnp.exp(jnp.sum(lm[0:1] * lm[1:2], axis=1, keepdims=True))
           - jnp.exp(jnp.sum(lm[2:3] * lm[3:4], axis=1, keepdims=True)) + lam_init)
    on = acc[:, 0:LANES] / acc[:, LANES:2 * LANES]
    o = on[0:tq] - lam * on[tq:2 * tq]
    y = o * lax.rsqrt(jnp.mean(o * o, axis=-1, keepdims=True) + RMS_EPS) * g_ref[0] * (1.0 - lam_init)
    o_ref[0] = (y * _silu(z_ref[0].astype(F32))).astype(o_ref.dtype)


def _diff_attention(qn, kn, pa, da_lambda, subln_g, layer, lam_init, *, tq, q_rows, q_off, k_rows, k_off):
    bsz = qn.shape[0]
    v_col = 0
    z_col = DA_HEADS
    qb, kb = q_off // tq, k_off // k_rows
    return pl.pallas_call(
        functools.partial(_attn_kernel, lam_init=lam_init),
        grid=(bsz, DA_HEADS, q_rows // tq),
        in_specs=[pl.BlockSpec((1, tq, LANES), lambda b, h, i: (b, i + qb, h)),
                  pl.BlockSpec((1, k_rows, LANES), lambda b, h, i: (b, kb, h)),
                  pl.BlockSpec((1, k_rows, LANES), lambda b, h, i: (b, kb, v_col + h)),
                  pl.BlockSpec((1, tq, LANES), lambda b, h, i: (b, i + qb, z_col + h)),
                  pl.BlockSpec((1,) + da_lambda.shape[1:], lambda b, h, i: (layer, 0, 0)),
                  pl.BlockSpec((1, 1, LANES), lambda b, h, i: (layer, 0, 0))],
        out_specs=pl.BlockSpec((1, tq, LANES), lambda b, h, i: (b, i, h)),
        out_shape=jax.ShapeDtypeStruct((bsz, q_rows, DA_HEADS * LANES), BF16),
        scratch_shapes=[pltpu.VMEM((2 * tq, LANES), BF16),
                        pltpu.VMEM((k_rows, 2 * LANES), BF16),
                        pltpu.VMEM((2 * tq, LANES), F32),
                        pltpu.VMEM((2 * tq, 2 * LANES), F32)],
        compiler_params=_params("parallel", "parallel", "arbitrary"),
        name="diff_attn",
    )(qn, kn, pa, pa, da_lambda, subln_g)


def _merge_kernel(x_ref, xc_ref, yc, of, ob, dz, ydl, ydc, mg, mod_ref, dng, wb, wo, o_ref, *,
                  n_lat_blocks, ctx_row):
    d = x_ref.shape[-1]
    b = pl.program_id(0)
    i = pl.program_id(1)
    is_ctx = i >= n_lat_blocks
    row = jnp.where(is_ctx, ctx_row, b)
    gate = mod_ref[pl.ds(row, 1), 2 * d:3 * d]
    o = of[0].astype(F32) + ob[0].astype(F32)
    hd = dng.shape[-1]
    parts = []
    for h in range(DN_HEADS):
        oh = o[:, h * hd:(h + 1) * hd]
        parts.append(oh * lax.rsqrt(jnp.mean(oh * oh, axis=-1, keepdims=True) + RMS_EPS) * dng[0])
    ydn = (jnp.concatenate(parts, axis=1) * _silu(dz[0].astype(F32))).astype(BF16)
    yda = jnp.where(is_ctx, ydc[0], ydl[0])
    merged = (jax.nn.sigmoid(mg[0, :, 0:d].astype(F32)) * _dot(yc[0], wb[0, 0])
              + jax.nn.sigmoid(mg[0, :, d:2 * d].astype(F32)) * _dot(ydn, wb[0, 1])
              + jax.nn.sigmoid(mg[0, :, 2 * d:3 * d].astype(F32)) * _dot(yda, wb[0, 2]))
    x = jnp.where(is_ctx, xc_ref[0], x_ref[0])
    o_ref[0] = x + gate * _dot(merged.astype(BF16), wo[0])


def _merge(tokens, y_conv, o_fwd, o_bwd, pd, yd_lat, yd_ctx, pm, mod, dn_norm_g, w_branch, w_out, layer,
           n_lat_blocks, n_blocks, ctx_row):
    x_lat, x_ctx, ctx_block0 = tokens
    bsz, _, d = x_lat.shape
    tm = TM_PROJ
    bw = y_conv.shape[-1]
    n_ctx_blocks = yd_ctx.shape[1] // tm
    tok = lambda n, col=0: pl.BlockSpec((1, tm, n), lambda b, i: (b, i, col))
    lat_spec, ctx_spec = _token_specs(tm, d, n_lat_blocks, ctx_block0, n_ctx_blocks)
    return pl.pallas_call(
        functools.partial(_merge_kernel, n_lat_blocks=n_lat_blocks, ctx_row=ctx_row),
        grid=(bsz, n_blocks),
        in_specs=[lat_spec, ctx_spec, tok(bw), tok(bw), tok(bw), tok(bw, 3),
                  pl.BlockSpec((1, tm, bw), lambda b, i: (b, jnp.minimum(i, n_lat_blocks - 1), 0)),
                  pl.BlockSpec((1, tm, bw),
                               lambda b, i: (b, jnp.clip(i - n_lat_blocks, 0, n_ctx_blocks - 1), 0)),
                  tok(N_BRANCH * d),
                  pl.BlockSpec(mod.shape, lambda b, i: (0, 0)),
                  pl.BlockSpec((1, 1, dn_norm_g.shape[-1]), lambda b, i: (layer, 0, 0)),
                  pl.BlockSpec((1,) + w_branch.shape[1:], lambda b, i: (layer, 0, 0, 0)),
                  pl.BlockSpec((1,) + w_out.shape[1:], lambda b, i: (layer, 0, 0))],
        out_specs=tok(d),
        out_shape=jax.ShapeDtypeStruct((bsz, n_blocks * tm, d), F32),
        compiler_params=_params("parallel", "parallel"),
        name="merge",
    )(x_lat, x_ctx, y_conv, o_fwd, o_bwd, pd, yd_lat, yd_ctx, pm, mod, dn_norm_g, w_branch, w_out)


def _rope_tables(seq, ctx_len, dh):
    n_freq = dh // 4
    inv_freq = ROPE_BASE ** (-jnp.arange(n_freq, dtype=F32) / n_freq)
    n_rows = seq // GRID_W
    row_ang = jnp.arange(n_rows, dtype=F32)[:, None] * inv_freq
    col_ang = jnp.arange(GRID_W, dtype=F32)[:, None] * inv_freq
    cos_r, sin_r, cos_c, sin_c = lax.optimization_barrier(
        (jnp.cos(row_ang), jnp.sin(row_ang), jnp.cos(col_ang), jnp.sin(col_ang)))

    def table(by_row, by_col):
        r = jnp.broadcast_to(by_row[:, None, :], (n_rows, GRID_W, n_freq)).reshape(seq, n_freq)
        c = jnp.broadcast_to(by_col[None, :, :], (n_rows, GRID_W, n_freq)).reshape(seq, n_freq)
        return jnp.tile(jnp.concatenate([r, c], axis=-1), (1, 2 * LANES // dh))

    cos = table(cos_r, cos_c)
    sin = table(sin_r, sin_c)
    first_half = (jnp.arange(LANES) % dh) < dh // 2
    sin_a = jnp.where(first_half, -sin, 0.0)
    sin_b = jnp.where(first_half, 0.0, sin)
    pad = lambda tbl, fill: jnp.concatenate([tbl, jnp.full((ctx_len, LANES), fill, F32)], axis=0)
    return pad(cos, 1.0), pad(sin_a, 0.0), pad(sin_b, 0.0)


def kernel(x, c, ctx, c_ctx, w_ada, b_ada, norm_g, w_in, conv_w, conv_b, conv_ln_g, conv_ln_b, dn_conv_w,
           dn_a_log, dn_dt_bias, dn_norm_g, da_q_norm_g, da_k_norm_g, da_lambda, da_subln_g, w_branch, w_out):
    bsz, seq, d = x.shape
    ctx_len = ctx.shape[1]
    depth = w_in.shape[0]
    bw = d // 2
    dh = bw // (2 * DA_HEADS)
    t = seq + ctx_len
    dna_rows = DNA_CHUNKS * DN_CHUNK
    dnb_rows = DNB_CHUNKS * DN_CHUNK
    assert conv_w.shape[1] == CONV_K and dn_conv_w.shape[1] == SHORT_K and w_branch.shape[1] == N_BRANCH
    assert seq % dnb_rows == 0 and ctx_len % dnb_rows == 0
    assert 2 * dh == LANES and bw // DN_HEADS == LANES and bsz + 1 <= 8
    assert seq % TQ == 0 and seq % ctx_len == 0 and seq % TM_PROJ == 0 and ctx_len % TM_PROJ == 0
    assert seq % dna_rows == 0 and ctx_len % dna_rows == 0 and (2 * TQ) % ATT_ROWS == 0
    n_lat_blocks, n_blocks = seq // TM_PROJ, t // TM_PROJ
    ctx_row = bsz

    tokens = (x, ctx, 0)
    cvec =jnp.concatenate([c, c_ctx[None, :], jnp.zeros((8 - bsz - 1, d), F32)], axis=0)
    cos, sin_a, sin_b = _rope_tables(seq, ctx_len, dh)

    e_conv, e_dn = 3 * bw, 7 * bw
    e_bd = e_dn + 4 * DN_HEADS
    e_da = e_bd + 4 * bw
    nbd = 4 * DN_HEADS
    row3 = lambda a: a.reshape(depth, 1, a.shape[-1])
    gate_row = lambda a: jnp.pad(a.reshape(depth, 1, 2 * DN_HEADS), ((0, 0), (0, 0), (2 * DN_HEADS, LANES - nbd)))
    gate_col = lambda a: jnp.pad(a.reshape(depth, 2 * DN_HEADS, 1), ((0, 0), (2 * DN_HEADS, 0), (0, 0)))
    alog_r, bias_r = gate_row(dn_a_log), gate_row(dn_dt_bias)
    alog_c, bias_c = gate_col(dn_a_log), gate_col(dn_dt_bias)
    tile2 = lambda a: row3(jnp.tile(a, (1, 2)))
    gq, gk = tile2(da_q_norm_g), tile2(da_k_norm_g)
    b_ada3, norm_g3 = row3(b_ada), row3(norm_g)
    conv_b3, ln_g3, ln_b3 = row3(conv_b), row3(conv_ln_g), row3(conv_ln_b)
    dng3, subln3 = row3(dn_norm_g), row3(da_subln_g)
    w_branch16, w_out16 = w_branch.astype(BF16), w_out.astype(BF16)

    spans = {"val": (0, bw), "glu": (bw, 2 * bw), "z": (2 * bw, e_conv), "dn": (e_conv, e_dn),
             "aq": (e_bd, e_bd + bw), "ak": (e_bd + bw, e_bd + 2 * bw), "avz": (e_bd + 2 * bw, e_da),
             "mg": (e_da, w_in.shape[2])}
    packed_order = ("dn", "avz", "mg", "val", "glu", "z", "aq", "ak")
    pieces, start = [], {}
    offset = 0
    for name in packed_order:
        lo, hi = spans[name]
        assert offset % (hi - lo) == 0
        start[name] = offset
        pieces.append(w_in[:, :, lo:hi])
        offset += hi - lo
    assert offset % LANES == 0
    start["bd"] = offset
    pieces.append(jnp.pad(w_in[:, :, e_dn:e_bd], ((0, 0), (0, 0), (0, LANES - nbd))))
    w_packed = jnp.concatenate(pieces, axis=-1).astype(BF16)
    width = lambda name: LANES if name == "bd" else spans[name][1] - spans[name][0]
    groups = [(width(name), start[name] // width(name))
              for name in ("val", "glu", "z", "dn", "bd", "aq", "ak", "avz", "mg")]
    weights = (w_packed, groups)

    for layer in range(depth):
        last = layer == depth - 1
        lam_init = 0.8 - 0.6 * math.exp(-0.3 * layer)
        mod = _adaln(cvec, w_ada, b_ada3, layer)
        y_conv, pd, pbd, qn, kn, pa, pm, pbdt = _inproj(tokens, t, mod, norm_g3, weights, conv_w, conv_b3, ln_g3, ln_b3,
                                                  cos, sin_a, sin_b, gq, gk, layer, n_lat_blocks, ctx_row)

        out_blocks = n_lat_blocks if last else n_blocks

        qg, kg, u, w, intra, sd = _deltanet_stage_a(pd, pbd, pbdt, dn_conv_w, alog_r, bias_r, alog_c, bias_c,
                                                    layer, seq // dna_rows, t // dna_rows)
        o_fwd, o_bwd = _deltanet_stage_b(qg, kg, u, w, intra, sd, seq // dnb_rows, t // dnb_rows)

        yd_lat = _diff_attention(qn, kn, pa, da_lambda, subln3, layer, lam_init,
                                 tq=TQ, q_rows=seq, q_off=0, k_rows=t, k_off=0)
        yd_ctx = _diff_attention(qn, kn, pa, da_lambda, subln3, layer, lam_init,
                                 tq=ctx_len, q_rows=ctx_len, q_off=seq, k_rows=ctx_len, k_off=seq)

        xs = _merge(tokens, y_conv, o_fwd, o_bwd, pd, yd_lat, yd_ctx, pm, mod, dng3, w_branch16, w_out16, layer,
                    n_lat_blocks, out_blocks, ctx_row)
        tokens = (xs, xs, n_lat_blocks)
    return xs
```

```python
import functools
import math

import jax
import jax.numpy as jnp
from jax import lax
from jax.experimental import pallas as pl
from jax.experimental.pallas import tpu as pltpu

F32 = jnp.float32
BF16 = jnp.bfloat16
HIGHEST = lax.Precision.HIGHEST

GRID_W = 64
N_BRANCH = 3
CONV_K = 31
DN_HEADS = 4
DN_CHUNK = 64
SHORT_K = 3
DA_HEADS = 4
ROPE_BASE = 10000.0
RMS_EPS = 1e-6
LN_EPS = 1e-5

LANES = 128
SUBLANES = 8
SUBLANES_BF16 = 16
HALO = SUBLANES_BF16
VMEM_LIMIT = 56 * 1024 * 1024

TM_PROJ = 256
PROJ_COLS = 512
CONV_ROWS = 64
DNA_CHUNKS = 4
DNB_CHUNKS = 4
TQ = 512
ATT_KEYS = 4096
ATT_ROWS = 512


def _silu(x):
    return x * jax.nn.sigmoid(x)


def _softplus(x):
    return jnp.maximum(x, 0.0) + jnp.log(1.0 + jnp.exp(-jnp.abs(x)))


def _dot(a, b):
    return jnp.dot(a, b, preferred_element_type=F32)


def _dot_hi(a, b):
    return jnp.dot(a, b, preferred_element_type=F32, precision=HIGHEST)


def _dot_nt(a, b):
    return lax.dot_general(a, b, (((1,), (1,)), ((), ())), preferred_element_type=F32)


def _dot_tn(a, b):
    return lax.dot_general(a, b, (((0,), (0,)), ((), ())), preferred_element_type=F32)


def _params(*sem):
    return pltpu.CompilerParams(dimension_semantics=sem, vmem_limit_bytes=VMEM_LIMIT)


def _adaln_kernel(c_ref, w_ref, b_ref, o_ref):
    o_ref[...] = _dot_hi(_silu(c_ref[...]), w_ref[0]) + b_ref[0]


def _adaln(cvec, w_ada, b_ada3, layer):
    d = cvec.shape[1]
    return pl.pallas_call(
        _adaln_kernel,
        grid=(3,),
        in_specs=[
            pl.BlockSpec((8, d), lambda j: (0, 0)),
            pl.BlockSpec((1, d, d), lambda j: (layer, 0, j)),
            pl.BlockSpec((1, 1, d), lambda j: (layer, 0, j)),
        ],
        out_specs=pl.BlockSpec((8, d), lambda j: (0, j)),
        out_shape=jax.ShapeDtypeStruct((8, 3 * d), F32),
        compiler_params=_params("parallel"),
        name="adaln",
    )(cvec, w_ada, b_ada3)


def _inproj_kernel(x_ref, xc_ref, xp_ref, xn_ref, mod_ref, g_ref, wv, wg, wz, wd, wb, wq, wk, wvz, wm,
                   cw_ref, cb_ref, lng, lnb, cos, sin_a, sin_b, gq, gk,
                   yc_o, od, ob, oq, ok, ovz, om, obt, hb_s, buf, zc_s, *, n_lat_blocks, n_blocks, ctx_row):
    d = x_ref.shape[-1]
    tm = x_ref.shape[1]
    b = pl.program_id(0)
    i = pl.program_id(1)
    row = jnp.where(i >= n_lat_blocks, ctx_row, b)
    m = mod_ref[pl.ds(row, 1), :]
    shift = m[:, 0:d]
    scale = m[:, d:2 * d]

    def modulated(x):
        ms = jnp.mean(x * x, axis=-1, keepdims=True)
        return ((x * lax.rsqrt(ms + RMS_EPS) * g_ref[0]) * (1.0 + scale) + shift).astype(BF16)

    hb_s[0:HALO, :] = modulated(xp_ref[0])
    hb_s[HALO:HALO + tm, :] = modulated(jnp.where(i >= n_lat_blocks, xc_ref[0], x_ref[0]))
    hb_s[HALO + tm:HALO + tm + HALO, :] = modulated(xn_ref[0])
    hb_ext = lambda: hb_s[...]
    hb = lambda: hb_s[HALO:HALO + tm, :]

    def project(w, o):
        n = w.shape[2]
        for c0 in range(0, n, PROJ_COLS):
            c1 = min(c0 + PROJ_COLS, n)
            o[0, :, c0:c1] = _dot(hb(), w[0, :, c0:c1]).astype(o.dtype)

    has_prev, has_next = _segment_edges(i, n_lat_blocks, n_blocks)
    a = _dot(hb_ext(), wv[0]) * jax.nn.sigmoid(_dot(hb_ext(), wg[0]))
    zc_s[...] = _dot(hb(), wz[0])
    buf[0, HALO:HALO + tm, :] = a[HALO:HALO + tm]
    buf[0, 0:HALO, :] = jnp.where(has_prev, a[0:HALO], 0.0)
    buf[0, HALO + tm:HALO + tm + HALO, :] = jnp.where(has_next, a[HALO + tm:HALO + tm + HALO], 0.0)
    span = buf.shape[1] - SUBLANES
    for j in range(1, SUBLANES):
        buf[j, 0:span, :] = buf[0, j:j + span, :]

    project(wd, od)

    logits = _dot(hb(), wb[0])
    ob[0] = logits
    logits_t = logits.T
    for c in range(tm // DN_CHUNK):
        obt[0, c] = logits_t[0:4 * DN_HEADS, c * DN_CHUNK:(c + 1) * DN_CHUNK]

    dh = LANES // 2
    first_map = lax.broadcasted_iota(jnp.int32, (tm, LANES), 1) < dh
    cs, sa, sb = cos[...], sin_a[...], sin_b[...]

    def norm_rope(x, gain, scale_):
        sq = x * x
        s0 = jnp.sum(jnp.where(first_map, sq, 0.0), axis=-1, keepdims=True)
        s1 = jnp.sum(jnp.where(first_map, 0.0, sq), axis=-1, keepdims=True)
        ms = jnp.where(first_map, s0, s1) * (1.0 / dh)
        y = x * lax.rsqrt(ms + RMS_EPS) * gain
        y = y * cs + pltpu.roll(y, LANES - dh // 2, 1) * sa + pltpu.roll(y, dh // 2, 1) * sb
        return y * scale_

    q_scale = dh ** -0.5 * math.log2(math.e)
    for h in range(DA_HEADS):
        hs = slice(h * LANES, (h + 1) * LANES)
        oq[0, :, hs] = norm_rope(_dot(hb(), wq[0, :, hs]), gq[0], q_scale).astype(oq.dtype)
        ok[0, :, hs] = norm_rope(_dot(hb(), wk[0, :, hs]), gk[0], 1.0).astype(ok.dtype)

    project(wvz, ovz)
    project(wm, om)

    pad = CONV_K // 2
    for r in range(0, tm, CONV_ROWS):
        acc = jnp.zeros((CONV_ROWS, buf.shape[2]), F32) + cb_ref[0]
        for k in range(CONV_K):
            whole, part = divmod(HALO - pad + k, SUBLANES)
            start = r + whole * SUBLANES
            acc = acc + buf[part, start:start + CONV_ROWS, :] * cw_ref[0, k:k + 1, :]
        mu = jnp.mean(acc, axis=-1, keepdims=True)
        xc = acc - mu
        y = xc * lax.rsqrt(jnp.mean(xc * xc, axis=-1, keepdims=True) + LN_EPS)
        y = y * lng[0] + lnb[0]
        yc_o[0, r:r + CONV_ROWS, :] = (_silu(y) * _silu(zc_s[r:r + CONV_ROWS, :])).astype(yc_o.dtype)


def _token_specs(tm, d, n_lat_blocks, ctx_block0, n_ctx_blocks):
    lat = pl.BlockSpec((1, tm, d), lambda b, i: (b, jnp.minimum(i, n_lat_blocks - 1), 0))
    ctx = pl.BlockSpec((1, tm, d),
                       lambda b, i: (b, ctx_block0 + jnp.clip(i - n_lat_blocks, 0, n_ctx_blocks - 1), 0))
    return lat, ctx


def _inproj(tokens, t, mod, norm_g, weights, conv_w, conv_b, ln_g, ln_b, cos, sin_a, sin_b, gq, gk, layer,
            n_lat_blocks, ctx_row):
    x_lat, x_ctx, ctx_block0 = tokens
    bsz, _, d = x_lat.shape
    tm = TM_PROJ
    nblk = t // tm
    w_packed, groups = weights
    bw = groups[0][0]
    resident = lambda width, blk: pl.BlockSpec((1, d, width), lambda b, i: (layer, 0, blk),
                                               pipeline_mode=pl.Buffered(1))
    out = lambda n: pl.BlockSpec((1, tm, n), lambda b, i: (b, i, 0))
    lat_spec, ctx_spec = _token_specs(tm, d, n_lat_blocks, ctx_block0, nblk - n_lat_blocks)
    assert nblk - n_lat_blocks == 1
    xp, xn = _halo_specs(tm, d, 0, n_lat_blocks * tm // HALO)
    vec = lambda n: pl.BlockSpec((1, 1, n), lambda b, i: (layer, 0, 0))
    tab = pl.BlockSpec((tm, LANES), lambda b, i: (i, 0))
    widths = (bw,) + tuple(width for width, _ in groups[3:])
    dtypes = (BF16, BF16, F32, BF16, BF16, BF16, BF16)
    return pl.pallas_call(
        functools.partial(_inproj_kernel, n_lat_blocks=n_lat_blocks, n_blocks=nblk, ctx_row=ctx_row),
        grid=(bsz, nblk),
        in_specs=[
            lat_spec, ctx_spec, xp, xn,
            pl.BlockSpec(mod.shape, lambda b, i: (0, 0)),
            vec(d),
            *[resident(width, blk) for width, blk in groups],
            pl.BlockSpec((1, CONV_K, bw), lambda b, i: (layer, 0, 0)), vec(bw), vec(bw), vec(bw),
            tab, tab, tab, vec(LANES), vec(LANES),
        ],
        out_specs=[out(n) for n in widths] + [
            pl.BlockSpec((1, tm // DN_CHUNK, 4 * DN_HEADS, DN_CHUNK), lambda b, i: (b, i, 0, 0))],
        out_shape=[jax.ShapeDtypeStruct((bsz, t, n), dt) for n, dt in zip(widths, dtypes)] + [
            jax.ShapeDtypeStruct((bsz, t // DN_CHUNK, 4 * DN_HEADS, DN_CHUNK), F32)],
        scratch_shapes=[pltpu.VMEM((tm + 2 * HALO, d), BF16),
                        pltpu.VMEM((SUBLANES, tm + 2 * HALO, bw), F32),
                        pltpu.VMEM((tm, bw), F32)],
        compiler_params=_params("parallel", "parallel"),
        name="inproj",
    )(x_lat, x_ctx, x_lat, x_lat, mod, norm_g, *[w_packed] * len(groups),
      conv_w, conv_b, ln_g, ln_b, cos, sin_a, sin_b, gq, gk)


def _segment_edges(i, n_lat, n_tot):
    has_prev = jnp.logical_and(i != 0, i != n_lat)
    has_next = jnp.logical_and(i != n_lat - 1, i != n_tot - 1)
    return has_prev, has_next


def _halo_specs(rows, width, col, n_halo_blocks):
    per = rows // HALO
    prev = pl.BlockSpec((1, HALO, width), lambda b, i: (b, jnp.maximum(i * per - 1, 0), col))
    nxt = pl.BlockSpec((1, HALO, width), lambda b, i: (b, jnp.minimum((i + 1) * per, n_halo_blocks - 1), col))
    return prev, nxt


def _split3(x):
    hi = x.astype(BF16)
    rest = x - hi.astype(F32)
    mid = rest.astype(BF16)
    lo = (rest - mid.astype(F32)).astype(BF16)
    return hi, mid, lo


def _tri_inverses(nmats, eye, ii, jj):
    b16 = lambda a: a.astype(BF16)
    same = lambda size: (ii // size) == (jj // size)
    ts = [eye - jnp.where(same(2), n, 0.0) for n in nmats]
    size = 2
    while size < nmats[0].shape[0]:
        level = jnp.logical_and(same(2 * size), jnp.logical_not(same(size)))
        cs = [b16(jnp.where(level, n, 0.0)) for n in nmats]
        tbs = [b16(t) for t in ts]
        mids = [_dot(c, t) for c, t in zip(cs, tbs)]
        mids = [b16(m) for m in mids]
        prods = [_dot(t, m) for t, m in zip(tbs, mids)]
        ts = [t - pr for t, pr in zip(ts, prods)]
        size *= 2
    return ts


def _dna_kernel(q, k, v, pq, pk, pv, nq, nk, nv, bd, bdt, cw, alog_r, bias_r, alog_c, bias_c,
                qg_o, kg_o, u_o, w_o, in_o, sd_o, act_s, *, n_lat, n_tot):
    i = pl.program_id(1)
    has_prev, has_next = _segment_edges(i, n_lat, n_tot)
    cc = DN_CHUNK
    rows = q.shape[1]
    bw = q.shape[-1]
    hd = bw // DN_HEADS
    nh = DN_HEADS
    ri = lax.broadcasted_iota(jnp.int32, (rows, rows), 0)
    rj = lax.broadcasted_iota(jnp.int32, (rows, rows), 1)
    shift_dn = (rj == ri - 1).astype(BF16)
    shift_up = (rj == ri + 1).astype(BF16)
    sub = lax.broadcasted_iota(jnp.int32, (SUBLANES, bw), 0)
    for idx, (m_, p_, n_) in enumerate(((q, pq, nq), (k, pk, nk), (v, pv, nv))):
        cols = slice(idx * bw, (idx + 1) * bw)
        x16 = m_[0]
        prev = _dot(shift_dn, x16)
        nxt = _dot(shift_up, x16)
        halo_prev = jnp.where(has_prev, p_[0, HALO - 1:HALO, :].astype(F32), 0.0)
        halo_next = jnp.where(has_next, n_[0, 0:1, :].astype(F32), 0.0)
        prev = jnp.concatenate([jnp.where(sub == 0, halo_prev, prev[0:SUBLANES]), prev[SUBLANES:]], axis=0)
        nxt = jnp.concatenate([nxt[0:rows - SUBLANES],
                               jnp.where(sub == SUBLANES - 1, halo_next, nxt[rows - SUBLANES:])], axis=0)
        conv = x16.astype(F32) * cw[0, 1:2, cols] + prev * cw[0, 0:1, cols] + nxt * cw[0, 2:3, cols]
        act_s[:, cols] = _silu(conv)

    ii = lax.broadcasted_iota(jnp.int32, (cc, cc), 0)
    jj = lax.broadcasted_iota(jnp.int32, (cc, cc), 1)
    low = jj <= ii
    upp = jj >= ii
    low16 = low.astype(BF16)
    upp16 = upp.astype(BF16)
    eye = (ii == jj).astype(F32)
    incl = (low, upp)
    strict = (jj < ii, jj > ii)
    last_row = (cc - 1, 0)

    n_ch = rows // cc
    beta_cs, g_cs, g_rs = [], [], []
    for ch in range(n_ch):
        x = bd[0, ch * cc:(ch + 1) * cc, :]
        beta_cs.append(jax.nn.sigmoid(x))
        g_cs.append(-jnp.exp(alog_r[0]) * _softplus(x + bias_r[0]))
        xt = bdt[0, ch]
        g_rs.append(-jnp.exp(alog_c[0]) * _softplus(xt + bias_c[0]))
    parts_c = _split3(jnp.concatenate(g_cs, axis=1))
    parts_r = _split3(jnp.concatenate(g_rs, axis=0))
    gf_c = sum(_dot(low16, p) for p in parts_c)
    gb_c = sum(_dot(upp16, p) for p in parts_c)
    gf_r = sum(_dot(p, upp16) for p in parts_r)
    gb_r = sum(_dot(p, low16) for p in parts_r)
    nbd = 4 * nh
    gcum_cs = [(gf_c[:, ch * LANES:(ch + 1) * LANES], gb_c[:, ch * LANES:(ch + 1) * LANES]) for ch in range(n_ch)]
    gcum_rs = [(gf_r[ch * nbd:(ch + 1) * nbd, :], gb_r[ch * nbd:(ch + 1) * nbd, :]) for ch in range(n_ch)]
    heads = []
    for ch in range(n_ch):
        act = act_s[ch * cc:(ch + 1) * cc, :]
        for h in range(nh):
            qh = act[:, h * hd:(h + 1) * hd]
            kh = act[:, bw + h * hd:bw + (h + 1) * hd]
            vh = act[:, 2 * bw + h * hd:2 * bw + (h + 1) * hd]
            qh = qh * lax.rsqrt(jnp.sum(qh * qh, axis=-1, keepdims=True) + RMS_EPS) * (hd ** -0.5)
            kh = kh * lax.rsqrt(jnp.sum(kh * kh, axis=-1, keepdims=True) + RMS_EPS)
            heads.append((ch, h, qh, kh, vh))
    k16 = [kh.astype(BF16) for (_, _, _, kh, _) in heads]
    q16 = [qh.astype(BF16) for (_, _, qh, _, _) in heads]
    kks = [_dot_nt(kb, kb) for kb in k16]
    qks = [_dot_nt(qb, kb) for qb, kb in zip(q16, k16)]

    inst = []
    nmats = []
    for (ch, h, _, _, _), kk in zip(heads, kks):
        for d in range(2):
            col = 2 * nh + nh * d + h
            gi = gcum_cs[ch][d][:, col:col + 1]
            gj = gcum_rs[ch][d][col:col + 1, :]
            dm = jnp.where(incl[d], jnp.exp(jnp.where(incl[d], gi - gj, 0.0)), 0.0)
            beta = beta_cs[ch][:, nh * d + h:nh * d + h + 1]
            nmats.append(jnp.where(strict[d], beta * kk * dm, 0.0))
            inst.append((ch, h, d, gi, beta, dm))
    ainvs = _tri_inverses(nmats, eye, ii, jj)

    rhss = []
    for (ch, h, d, gi, beta, _), ainv in zip(inst, ainvs):
        _, _, _, kh, vh = heads[ch * nh + h]
        rhss.append(jnp.concatenate([vh * beta, kh * (beta * jnp.exp(gi))], axis=1).astype(BF16))
    a16 = [a.astype(BF16) for a in ainvs]
    sols = [_dot(a, r) for a, r in zip(a16, rhss)]

    sd_rows = [[[] for _ in range(2)] for _ in range(n_ch)]
    for (ch, h, d, gi, _, dm), sol in zip(inst, sols):
        _, _, qh, kh, _ = heads[ch * nh + h]
        rs = slice(ch * cc, (ch + 1) * cc)
        hs = slice(h * hd, (h + 1) * hd)
        glast = gi[last_row[d]:last_row[d] + 1, :]
        u_o[d, 0, rs, hs] = sol[:, 0:hd].astype(u_o.dtype)
        w_o[d, 0, rs, hs] = sol[:, hd:2 * hd].astype(w_o.dtype)
        qg_o[d, 0, rs, hs] = (qh * jnp.exp(gi)).astype(qg_o.dtype)
        kg_o[d, 0, rs, hs] = (kh * jnp.exp(glast - gi)).astype(kg_o.dtype)
        in_o[d, 0, rs, h * cc:(h + 1) * cc] = (qks[ch * nh + h] * dm).astype(in_o.dtype)
        sd_rows[ch][d].append(jnp.broadcast_to(jnp.exp(glast), (1, LANES)))
    for ch in range(n_ch):
        for d in range(2):
            sd_o[d, 0, ch] = jnp.concatenate(sd_rows[ch][d] + [jnp.zeros((8 - nh, LANES), F32)], axis=0)


def _deltanet_stage_a(pd, pbd, pbdt, dn_conv_w, alog_r, bias_r, alog_c, bias_c, layer, n_lat, n_tot):
    bsz, t, _ = pd.shape
    bw = dn_conv_w.shape[-1] // 3
    cc = DN_CHUNK
    rows = DNA_CHUNKS * cc
    main = lambda col: pl.BlockSpec((1, rows, bw), lambda b, i: (b, i, col))
    halos = [_halo_specs(rows, bw, col, t // HALO) for col in range(3)]
    small = lambda a: pl.BlockSpec((1,) + a.shape[1:], lambda b, i: (layer,) + (0,) * (a.ndim - 1))
    tok = lambda n, dt: jax.ShapeDtypeStruct((2, bsz, t, n), dt)
    tok_spec = lambda n: pl.BlockSpec((2, 1, rows, n), lambda b, i: (0, b, i, 0))
    return pl.pallas_call(
        functools.partial(_dna_kernel, n_lat=n_lat, n_tot=n_tot),
        grid=(bsz, n_tot),
        in_specs=[main(0), main(1), main(2),
                  halos[0][0], halos[1][0], halos[2][0], halos[0][1], halos[1][1], halos[2][1],
                  pl.BlockSpec((1, rows, LANES), lambda b, i: (b, i, 0)),
                  pl.BlockSpec((1, DNA_CHUNKS, 4 * DN_HEADS, cc), lambda b, i: (b, i, 0, 0)),
                  small(dn_conv_w), small(alog_r), small(bias_r), small(alog_c), small(bias_c)],
        out_specs=[tok_spec(bw), tok_spec(bw), tok_spec(bw), tok_spec(bw), tok_spec(DN_HEADS * cc),
                   pl.BlockSpec((2, 1, DNA_CHUNKS, 8, LANES), lambda b, i: (0, b, i, 0, 0))],
        out_shape=[tok(bw, BF16), tok(bw, BF16), tok(bw, BF16), tok(bw, BF16), tok(DN_HEADS * cc, BF16),
                   jax.ShapeDtypeStruct((2, bsz, t // cc, 8, LANES), F32)],
        scratch_shapes=[pltpu.VMEM((rows, 3 * bw), F32)],
        compiler_params=_params("parallel", "parallel"),
        name="deltanet_a",
    )(pd, pd, pd, pd, pd, pd, pd, pd, pd, pbd, pbdt, dn_conv_w, alog_r, bias_r, alog_c, bias_c)


def _dnb_kernel(qg_f, kg_f, u_f, w_f, in_f, sd_f, qg_b, kg_b, u_b, w_b, in_b, sd_b, of_ref, ob_ref, state):
    cc = DN_CHUNK
    hd = state.shape[-1]
    bsz = state.shape[1]

    @pl.when(pl.program_id(0) == 0)
    def _():
        state[...] = jnp.zeros_like(state)

    dirs = ((qg_f, kg_f, u_f, w_f, in_f, sd_f, of_ref), (qg_b, kg_b, u_b, w_b, in_b, sd_b, ob_ref))
    chains = [(d, b, h) for d in range(2) for b in range(bsz) for h in range(DN_HEADS)]
    hs = lambda h: slice(h * hd, (h + 1) * hd)
    n_ch = qg_f.shape[2] // cc
    s32 = [state[d, b, h] for d, b, h in chains]
    for step in range(n_ch):
        chunk = (step, n_ch - 1 - step)
        rs = [slice(c * cc, (c + 1) * cc) for c in chunk]
        s16 = [s.astype(BF16) for s in s32]
        ws = [_dot(dirs[d][3][0, b, rs[d], hs(h)], s) for (d, b, h), s in zip(chains, s16)]
        qs = [_dot(dirs[d][0][0, b, rs[d], hs(h)], s) for (d, b, h), s in zip(chains, s16)]
        vnew = [(dirs[d][2][0, b, rs[d], hs(h)].astype(F32) - x).astype(BF16) for (d, b, h), x in zip(chains, ws)]
        intra = [_dot(dirs[d][4][0, b, rs[d], h * cc:(h + 1) * cc], v) for (d, b, h), v in zip(chains, vnew)]
        upd = [_dot_tn(dirs[d][1][0, b, rs[d], hs(h)], v) for (d, b, h), v in zip(chains, vnew)]
        for (d, b, h), o1, o2 in zip(chains, qs, intra):
            dirs[d][6][b, rs[d], hs(h)] = (o1 + o2).astype(dirs[d][6].dtype)
        s32 = [s * dirs[d][5][0, b, chunk[d], h:h + 1, :] + up for (d, b, h), s, up in zip(chains, s32, upd)]
    for (d, b, h), s in zip(chains, s32):
        state[d, b, h] = s


def _deltanet_stage_b(qg, kg, u, w, intra, sd, n_lat, n_tot):
    _, bsz, t, bw = qg.shape
    rows = DNB_CHUNKS * DN_CHUNK
    hd = bw // DN_HEADS
    n_ctx = n_tot - n_lat
    block_f = lambda s: jnp.where(s < n_ctx, n_lat + s, s - n_ctx)
    block_b = lambda s: n_tot - 1 - s

    def specs(d, block):
        tok = lambda n: pl.BlockSpec((1, bsz, rows, n), lambda s: (d, 0, block(s), 0))
        return [tok(bw), tok(bw), tok(bw), tok(bw), tok(DN_HEADS * DN_CHUNK),
                pl.BlockSpec((1, bsz, DNB_CHUNKS, 8, LANES), lambda s: (d, 0, block(s), 0, 0))]

    out = lambda block: pl.BlockSpec((bsz, rows, bw), lambda s: (0, block(s), 0))
    args = (qg, kg, u, w, intra, sd)
    return pl.pallas_call(
        _dnb_kernel,
        grid=(n_tot,),
        in_specs=specs(0, block_f) + specs(1, block_b),
        out_specs=[out(block_f), out(block_b)],
        out_shape=[jax.ShapeDtypeStruct((bsz, t, bw), BF16)] * 2,
        scratch_shapes=[pltpu.VMEM((2, bsz, DN_HEADS, hd, hd), F32)],
        compiler_params=_params("arbitrary"),
        name="deltanet_b",
    )(*args, *args)


def _attn_kernel(q_ref, k_ref, v_ref, z_ref, lam_ref, g_ref, o_ref, qs, vx, m_s, acc, *, lam_init):
    tq = q_ref.shape[1]
    n_keys = k_ref.shape[1]
    dh = LANES // 2
    n_full, rem = divmod(n_keys, ATT_KEYS)

    @pl.when(pl.program_id(2) == 0)
    def _():
        vx[:, 0:LANES] = v_ref[0]
        vx[:, LANES:2 * LANES] = jnp.ones((n_keys, LANES), vx.dtype)

    q = q_ref[0].astype(F32)
    lane = lax.broadcasted_iota(jnp.int32, q.shape, 1)
    qs[0:tq, :] = jnp.where(lane < dh, q, 0.0).astype(qs.dtype)
    qs[tq:2 * tq, :] = jnp.where(lane >= dh, q, 0.0).astype(qs.dtype)
    m_s[...] = jnp.full_like(m_s, -jnp.inf)
    acc[...] = jnp.zeros_like(acc)

    key_blocks = [(j * ATT_KEYS, ATT_KEYS) for j in range(n_full)] + ([(n_full * ATT_KEYS, rem)] if rem else [])
    groups = [slice(r0, r0 + ATT_ROWS) for r0 in range(0, 2 * tq, ATT_ROWS)]
    tasks = [(k0, size, rows) for k0, size in key_blocks for rows in groups]
    scores = lambda k0, size, rows: _dot_nt(qs[rows, :], k_ref[0, k0:k0 + size, :])
    s_next = scores(*tasks[0])
    for i, (k0, size, rows) in enumerate(tasks):
        s = s_next
        if i + 1 < len(tasks):
            s_next = scores(*tasks[i + 1])
        m_prev = m_s[rows, :]
        m_new = jnp.maximum(m_prev, jnp.max(s, axis=1, keepdims=True))
        alpha = jnp.exp2(m_prev - m_new)
        p = jnp.exp2(s - m_new[:, 0:1])
        pv = _dot(p.astype(BF16), vx[k0:k0 + size, :])
        acc[rows, 0:LANES] = alpha * acc[rows, 0:LANES] + pv[:, 0:LANES]
        acc[rows, LANES:2 * LANES] = alpha * acc[rows, LANES:2 * LANES] + pv[:, LANES:2 * LANES]
        m_s[rows, :] = m_new

    lm = lam_ref[0]
    lam = (jnp.exp(jnp.sum(lm[0:1] * lm[1:2], axis=1, keepdims=True))
           - jnp.exp(jnp.sum(lm[2:3] * lm[3:4], axis=1, keepdims=True)) + lam_init)
    on = acc[:, 0:LANES] / acc[:, LANES:2 * LANES]
    o = on[0:tq] - lam * on[tq:2 * tq]
    y = o * lax.rsqrt(jnp.mean(o * o, axis=-1, keepdims=True) + RMS_EPS) * g_ref[0] * (1.0 - lam_init)
    o_ref[0] = (y * _silu(z_ref[0].astype(F32))).astype(o_ref.dtype)


def _diff_attention(qn, kn, pa, da_lambda, subln_g, layer, lam_init, *, tq, q_rows, q_off, k_rows, k_off):
    bsz = qn.shape[0]
    v_col = 0
    z_col = DA_HEADS
    qb, kb = q_off // tq, k_off // k_rows
    return pl.pallas_call(
        functools.partial(_attn_kernel, lam_init=lam_init),
        grid=(bsz, DA_HEADS, q_rows // tq),
        in_specs=[pl.BlockSpec((1, tq, LANES), lambda b, h, i: (b, i + qb, h)),
                  pl.BlockSpec((1, k_rows, LANES), lambda b, h, i: (b, kb, h)),
                  pl.BlockSpec((1, k_rows, LANES), lambda b, h, i: (b, kb, v_col + h)),
                  pl.BlockSpec((1, tq, LANES), lambda b, h, i: (b, i + qb, z_col + h)),
                  pl.BlockSpec((1,) + da_lambda.shape[1:], lambda b, h, i: (layer, 0, 0)),
                  pl.BlockSpec((1, 1, LANES), lambda b, h, i: (layer, 0, 0))],
        out_specs=pl.BlockSpec((1, tq, LANES), lambda b, h, i: (b, i, h)),
        out_shape=jax.ShapeDtypeStruct((bsz, q_rows, DA_HEADS * LANES), BF16),
        scratch_shapes=[pltpu.VMEM((2 * tq, LANES), BF16),
                        pltpu.VMEM((k_rows, 2 * LANES), BF16),
                        pltpu.VMEM((2 * tq, LANES), F32),
                        pltpu.VMEM((2 * tq, 2 * LANES), F32)],
        compiler_params=_params("parallel", "parallel", "arbitrary"),
        name="diff_attn",
    )(qn, kn, pa, pa, da_lambda, subln_g)


def _merge_kernel(x_ref, xc_ref, yc, of, ob, dz, ydl, ydc, mg, mod_ref, dng, wb, wo, o_ref, *,
                  n_lat_blocks, ctx_row):
    d = x_ref.shape[-1]
    b = pl.program_id(0)
    i = pl.program_id(1)
    is_ctx = i >= n_lat_blocks
    row = jnp.where(is_ctx, ctx_row, b)
    gate = mod_ref[pl.ds(row, 1), 2 * d:3 * d]
    o = of[0].astype(F32) + ob[0].astype(F32)
    hd = dng.shape[-1]
    parts = []
    for h in range(DN_HEADS):
        oh = o[:, h * hd:(h + 1) * hd]
        parts.append(oh * lax.rsqrt(jnp.mean(oh * oh, axis=-1, keepdims=True) + RMS_EPS) * dng[0])
    ydn = (jnp.concatenate(parts, axis=1) * _silu(dz[0].astype(F32))).astype(BF16)
    yda = jnp.where(is_ctx, ydc[0], ydl[0])
    merged = (jax.nn.sigmoid(mg[0, :, 0:d].astype(F32)) * _dot(yc[0], wb[0, 0])
              + jax.nn.sigmoid(mg[0, :, d:2 * d].astype(F32)) * _dot(ydn, wb[0, 1])
              + jax.nn.sigmoid(mg[0, :, 2 * d:3 * d].astype(F32)) * _dot(yda, wb[0, 2]))
    x = jnp.where(is_ctx, xc_ref[0], x_ref[0])
    o_ref[0] = x + gate * _dot(merged.astype(BF16), wo[0])


def _merge(tokens, y_conv, o_fwd, o_bwd, pd, yd_lat, yd_ctx, pm, mod, dn_norm_g, w_branch, w_out, layer,
           n_lat_blocks, n_blocks, ctx_row):
    x_lat, x_ctx, ctx_block0 = tokens
    bsz, _, d = x_lat.shape
    tm = TM_PROJ
    bw = y_conv.shape[-1]
    n_ctx_blocks = yd_ctx.shape[1] // tm
    tok = lambda n, col=0: pl.BlockSpec((1, tm, n), lambda b, i: (b, i, col))
    lat_spec, ctx_spec = _token_specs(tm, d, n_lat_blocks, ctx_block0, n_ctx_blocks)
    return pl.pallas_call(
        functools.partial(_merge_kernel, n_lat_blocks=n_lat_blocks, ctx_row=ctx_row),
        grid=(bsz, n_blocks),
        in_specs=[lat_spec, ctx_spec, tok(bw), tok(bw), tok(bw), tok(bw, 3),
                  pl.BlockSpec((1, tm, bw), lambda b, i: (b, jnp.minimum(i, n_lat_blocks - 1), 0)),
                  pl.BlockSpec((1, tm, bw),
                               lambda b, i: (b, jnp.clip(i - n_lat_blocks, 0, n_ctx_blocks - 1), 0)),
                  tok(N_BRANCH * d),
                  pl.BlockSpec(mod.shape, lambda b, i: (0, 0)),
                  pl.BlockSpec((1, 1, dn_norm_g.shape[-1]), lambda b, i: (layer, 0, 0)),
                  pl.BlockSpec((1,) + w_branch.shape[1:], lambda b, i: (layer, 0, 0, 0)),
                  pl.BlockSpec((1,) + w_out.shape[1:], lambda b, i: (layer, 0, 0))],
        out_specs=tok(d),
        out_shape=jax.ShapeDtypeStruct((bsz, n_blocks * tm, d), F32),
        compiler_params=_params("parallel", "parallel"),
        name="merge",
    )(x_lat, x_ctx, y_conv, o_fwd, o_bwd, pd, yd_lat, yd_ctx, pm, mod, dn_norm_g, w_branch, w_out)


def _rope_tables(seq, ctx_len, dh):
    n_freq = dh // 4
    inv_freq = ROPE_BASE ** (-jnp.arange(n_freq, dtype=F32) / n_freq)
    n_rows = seq // GRID_W
    row_ang = jnp.arange(n_rows, dtype=F32)[:, None] * inv_freq
    col_ang = jnp.arange(GRID_W, dtype=F32)[:, None] * inv_freq
    cos_r, sin_r, cos_c, sin_c = lax.optimization_barrier(
        (jnp.cos(row_ang), jnp.sin(row_ang), jnp.cos(col_ang), jnp.sin(col_ang)))

    def table(by_row, by_col):
        r = jnp.broadcast_to(by_row[:, None, :], (n_rows, GRID_W, n_freq)).reshape(seq, n_freq)
        c = jnp.broadcast_to(by_col[None, :, :], (n_rows, GRID_W, n_freq)).reshape(seq, n_freq)
        return jnp.tile(jnp.concatenate([r, c], axis=-1), (1, 2 * LANES // dh))

    cos = table(cos_r, cos_c)
    sin = table(sin_r, sin_c)
    first_half = (jnp.arange(LANES) % dh) < dh // 2
    sin_a = jnp.where(first_half, -sin, 0.0)
    sin_b = jnp.where(first_half, 0.0, sin)
    pad = lambda tbl, fill: jnp.concatenate([tbl, jnp.full((ctx_len, LANES), fill, F32)], axis=0)
    return pad(cos, 1.0), pad(sin_a, 0.0), pad(sin_b, 0.0)


def kernel(x, c, ctx, c_ctx, w_ada, b_ada, norm_g, w_in, conv_w, conv_b, conv_ln_g, conv_ln_b, dn_conv_w,
           dn_a_log, dn_dt_bias, dn_norm_g, da_q_norm_g, da_k_norm_g, da_lambda, da_subln_g, w_branch, w_out):
    bsz, seq, d = x.shape
    ctx_len = ctx.shape[1]
    depth = w_in.shape[0]
    bw = d // 2
    dh = bw // (2 * DA_HEADS)
    t = seq + ctx_len
    dna_rows = DNA_CHUNKS * DN_CHUNK
    dnb_rows = DNB_CHUNKS * DN_CHUNK
    assert conv_w.shape[1] == CONV_K and dn_conv_w.shape[1] == SHORT_K and w_branch.shape[1] == N_BRANCH
    assert seq % dnb_rows == 0 and ctx_len % dnb_rows == 0
    assert 2 * dh == LANES and bw // DN_HEADS == LANES and bsz + 1 <= 8
    assert seq % TQ == 0 and seq % ctx_len == 0 and seq % TM_PROJ == 0 and ctx_len % TM_PROJ == 0
    assert seq % dna_rows == 0 and ctx_len % dna_rows == 0 and (2 * TQ) % ATT_ROWS == 0
    n_lat_blocks, n_blocks = seq // TM_PROJ, t // TM_PROJ
    ctx_row = bsz

    tokens = (x, ctx, 0)
    cvec =jnp.concatenate([c, c_ctx[None, :], jnp.zeros((8 - bsz - 1, d), F32)], axis=0)
    cos, sin_a, sin_b = _rope_tables(seq, ctx_len, dh)

    e_conv, e_dn = 3 * bw, 7 * bw
    e_bd = e_dn + 4 * DN_HEADS
    e_da = e_bd + 4 * bw
    nbd = 4 * DN_HEADS
    row3 = lambda a: a.reshape(depth, 1, a.shape[-1])
    gate_row = lambda a: jnp.pad(a.reshape(depth, 1, 2 * DN_HEADS), ((0, 0), (0, 0), (2 * DN_HEADS, LANES - nbd)))
    gate_col = lambda a: jnp.pad(a.reshape(depth, 2 * DN_HEADS, 1), ((0, 0), (2 * DN_HEADS, 0), (0, 0)))
    alog_r, bias_r = gate_row(dn_a_log), gate_row(dn_dt_bias)
    alog_c, bias_c = gate_col(dn_a_log), gate_col(dn_dt_bias)
    tile2 = lambda a: row3(jnp.tile(a, (1, 2)))
    gq, gk = tile2(da_q_norm_g), tile2(da_k_norm_g)
    b_ada3, norm_g3 = row3(b_ada), row3(norm_g)
    conv_b3, ln_g3, ln_b3 = row3(conv_b), row3(conv_ln_g), row3(conv_ln_b)
    dng3, subln3 = row3(dn_norm_g), row3(da_subln_g)
    w_branch16, w_out16 = w_branch.astype(BF16), w_out.astype(BF16)

    spans = {"val": (0, bw), "glu": (bw, 2 * bw), "z": (2 * bw, e_conv), "dn": (e_conv, e_dn),
             "aq": (e_bd, e_bd + bw), "ak": (e_bd + bw, e_bd + 2 * bw), "avz": (e_bd + 2 * bw, e_da),
             "mg": (e_da, w_in.shape[2])}
    packed_order = ("dn", "avz", "mg", "val", "glu", "z", "aq", "ak")
    pieces, start = [], {}
    offset = 0
    for name in packed_order:
        lo, hi = spans[name]
        assert offset % (hi - lo) == 0
        start[name] = offset
        pieces.append(w_in[:, :, lo:hi].astype(BF16))
        offset += hi - lo
    assert offset % LANES == 0
    start["bd"] = offset
    pieces.append(jnp.pad(w_in[:, :, e_dn:e_bd].astype(BF16), ((0, 0), (0, 0), (0, LANES - nbd))))
    w_packed = jnp.concatenate(pieces, axis=-1)
    width = lambda name: LANES if name == "bd" else spans[name][1] - spans[name][0]
    groups = [(width(name), start[name] // width(name))
              for name in ("val", "glu", "z", "dn", "bd", "aq", "ak", "avz", "mg")]
    weights = (w_packed, groups)

    for layer in range(depth):
        last = layer == depth - 1
        lam_init = 0.8 - 0.6 * math.exp(-0.3 * layer)
        mod = _adaln(cvec, w_ada, b_ada3, layer)
        y_conv, pd, pbd, qn, kn, pa, pm, pbdt = _inproj(tokens, t, mod, norm_g3, weights, conv_w, conv_b3, ln_g3, ln_b3,
                                                  cos, sin_a, sin_b, gq, gk, layer, n_lat_blocks, ctx_row)

        out_blocks = n_lat_blocks if last else n_blocks

        qg, kg, u, w, intra, sd = _deltanet_stage_a(pd, pbd, pbdt, dn_conv_w, alog_r, bias_r, alog_c, bias_c,
                                                    layer, seq // dna_rows, t // dna_rows)
        o_fwd, o_bwd = _deltanet_stage_b(qg, kg, u, w, intra, sd, seq // dnb_rows, t // dnb_rows)

        yd_lat = _diff_attention(qn, kn, pa, da_lambda, subln3, layer, lam_init,
                                 tq=TQ, q_rows=seq, q_off=0, k_rows=t, k_off=0)
        yd_ctx = _diff_attention(qn, kn, pa, da_lambda, subln3, layer, lam_init,
                                 tq=ctx_len, q_rows=ctx_len, q_off=seq, k_rows=ctx_len, k_off=seq)

        xs = _merge(tokens, y_conv, o_fwd, o_bwd, pd, yd_lat, yd_ctx, pm, mod, dng3, w_branch16, w_out16, layer,
                    n_lat_blocks, out_blocks, ctx_row)
        tokens = (xs, xs, n_lat_blocks)
    return xs
```
